```python
import math
import jax, jax.numpy as jnp
from jax import lax
import numpy as np

D_MODEL = 1024
BATCH = 8
SEQ = 2048
DEPTH = 4
DEC_BATCH = 128
DEC_SEQ = 4
PAST_LEN = 16384
PAGE_SIZE = 128

MIX_A = D_MODEL // 2
SSM_GROUP = 16
NG_A = MIX_A // SSM_GROUP
SSM_STATE = 64
MIX_B = D_MODEL - MIX_A
HD_B = 64
NH_B = MIX_B // HD_B
CHUNK = 128
MIX_WIDTH = MIX_A + MIX_B
IN_WIDTH = MIX_A + 2 * MIX_B
D_FF = ((8 * D_MODEL // 3 + 63) // 64) * 64
N_EXPERTS = 8
TOP_K = 2
D_FF_EXPERT = D_FF // 2
N_DENSE = (DEPTH + 1) // 2
N_MOE = DEPTH // 2
EPS = 1e-6
DT_MIN = 1e-3
DT_MAX = 1e-1

kernel_name = "hybrid_s5_gmlp_decoder_step"


def rmsnorm(x, g):
    xf = x.astype(jnp.float32)
    y = xf * lax.rsqrt(jnp.mean(xf * xf, axis=-1, keepdims=True) + EPS)
    return (y * g.astype(jnp.float32)).astype(x.dtype)


def layernorm(x, g, b):
    xf = x.astype(jnp.float32)
    mu = jnp.mean(xf, axis=-1, keepdims=True)
    xc = xf - mu
    var = jnp.mean(xc * xc, axis=-1, keepdims=True)
    y = xc * lax.rsqrt(var + EPS) * g.astype(jnp.float32) + b.astype(jnp.float32)
    return y.astype(x.dtype)


def _cmul(ar, ai, br, bi):
    return ar * br - ai * bi, ar * bi + ai * br


def _scan_combine(e1, e2):
    a1r, a1i, b1r, b1i = e1
    a2r, a2i, b2r, b2i = e2
    ar, ai = _cmul(a2r, a2i, a1r, a1i)
    br, bi = _cmul(a2r, a2i, b1r, b1i)
    return ar, ai, br + b2r, bi + b2i


def s5_mixer(u, h0_re, h0_im, a_re, a_im, log_dt, b_re, b_im, c_re, c_im, d_skip, w_glu, b_glu):
    bsz, L = u.shape[0], u.shape[1]
    uf = u.astype(jnp.float32)
    dt = jnp.exp(log_dt.astype(jnp.float32))[:, None]
    lam_re = a_re.astype(jnp.float32)
    lam_im = a_im.astype(jnp.float32)
    mag = jnp.exp(lam_re * dt)
    ab_re = mag * jnp.cos(lam_im * dt)
    ab_im = mag * jnp.sin(lam_im * dt)
    den = lam_re * lam_re + lam_im * lam_im
    nr = ab_re - 1.0
    q_re = (nr * lam_re + ab_im * lam_im) / den
    q_im = (ab_im * lam_re - nr * lam_im) / den
    bb_re, bb_im = _cmul(q_re[..., None], q_im[..., None],
                         b_re.astype(jnp.float32), b_im.astype(jnp.float32))
    bu_re = jnp.einsum('blgh,gph->blgp', uf, bb_re)
    bu_im = jnp.einsum('blgh,gph->blgp', uf, bb_im)
    i_re, i_im = _cmul(ab_re, ab_im, h0_re.astype(jnp.float32), h0_im.astype(jnp.float32))
    bu_re = bu_re.at[:, 0].add(i_re)
    bu_im = bu_im.at[:, 0].add(i_im)
    lam_b_re = jnp.broadcast_to(ab_re, bu_re.shape)
    lam_b_im = jnp.broadcast_to(ab_im, bu_im.shape)
    _, _, h_re, h_im = lax.associative_scan(_scan_combine, (lam_b_re, lam_b_im, bu_re, bu_im), axis=1)
    y = (jnp.einsum('blgp,ghp->blgh', h_re, c_re.astype(jnp.float32))
         - jnp.einsum('blgp,ghp->blgh', h_im, c_im.astype(jnp.float32))
         + d_skip.astype(jnp.float32) * uf)
    y = jax.nn.gelu(y)
    z = jnp.einsum('blgh,ghk->blgk', y, w_glu.astype(jnp.float32)) + b_glu.astype(jnp.float32)
    out = z[..., :SSM_GROUP] * jax.nn.sigmoid(z[..., SSM_GROUP:])
    return out.reshape(bsz, L, MIX_A).astype(u.dtype), h_re[:, -1], h_im[:, -1]


def sgu_mixer(u, v, w_s, b_s, ln_g, ln_b):
    bsz, L = u.shape[0], u.shape[1]
    u = jax.nn.gelu(u)
    v = layernorm(jax.nn.gelu(v), ln_g, ln_b)
    c = min(L, CHUNK)
    n = L // c
    mask = jnp.tril(jnp.ones((c, c), dtype=bool))
    ws = jnp.where(mask[None], w_s[:, :c, :c], 0.0)
    vc = v.reshape(bsz, n, c, NH_B, HD_B)
    s = jnp.einsum('hqk,bnkhd->bnqhd', ws, vc) + b_s[:, :c].T[None, None, :, :, None]
    out = u * s.reshape(bsz, L, NH_B, HD_B)
    return out.reshape(bsz, L, MIX_B), v


def swiglu(x, wg, wu, wd):
    return (jax.nn.silu(x @ wg) * (x @ wu)) @ wd


def moe_swiglu(x, w_router, b_router, we_gate, we_up, we_down):
    logits = (x @ w_router).astype(jnp.float32) + b_router.astype(jnp.float32)
    top_v, top_i = lax.top_k(logits, TOP_K)
    top_w = jax.nn.softmax(top_v, axis=-1)
    gates = jnp.sum(jax.nn.one_hot(top_i, N_EXPERTS, dtype=jnp.float32) * top_w[..., None], axis=-2)
    out = jnp.zeros(x.shape, jnp.float32)
    for e in range(N_EXPERTS):
        out = out + gates[..., e:e + 1] * swiglu(x, we_gate[e], we_up[e], we_down[e]).astype(jnp.float32)
    return out.astype(x.dtype)


def _trunk(x, h0_re, h0_im, norm_mix, w_in, ssm_a_re, ssm_a_im, ssm_log_dt, ssm_b_re, ssm_b_im,
           ssm_c_re, ssm_c_im, ssm_d, glu_w, glu_b, sgu_w, sgu_b, sgu_ln_g, sgu_ln_b,
           out_norm_a, out_norm_b, w_out, norm_ffn, ffn_w_gate, ffn_w_up, ffn_w_down,
           router_w, router_b, moe_w_gate, moe_w_up, moe_w_down, norm_final):
    bsz, L = x.shape[0], x.shape[1]
    new_re, new_im, new_v = [], [], []
    for layer in range(DEPTH):
        hn = rmsnorm(x, norm_mix[layer])
        z = hn @ w_in[layer]
        ua = z[..., :MIX_A].reshape(bsz, L, NG_A, SSM_GROUP)
        ub = z[..., MIX_A:MIX_A + MIX_B].reshape(bsz, L, NH_B, HD_B)
        vb = z[..., MIX_A + MIX_B:].reshape(bsz, L, NH_B, HD_B)
        ya, hr, hi = s5_mixer(ua, h0_re[layer], h0_im[layer], ssm_a_re[layer], ssm_a_im[layer],
                              ssm_log_dt[layer], ssm_b_re[layer], ssm_b_im[layer],
                              ssm_c_re[layer], ssm_c_im[layer], ssm_d[layer],
                              glu_w[layer], glu_b[layer])
        yb, vn = sgu_mixer(ub, vb, sgu_w[layer], sgu_b[layer], sgu_ln_g[layer], sgu_ln_b[layer])
        ycat = jnp.concatenate([rmsnorm(ya, out_norm_a[layer]), rmsnorm(yb, out_norm_b[layer])], axis=-1)
        x = x + ycat @ w_out[layer]
        hn2 = rmsnorm(x, norm_ffn[layer])
        if layer % 2 == 0:
            j = layer // 2
            f = swiglu(hn2, ffn_w_gate[j], ffn_w_up[j], ffn_w_down[j])
        else:
            j = layer // 2
            f = moe_swiglu(hn2, router_w[j], router_b[j], moe_w_gate[j], moe_w_up[j], moe_w_down[j])
        x = x + f
        new_re.append(hr)
        new_im.append(hi)
        new_v.append(vn)
    y = rmsnorm(x, norm_final)
    return y, jnp.stack(new_re), jnp.stack(new_im), jnp.stack(new_v)


def setup_inputs(seed: int = 0) -> dict:
    key = jax.random.key(seed)
    ks = jax.random.split(key, 40)
    f32 = jnp.float32
    nrm = lambda k, shape, s: (jax.random.normal(k, shape, f32) * s)
    n_idx = jnp.arange(SSM_STATE, dtype=f32)
    inp = {}
    inp['x_prompt'] = nrm(ks[0], (BATCH, SEQ, D_MODEL), 1.0)
    inp['x_sample'] = nrm(ks[1], (DEC_BATCH, DEC_SEQ, D_MODEL), 1.0)
    inp['state_ssm_re'] = nrm(ks[2], (DEPTH, DEC_BATCH, NG_A, SSM_STATE), 0.5)
    inp['state_ssm_im'] = nrm(ks[3], (DEPTH, DEC_BATCH, NG_A, SSM_STATE), 0.5)
    inp['norm_mix'] = 1.0 + nrm(ks[4], (DEPTH, D_MODEL), 0.02)
    inp['w_in'] = nrm(ks[5], (DEPTH, D_MODEL, IN_WIDTH), D_MODEL ** -0.5)
    inp['ssm_a_re'] = -0.5 + nrm(ks[6], (DEPTH, NG_A, SSM_STATE), 0.01)
    inp['ssm_a_im'] = math.pi * n_idx + nrm(ks[7], (DEPTH, NG_A, SSM_STATE), 0.01)
    inp['ssm_log_dt'] = jax.random.uniform(ks[8], (DEPTH, NG_A), f32, math.log(DT_MIN), math.log(DT_MAX))
    inp['ssm_b_re'] = nrm(ks[9], (DEPTH, NG_A, SSM_STATE, SSM_GROUP), (2 * SSM_GROUP) ** -0.5)
    inp['ssm_b_im'] = nrm(ks[10], (DEPTH, NG_A, SSM_STATE, SSM_GROUP), (2 * SSM_GROUP) ** -0.5)
    inp['ssm_c_re'] = nrm(ks[11], (DEPTH, NG_A, SSM_GROUP, SSM_STATE), (2 * SSM_STATE) ** -0.5)
    inp['ssm_c_im'] = nrm(ks[12], (DEPTH, NG_A, SSM_GROUP, SSM_STATE), (2 * SSM_STATE) ** -0.5)
    inp['ssm_d'] = nrm(ks[13], (DEPTH, NG_A, SSM_GROUP), 1.0)
    inp['glu_w'] = nrm(ks[14], (DEPTH, NG_A, SSM_GROUP, 2 * SSM_GROUP), SSM_GROUP ** -0.5)
    inp['glu_b'] = nrm(ks[15], (DEPTH, NG_A, 2 * SSM_GROUP), 0.02)
    inp['sgu_w'] = nrm(ks[16], (DEPTH, NH_B, CHUNK, CHUNK), CHUNK ** -0.5)
    inp['sgu_b'] = 1.0 + nrm(ks[17], (DEPTH, NH_B, CHUNK), 0.02)
    inp['sgu_ln_g'] = 1.0 + nrm(ks[18], (DEPTH, NH_B, HD_B), 0.02)
    inp['sgu_ln_b'] = nrm(ks[19], (DEPTH, NH_B, HD_B), 0.02)
    inp['out_norm_a'] = 1.0 + nrm(ks[20], (DEPTH, MIX_A), 0.02)
    inp['out_norm_b'] = 1.0 + nrm(ks[21], (DEPTH, MIX_B), 0.02)
    inp['w_out'] = nrm(ks[22], (DEPTH, MIX_WIDTH, D_MODEL), MIX_WIDTH ** -0.5)
    inp['norm_ffn'] = 1.0 + nrm(ks[23], (DEPTH, D_MODEL), 0.02)
    inp['ffn_w_gate'] = nrm(ks[24], (N_DENSE, D_MODEL, D_FF), D_MODEL ** -0.5)
    inp['ffn_w_up'] = nrm(ks[25], (N_DENSE, D_MODEL, D_FF), D_MODEL ** -0.5)
    inp['ffn_w_down'] = nrm(ks[26], (N_DENSE, D_FF, D_MODEL), D_FF ** -0.5)
    inp['router_w'] = nrm(ks[27], (N_MOE, D_MODEL, N_EXPERTS), D_MODEL ** -0.5)
    inp['router_b'] = nrm(ks[28], (N_MOE, N_EXPERTS), 0.01)
    inp['moe_w_gate'] = nrm(ks[29], (N_MOE, N_EXPERTS, D_MODEL, D_FF_EXPERT), D_MODEL ** -0.5)
    inp['moe_w_up'] = nrm(ks[30], (N_MOE, N_EXPERTS, D_MODEL, D_FF_EXPERT), D_MODEL ** -0.5)
    inp['moe_w_down'] = nrm(ks[31], (N_MOE, N_EXPERTS, D_FF_EXPERT, D_MODEL), D_FF_EXPERT ** -0.5)
    inp['norm_final'] = 1.0 + nrm(ks[32], (D_MODEL,), 0.02)
    return inp


def reference(x_prompt, x_sample, state_ssm_re, state_ssm_im, norm_mix, w_in, ssm_a_re, ssm_a_im,
              ssm_log_dt, ssm_b_re, ssm_b_im, ssm_c_re, ssm_c_im, ssm_d, glu_w, glu_b, sgu_w, sgu_b,
              sgu_ln_g, sgu_ln_b, out_norm_a, out_norm_b, w_out, norm_ffn, ffn_w_gate, ffn_w_up,
              ffn_w_down, router_w, router_b, moe_w_gate, moe_w_up, moe_w_down, norm_final):
    weights = (norm_mix, w_in, ssm_a_re, ssm_a_im, ssm_log_dt, ssm_b_re, ssm_b_im, ssm_c_re, ssm_c_im,
               ssm_d, glu_w, glu_b, sgu_w, sgu_b, sgu_ln_g, sgu_ln_b, out_norm_a, out_norm_b, w_out,
               norm_ffn, ffn_w_gate, ffn_w_up, ffn_w_down, router_w, router_b, moe_w_gate, moe_w_up,
               moe_w_down, norm_final)
    h0 = jnp.zeros((DEPTH, x_prompt.shape[0], NG_A, SSM_STATE), jnp.float32)
    y_prompt, ssm_re_prompt, ssm_im_prompt, _ = _trunk(x_prompt, h0, h0, *weights)
    y_sample, ssm_re_sample, ssm_im_sample, sgu_v_sample = _trunk(x_sample, state_ssm_re, state_ssm_im, *weights)
    return (y_prompt, y_sample, ssm_re_prompt, ssm_im_prompt, ssm_re_sample, ssm_im_sample, sgu_v_sample)
```

```python
import functools
import math

import jax
import jax.numpy as jnp
from jax import lax
from jax.experimental import pallas as pl
from jax.experimental.pallas import tpu as pltpu

F32 = jnp.float32
BF16 = jnp.bfloat16

D_MODEL = 1024
DEPTH = 4
MIX_A = 512
SSM_GROUP = 16
NG_A = 32
SSM_STATE = 64
N_STATE = NG_A * SSM_STATE
MIX_B = 512
HD_B = 64
NH_B = 8
CHUNK = 128
IN_WIDTH = MIX_A + 2 * MIX_B
D_FF = 2752
N_EXPERTS = 8
D_FF_EXPERT = D_FF // 2
FF_PAD = 1408
EPS = 1e-6

LANES = 128
SUBLANES = 8
STATE_COLS = 512
N_SCOL = N_STATE // STATE_COLS
VMEM_LIMIT = 52 * 1024 * 1024


def _rms(x, g):
    return x * lax.rsqrt(jnp.mean(x * x, axis=-1, keepdims=True) + EPS) * g


def _gelu(x):
    c = math.sqrt(2.0 / math.pi)
    return 0.5 * x * (1.0 + jnp.tanh(c * (x + 0.044715 * (x * x * x))))


def _sigmoid(x):
    return 1.0 / (1.0 + jnp.exp(-x))


def _split_bf16(x):
    hi = x.astype(BF16)
    lo = (x - hi.astype(F32)).astype(BF16)
    return hi, lo


def _dot(a, b):
    return jnp.dot(a, b, preferred_element_type=F32)


def _s5_prep_kernel(are_ref, aim_ref, ldt_ref, bre_ref, bim_ref,
                    abre_ref, abim_ref, bbre_ref, bbim_ref):
    lam_re = are_ref[...]
    lam_im = aim_ref[...]
    dt = jnp.exp(ldt_ref[...])
    mag = jnp.exp(lam_re * dt)
    ab_re = mag * jnp.cos(lam_im * dt)
    ab_im = mag * jnp.sin(lam_im * dt)
    den = lam_re * lam_re + lam_im * lam_im
    nr = ab_re - 1.0
    q_re = (nr * lam_re + ab_im * lam_im) / den
    q_im = (ab_im * lam_re - nr * lam_im) / den
    b_re = bre_ref[...]
    b_im = bim_ref[...]
    abre_ref[...] = ab_re
    abim_ref[...] = ab_im
    bbre_ref[...] = q_re * b_re - q_im * b_im
    bbim_ref[...] = q_re * b_im + q_im * b_re


def _s5_prep(ssm_a_re, ssm_a_im, ssm_log_dt, ssm_b_re, ssm_b_im):
    shp = (DEPTH, NG_A, SSM_GROUP, SSM_STATE)
    rows = DEPTH * NG_A * SSM_GROUP
    bc = lambda a: jnp.broadcast_to(a, shp).reshape(rows, SSM_STATE)
    are = bc(ssm_a_re[:, :, None, :])
    aim = bc(ssm_a_im[:, :, None, :])
    ldt = bc(ssm_log_dt[:, :, None, None])
    bre = jnp.swapaxes(ssm_b_re, 2, 3).reshape(rows, SSM_STATE)
    bim = jnp.swapaxes(ssm_b_im, 2, 3).reshape(rows, SSM_STATE)
    sds = jax.ShapeDtypeStruct((rows, SSM_STATE), F32)
    abre, abim, bbre, bbim = pl.pallas_call(
        _s5_prep_kernel, out_shape=(sds, sds, sds, sds), name="s5_prep",
    )(are, aim, ldt, bre, bim)
    lam_re = abre.reshape(shp)[:, :, 0, :].reshape(DEPTH, 1, N_STATE)
    lam_im = abim.reshape(shp)[:, :, 0, :].reshape(DEPTH, 1, N_STATE)
    lam = jnp.concatenate([lam_re, lam_im], axis=1)
    eye8 = jnp.eye(8, dtype=F32)

    def bblock(bb):
        bb = bb.reshape(DEPTH, 4, 8, SSM_GROUP, 1, SSM_STATE)
        blk = bb * eye8[None, None, :, None, :, None]
        return blk.reshape(DEPTH, 4, 8 * SSM_GROUP, 8 * SSM_STATE)

    bblk = jnp.concatenate([bblock(bbre), bblock(bbim)], axis=-1).astype(BF16)
    return lam, bblk


def _c_blocks(c):
    eye8 = jnp.eye(8, dtype=F32)
    ct = jnp.swapaxes(c, 2, 3).reshape(DEPTH, 4, 8, SSM_STATE, 1, SSM_GROUP)
    blk = ct * eye8[None, None, :, None, :, None]
    return blk.reshape(DEPTH, 4, 8 * SSM_STATE, 8 * SSM_GROUP)


def _glu_blocks(w):
    k = w.shape[-1]
    eye = jnp.eye(NG_A, dtype=F32)
    blk = w.reshape(DEPTH, NG_A, SSM_GROUP, 1, k) * eye[None, :, None, :, None]
    return blk.reshape(DEPTH, NG_A * SSM_GROUP, NG_A * k)


def _inproj_kernel(x_ref, g_ref, w_ref, z_ref):
    hn = _rms(x_ref[...], g_ref[...])
    z_ref[...] = _dot(hn.astype(BF16), w_ref[...])


def _inproj(x, g, w, tm):
    m = x.shape[0]
    return pl.pallas_call(
        _inproj_kernel,
        grid=(m // tm,),
        in_specs=[
            pl.BlockSpec((tm, D_MODEL), lambda i: (i, 0)),
            pl.BlockSpec((1, D_MODEL), lambda i: (0, 0)),
            pl.BlockSpec((D_MODEL, IN_WIDTH), lambda i: (0, 0)),
        ],
        out_specs=pl.BlockSpec((tm, IN_WIDTH), lambda i: (i, 0)),
        out_shape=jax.ShapeDtypeStruct((m, IN_WIDTH), F32),
        compiler_params=pltpu.CompilerParams(
            dimension_semantics=("parallel",), vmem_limit_bytes=VMEM_LIMIT),
        name="inproj",
    )(x, g, w)


def _s5_kernel(*refs, nb, t_blk, prompt):
    if prompt:
        (u_ref, bblk_ref, lam_ref, cre_ref, cim_ref, d_ref, wa_ref, wg_ref, ba_ref, bg_ref,
         ya_ref, hre_ref, him_ref, utm, bre, bim, otm) = refs
    else:
        (u_ref, h0re_ref, h0im_ref, bblk_ref, lam_ref, cre_ref, cim_ref, d_ref, wa_ref, wg_ref,
         ba_ref, bg_ref, ya_ref, hre_ref, him_ref, utm, bre, bim, otm, tmp) = refs

    @pl.when(pl.program_id(0) == 0)
    def _():
        if prompt:
            hre_ref[...] = jnp.zeros_like(hre_ref)
            him_ref[...] = jnp.zeros_like(him_ref)
        else:
            hre_ref[...] = h0re_ref[...]
            him_ref[...] = h0im_ref[...]

    for c in range(4):
        ls = slice(c * LANES, (c + 1) * LANES)
        if prompt:
            for b in range(nb):
                utm[c, pl.ds(b, t_blk, stride=nb), :] = u_ref[b, :, ls]
        else:
            tmp[c] = u_ref[:, ls]
            for t in range(t_blk):
                utm[c, t * nb:(t + 1) * nb, :] = tmp[c, pl.ds(t, nb, stride=t_blk), :]

    for c in range(4):
        r = _dot(utm[c].astype(BF16), bblk_ref[c])
        bre[:, c * STATE_COLS:(c + 1) * STATE_COLS] = r[:, :STATE_COLS]
        bim[:, c * STATE_COLS:(c + 1) * STATE_COLS] = r[:, STATE_COLS:]

    for c in range(N_SCOL):
        cs = slice(c * STATE_COLS, (c + 1) * STATE_COLS)
        lr = jnp.broadcast_to(lam_ref[0:1, cs], (SUBLANES, STATE_COLS))
        li = jnp.broadcast_to(lam_ref[1:2, cs], (SUBLANES, STATE_COLS))

        def group_body(bg, carry, cs=cs, lr=lr, li=li):
            r0 = pl.multiple_of(bg * SUBLANES, SUBLANES)

            def t_body(t, h):
                hr, hi = h
                row = pl.multiple_of(t * nb + r0, SUBLANES)
                nr = lr * hr - li * hi + bre[pl.ds(row, SUBLANES), cs]
                ni = lr * hi + li * hr + bim[pl.ds(row, SUBLANES), cs]
                bre[pl.ds(row, SUBLANES), cs] = nr
                bim[pl.ds(row, SUBLANES), cs] = ni
                return nr, ni

            h0 = (hre_ref[pl.ds(r0, SUBLANES), cs], him_ref[pl.ds(r0, SUBLANES), cs])
            hr, hi = lax.fori_loop(0, t_blk, t_body, h0, unroll=min(t_blk, 8))
            hre_ref[pl.ds(r0, SUBLANES), cs] = hr
            him_ref[pl.ds(r0, SUBLANES), cs] = hi
            return carry

        lax.fori_loop(0, nb // SUBLANES, group_body, 0)

    ys = []
    for c in range(4):
        cs = slice(c * STATE_COLS, (c + 1) * STATE_COLS)
        ls = slice(c * LANES, (c + 1) * LANES)
        y = _dot(bre[:, cs].astype(BF16), cre_ref[c]) + _dot(bim[:, cs].astype(BF16), cim_ref[c])
        ys.append(_gelu(y + d_ref[:, ls] * utm[c]).astype(BF16))
    yb = jnp.concatenate(ys, axis=1)
    za = _dot(yb, wa_ref[...]) + ba_ref[...]
    zg = _dot(yb, wg_ref[...]) + bg_ref[...]
    o = za * _sigmoid(zg)

    for c in range(4):
        ls = slice(c * LANES, (c + 1) * LANES)
        if prompt:
            otm[c] = o[:, ls]
            for b in range(nb):
                ya_ref[b, :, ls] = otm[c, pl.ds(b, t_blk, stride=nb), :]
        else:
            for t in range(t_blk):
                otm[c, pl.ds(t, nb, stride=t_blk), :] = o[t * nb:(t + 1) * nb, ls]
            ya_ref[:, ls] = otm[c]


def _s5_weight_specs(fix):
    return [
        pl.BlockSpec((4, LANES, 2 * STATE_COLS), fix(3)),
        pl.BlockSpec((2, N_STATE), fix(2)),
        pl.BlockSpec((4, STATE_COLS, LANES), fix(3)),
        pl.BlockSpec((4, STATE_COLS, LANES), fix(3)),
        pl.BlockSpec((1, MIX_A), fix(2)),
        pl.BlockSpec((MIX_A, MIX_A), fix(2)),
        pl.BlockSpec((MIX_A, MIX_A), fix(2)),
        pl.BlockSpec((1, MIX_A), fix(2)),
        pl.BlockSpec((1, MIX_A), fix(2)),
    ]


def _s5_prompt(z3, weights, t_blk):
    nb, seq = z3.shape[0], z3.shape[1]
    rows = nb * t_blk
    fix = lambda n: (lambda j: (0,) * n)
    return pl.pallas_call(
        functools.partial(_s5_kernel, nb=nb, t_blk=t_blk, prompt=True),
        grid=(seq // t_blk,),
        in_specs=[pl.BlockSpec((nb, t_blk, MIX_A), lambda j: (0, j, 0))] + _s5_weight_specs(fix),
        out_specs=[
            pl.BlockSpec((nb, t_blk, MIX_A), lambda j: (0, j, 0)),
            pl.BlockSpec((nb, N_STATE), lambda j: (0, 0)),
            pl.BlockSpec((nb, N_STATE), lambda j: (0, 0)),
        ],
        out_shape=[
            jax.ShapeDtypeStruct((nb, seq, MIX_A), F32),
            jax.ShapeDtypeStruct((nb, N_STATE), F32),
            jax.ShapeDtypeStruct((nb, N_STATE), F32),
        ],
        scratch_shapes=[
            pltpu.VMEM((4, rows, LANES), F32),
            pltpu.VMEM((rows, N_STATE), F32),
            pltpu.VMEM((rows, N_STATE), F32),
            pltpu.VMEM((4, rows, LANES), F32),
        ],
        compiler_params=pltpu.CompilerParams(
            dimension_semantics=("arbitrary",), vmem_limit_bytes=VMEM_LIMIT),
        name="s5_prompt",
    )(z3, *weights)


def _s5_sample(z, h0re, h0im, weights, nb, t_blk):
    rows = nb * t_blk
    fix = lambda n: (lambda j: (0,) * n)
    return pl.pallas_call(
        functools.partial(_s5_kernel, nb=nb, t_blk=t_blk, prompt=False),
        grid=(1,),
        in_specs=[
            pl.BlockSpec((rows, MIX_A), lambda j: (0, 0)),
            pl.BlockSpec((nb, N_STATE), lambda j: (0, 0)),
            pl.BlockSpec((nb, N_STATE), lambda j: (0, 0)),
        ] + _s5_weight_specs(fix),
        out_specs=[
            pl.BlockSpec((rows, MIX_A), lambda j: (0, 0)),
            pl.BlockSpec((nb, N_STATE), lambda j: (0, 0)),
            pl.BlockSpec((nb, N_STATE), lambda j: (0, 0)),
        ],
        out_shape=[
            jax.ShapeDtypeStruct((rows, MIX_A), F32),
            jax.ShapeDtypeStruct((nb, N_STATE), F32),
            jax.ShapeDtypeStruct((nb, N_STATE), F32),
        ],
        scratch_shapes=[
            pltpu.VMEM((4, rows, LANES), F32),
            pltpu.VMEM((rows, N_STATE), F32),
            pltpu.VMEM((rows, N_STATE), F32),
            pltpu.VMEM((4, rows, LANES), F32),
            pltpu.VMEM((4, rows, LANES), F32),
        ],
        compiler_params=pltpu.CompilerParams(
            dimension_semantics=("arbitrary",), vmem_limit_bytes=VMEM_LIMIT),
        name="s5_sample",
    )(z, h0re, h0im, *weights)


def _group_layernorm(gv, gm_ref, lng_ref, lnb_ref):
    def gmean(a):
        hi, lo = _split_bf16(a)
        return _dot(hi, gm_ref[...]) + _dot(lo, gm_ref[...])

    xc = gv - gmean(gv)
    var = gmean(xc * xc)
    return xc * lax.rsqrt(var + EPS) * lng_ref[...] + lnb_ref[...]


def _sgu_prompt_kernel(u_ref, v_ref, ws_ref, bias_ref, lng_ref, lnb_ref, gm_ref, o_ref, *, n_chunks):
    row = lax.broadcasted_iota(jnp.int32, (CHUNK, 2 * CHUNK), 0)
    col = lax.broadcasted_iota(jnp.int32, (CHUNK, 2 * CHUNK), 1)
    causal = (col % CHUNK) <= row
    lane = lax.broadcasted_iota(jnp.int32, (CHUNK, LANES), 1)
    first_head = lane < HD_B
    wcat = []
    for p in range(NH_B // 2):
        w = jnp.concatenate([ws_ref[2 * p], ws_ref[2 * p + 1]], axis=1)
        wcat.append(jnp.where(causal, w, 0.0).astype(BF16))
    for n in range(n_chunks):
        rs = slice(n * CHUNK, (n + 1) * CHUNK)
        u = _gelu(u_ref[rs, :])
        vn = _group_layernorm(_gelu(v_ref[rs, :]), gm_ref, lng_ref, lnb_ref)
        for p in range(NH_B // 2):
            ls = slice(p * LANES, (p + 1) * LANES)
            vp = vn[:, ls]
            rhs = jnp.concatenate(
                [jnp.where(first_head, vp, 0.0), jnp.where(first_head, 0.0, vp)], axis=0).astype(BF16)
            s = _dot(wcat[p], rhs)
            o_ref[rs, ls] = u[:, ls] * (s + bias_ref[:, ls])


def _sgu_prompt(z, ws, bias, lng, lnb, gm, n_chunks):
    m = z.shape[0]
    tm = n_chunks * CHUNK
    fix2 = lambda i: (0, 0)
    return pl.pallas_call(
        functools.partial(_sgu_prompt_kernel, n_chunks=n_chunks),
        grid=(m // tm,),
        in_specs=[
            pl.BlockSpec((tm, MIX_B), lambda i: (i, 1)),
            pl.BlockSpec((tm, MIX_B), lambda i: (i, 2)),
            pl.BlockSpec((NH_B, CHUNK, CHUNK), lambda i: (0, 0, 0)),
            pl.BlockSpec((CHUNK, MIX_B), fix2),
            pl.BlockSpec((1, MIX_B), fix2),
            pl.BlockSpec((1, MIX_B), fix2),
            pl.BlockSpec((MIX_B, MIX_B), fix2),
        ],
        out_specs=pl.BlockSpec((tm, MIX_B), lambda i: (i, 0)),
        out_shape=jax.ShapeDtypeStruct((m, MIX_B), F32),
        compiler_params=pltpu.CompilerParams(
            dimension_semantics=("parallel",), vmem_limit_bytes=VMEM_LIMIT),
        name="sgu_prompt",
    )(z, z, ws, bias, lng, lnb, gm)


def _sgu_sample_kernel(u_ref, v_ref, coef_ref, bias_ref, lng_ref, lnb_ref, gm_ref, o_ref, vn_ref,
                       usc, vsc, osc, *, nb, t_blk):
    vn = _group_layernorm(_gelu(v_ref[...]), gm_ref, lng_ref, lnb_ref)
    vn_ref[...] = vn
    u = _gelu(u_ref[...])
    for c in range(4):
        ls = slice(c * LANES, (c + 1) * LANES)
        vsc[c] = vn[:, ls]
        usc[c] = u[:, ls]
        for q in range(t_blk):
            s = bias_ref[q:q + 1, ls]
            for k in range(q + 1):
                s = s + coef_ref[q * t_blk + k:q * t_blk + k + 1, ls] * vsc[c, pl.ds(k, nb, stride=t_blk), :]
            osc[c, pl.ds(q, nb, stride=t_blk), :] = usc[c, pl.ds(q, nb, stride=t_blk), :] * s
        o_ref[:, ls] = osc[c]


def _sgu_sample(z, coef, bias, lng, lnb, gm, nb, t_blk):
    m = nb * t_blk
    fix2 = lambda i: (0, 0)
    return pl.pallas_call(
        functools.partial(_sgu_sample_kernel, nb=nb, t_blk=t_blk),
        grid=(1,),
        in_specs=[
            pl.BlockSpec((m, MIX_B), lambda i: (0, 1)),
            pl.BlockSpec((m, MIX_B), lambda i: (0, 2)),
            pl.BlockSpec((t_blk * t_blk, MIX_B), fix2),
            pl.BlockSpec((t_blk, MIX_B), fix2),
            pl.BlockSpec((1, MIX_B), fix2),
            pl.BlockSpec((1, MIX_B), fix2),
            pl.BlockSpec((MIX_B, MIX_B), fix2),
        ],
        out_specs=[pl.BlockSpec((m, MIX_B), fix2), pl.BlockSpec((m, MIX_B), fix2)],
        out_shape=[jax.ShapeDtypeStruct((m, MIX_B), F32), jax.ShapeDtypeStruct((m, MIX_B), F32)],
        scratch_shapes=[pltpu.VMEM((4, m, LANES), F32)] * 3,
        compiler_params=pltpu.CompilerParams(
            dimension_semantics=("arbitrary",), vmem_limit_bytes=VMEM_LIMIT),
        name="sgu_sample",
    )(z, z, coef, bias, lng, lnb, gm)


def _top2_gates(logits):
    lane = lax.broadcasted_iota(jnp.int32, logits.shape, 1).astype(F32)
    neg = jnp.float32(-jnp.inf)
    lg = jnp.where(lane < N_EXPERTS, logits, neg)
    m1 = jnp.max(lg, axis=1, keepdims=True)
    i1 = jnp.min(jnp.where(lg == m1, lane, float(LANES)), axis=1, keepdims=True)
    lg2 = jnp.where(lane == i1, neg, lg)
    m2 = jnp.max(lg2, axis=1, keepdims=True)
    i2 = jnp.min(jnp.where(lg2 == m2, lane, float(LANES)), axis=1, keepdims=True)
    ex = jnp.exp(m2 - m1)
    w1 = 1.0 / (1.0 + ex)
    w2 = ex / (1.0 + ex)
    return jnp.where(lane == i1, w1, 0.0) + jnp.where(lane == i2, w2, 0.0)


def _ffn_kernel(x_ref, ya_ref, yb_ref, ga_ref, gb_ref, wout_ref, gf_ref, rwh_ref, rwl_ref, rb_ref,
                wgu_ref, wd_ref, gfin_ref, o_ref, acc_ref, hn_ref, gates_ref, *, moe, final, n_e):
    e = pl.program_id(1)

    @pl.when(e == 0)
    def _():
        na = _rms(ya_ref[...], ga_ref[...]).astype(BF16)
        nb = _rms(yb_ref[...], gb_ref[...]).astype(BF16)
        x1 = x_ref[...] + _dot(na, wout_ref[0:MIX_A, :]) + _dot(nb, wout_ref[MIX_A:, :])
        hn = _rms(x1, gf_ref[...])
        hn_ref[...] = hn.astype(BF16)
        acc_ref[...] = x1
        if moe:
            hi, lo = _split_bf16(hn)
            logits = _dot(hi, rwh_ref[...]) + _dot(lo, rwh_ref[...]) + _dot(hi, rwl_ref[...])
            gates_ref[...] = _top2_gates(logits + rb_ref[...])

    h = _dot(hn_ref[...], wgu_ref[...])
    hg = h[:, :FF_PAD]
    a = (hg * _sigmoid(hg) * h[:, FF_PAD:]).astype(BF16)
    y = _dot(a, wd_ref[...])
    if moe:
        lane = lax.broadcasted_iota(jnp.int32, gates_ref.shape, 1)
        g = jnp.sum(jnp.where(lane == e, gates_ref[...], 0.0), axis=1, keepdims=True)
        acc_ref[...] += g * y
    else:
        acc_ref[...] += y

    @pl.when(e == n_e - 1)
    def _():
        if final:
            o_ref[...] = _rms(acc_ref[...], gfin_ref[...])
        else:
            o_ref[...] = acc_ref[...]


def _ffn(x, ya, yb, ga, gb, wout, gf, rwh, rwl, rb, wgu, wd, gfin, *, tm, moe, final):
    m = x.shape[0]
    n_e = wgu.shape[0]
    fix2 = lambda i, e: (0, 0)
    return pl.pallas_call(
        functools.partial(_ffn_kernel, moe=moe, final=final, n_e=n_e),
        grid=(m // tm, n_e),
        in_specs=[
            pl.BlockSpec((tm, D_MODEL), lambda i, e: (i, 0)),
            pl.BlockSpec((tm, MIX_A), lambda i, e: (i, 0)),
            pl.BlockSpec((tm, MIX_B), lambda i, e: (i, 0)),
            pl.BlockSpec((1, MIX_A), fix2),
            pl.BlockSpec((1, MIX_B), fix2),
            pl.BlockSpec((D_MODEL, D_MODEL), fix2),
            pl.BlockSpec((1, D_MODEL), fix2),
            pl.BlockSpec((D_MODEL, LANES), fix2),
            pl.BlockSpec((D_MODEL, LANES), fix2),
            pl.BlockSpec((1, LANES), fix2),
            pl.BlockSpec((None, D_MODEL, 2 * FF_PAD), lambda i, e: (e, 0, 0)),
            pl.BlockSpec((None, FF_PAD, D_MODEL), lambda i, e: (e, 0, 0)),
            pl.BlockSpec((1, D_MODEL), fix2),
        ],
        out_specs=pl.BlockSpec((tm, D_MODEL), lambda i, e: (i, 0)),
        out_shape=jax.ShapeDtypeStruct((m, D_MODEL), F32),
        scratch_shapes=[
            pltpu.VMEM((tm, D_MODEL), F32),
            pltpu.VMEM((tm, D_MODEL), BF16),
            pltpu.VMEM((tm, LANES), F32),
        ],
        compiler_params=pltpu.CompilerParams(
            dimension_semantics=("parallel", "arbitrary"), vmem_limit_bytes=VMEM_LIMIT),
        name="ffn_moe" if moe else "ffn_dense",
    )(x, ya, yb, ga, gb, wout, gf, rwh, rwl, rb, wgu, wd, gfin)


def _pack_ffn(wg, wu, wd):
    pad = FF_PAD - wg.shape[-1]
    wgu = jnp.concatenate(
        [jnp.pad(wg, ((0, 0), (0, 0), (0, pad))), jnp.pad(wu, ((0, 0), (0, 0), (0, pad)))], axis=-1)
    return wgu.astype(BF16), jnp.pad(wd, ((0, 0), (0, pad), (0, 0))).astype(BF16)


def kernel(x_prompt, x_sample, state_ssm_re, state_ssm_im, norm_mix, w_in, ssm_a_re, ssm_a_im, ssm_log_dt, ssm_b_re, ssm_b_im, ssm_c_re, ssm_c_im, ssm_d, glu_w, glu_b, sgu_w, sgu_b, sgu_ln_g, sgu_ln_b, out_norm_a, out_norm_b, w_out, norm_ffn, ffn_w_gate, ffn_w_up, ffn_w_down, router_w, router_b, moe_w_gate, moe_w_up, moe_w_down, norm_final):
    batch, seq = x_prompt.shape[0], x_prompt.shape[1]
    dec_batch, dec_seq = x_sample.shape[0], x_sample.shape[1]
    mp, ms = batch * seq, dec_batch * dec_seq

    lam, bblk = _s5_prep(ssm_a_re, ssm_a_im, ssm_log_dt, ssm_b_re, ssm_b_im)
    cre = _c_blocks(ssm_c_re).astype(BF16)
    cim = (-_c_blocks(ssm_c_im)).astype(BF16)
    dskip = ssm_d.reshape(DEPTH, 1, MIX_A)
    wa = _glu_blocks(glu_w[..., :SSM_GROUP]).astype(BF16)
    wg = _glu_blocks(glu_w[..., SSM_GROUP:]).astype(BF16)
    ba = glu_b[..., :SSM_GROUP].reshape(DEPTH, 1, MIX_A)
    bg = glu_b[..., SSM_GROUP:].reshape(DEPTH, 1, MIX_A)
    w_in_b = w_in.astype(BF16)
    w_out_b = w_out.astype(BF16)
    gm = jnp.kron(jnp.eye(NH_B, dtype=F32), jnp.full((HD_B, HD_B), 1.0 / HD_B, F32)).astype(BF16)
    sgu_bias = jnp.repeat(jnp.swapaxes(sgu_b, 1, 2), HD_B, axis=2)
    lng = sgu_ln_g.reshape(DEPTH, 1, MIX_B)
    lnb = sgu_ln_b.reshape(DEPTH, 1, MIX_B)
    coef_s = jnp.repeat(
        jnp.transpose(sgu_w[:, :, :dec_seq, :dec_seq], (0, 2, 3, 1)).reshape(DEPTH, dec_seq * dec_seq, NH_B),
        HD_B, axis=2)
    dense_gu, dense_d = _pack_ffn(
        ffn_w_gate.reshape(-1, D_MODEL, 2, D_FF_EXPERT).transpose(0, 2, 1, 3).reshape(-1, D_MODEL, D_FF_EXPERT),
        ffn_w_up.reshape(-1, D_MODEL, 2, D_FF_EXPERT).transpose(0, 2, 1, 3).reshape(-1, D_MODEL, D_FF_EXPERT),
        ffn_w_down.reshape(-1, D_FF_EXPERT, D_MODEL))
    dense_gu = dense_gu.reshape(-1, 2, D_MODEL, 2 * FF_PAD)
    dense_d = dense_d.reshape(-1, 2, FF_PAD, D_MODEL)
    n_moe = moe_w_gate.shape[0]
    moe_gu, moe_d = _pack_ffn(
        moe_w_gate.reshape(-1, D_MODEL, D_FF_EXPERT), moe_w_up.reshape(-1, D_MODEL, D_FF_EXPERT),
        moe_w_down.reshape(-1, D_FF_EXPERT, D_MODEL))
    moe_gu = moe_gu.reshape(n_moe, N_EXPERTS, D_MODEL, 2 * FF_PAD)
    moe_d = moe_d.reshape(n_moe, N_EXPERTS, FF_PAD, D_MODEL)
    rw = jnp.pad(router_w, ((0, 0), (0, 0), (0, LANES - N_EXPERTS)))
    rwh = rw.astype(BF16)
    rwl = (rw - rwh.astype(F32)).astype(BF16)
    rb = jnp.pad(router_b, ((0, 0), (0, LANES - N_EXPERTS))).reshape(n_moe, 1, LANES)
    gfin = norm_final.reshape(1, D_MODEL)

    xp = x_prompt.reshape(mp, D_MODEL)
    xs = x_sample.reshape(ms, D_MODEL)
    h0re = state_ssm_re.reshape(DEPTH, dec_batch, N_STATE)
    h0im = state_ssm_im.reshape(DEPTH, dec_batch, N_STATE)

    re_p, im_p, re_s, im_s, v_s = [], [], [], [], []
    for l in range(DEPTH):
        g_mix = norm_mix[l].reshape(1, D_MODEL)
        s5w = (bblk[l], lam[l], cre[l], cim[l], dskip[l], wa[l], wg[l], ba[l], bg[l])
        zp = _inproj(xp, g_mix, w_in_b[l], tm=1024)
        zs = _inproj(xs, g_mix, w_in_b[l], tm=ms)
        ya_p, hre_p, him_p = _s5_prompt(zp.reshape(batch, seq, IN_WIDTH), s5w, t_blk=64)
        ya_s, hre_s, him_s = _s5_sample(zs, h0re[l], h0im[l], s5w, nb=dec_batch, t_blk=dec_seq)
        yb_p = _sgu_prompt(zp, sgu_w[l], sgu_bias[l], lng[l], lnb[l], gm, n_chunks=4)
        yb_s, vn_s = _sgu_sample(zs, coef_s[l], sgu_bias[l, :dec_seq], lng[l], lnb[l], gm,
                                 nb=dec_batch, t_blk=dec_seq)
        moe = (l % 2 == 1)
        j = l // 2
        if moe:
            ffw = (rwh[j], rwl[j], rb[j], moe_gu[j], moe_d[j])
        else:
            ffw = (rwh[0], rwl[0], rb[0], dense_gu[j], dense_d[j])
        common = (out_norm_a[l].reshape(1, MIX_A), out_norm_b[l].reshape(1, MIX_B), w_out_b[l],
                  norm_ffn[l].reshape(1, D_MODEL)) + ffw + (gfin,)
        final = (l == DEPTH - 1)
        xp = _ffn(xp, ya_p.reshape(mp, MIX_A), yb_p, *common, tm=512, moe=moe, final=final)
        xs = _ffn(xs, ya_s, yb_s, *common, tm=ms, moe=moe, final=final)
        re_p.append(hre_p)
        im_p.append(him_p)
        re_s.append(hre_s)
        im_s.append(him_s)
        v_s.append(vn_s)

    st = lambda hs, b: jnp.stack(hs).reshape(DEPTH, b, NG_A, SSM_STATE)
    return (xp.reshape(batch, seq, D_MODEL), xs.reshape(dec_batch, dec_seq, D_MODEL),
            st(re_p, batch), st(im_p, batch), st(re_s, dec_batch), st(im_s, dec_batch),
            jnp.stack(v_s).reshape(DEPTH, dec_batch, dec_seq, NH_B, HD_B))
```

```python
import functools
import math

import jax
import jax.numpy as jnp
from jax import lax
from jax.experimental import pallas as pl
from jax.experimental.pallas import tpu as pltpu

F32 = jnp.float32
BF16 = jnp.bfloat16

D_MODEL = 1024
DEPTH = 4
MIX_A = 512
SSM_GROUP = 16
NG_A = 32
SSM_STATE = 64
N_STATE = NG_A * SSM_STATE
MIX_B = 512
HD_B = 64
NH_B = 8
CHUNK = 128
IN_WIDTH = MIX_A + 2 * MIX_B
D_FF = 2752
N_EXPERTS = 8
D_FF_EXPERT = D_FF // 2
FF_PAD = 1408
EPS = 1e-6

LANES = 128
SUBLANES = 8
STATE_COLS = 512
N_SCOL = N_STATE // STATE_COLS
VMEM_LIMIT = 52 * 1024 * 1024
MOE_VMEM_LIMIT = 58 * 1024 * 1024


def _rms(x, g):
    return x * lax.rsqrt(jnp.mean(x * x, axis=-1, keepdims=True) + EPS) * g


def _gelu(x):
    c = math.sqrt(2.0 / math.pi)
    return 0.5 * x * (1.0 + jnp.tanh(c * (x + 0.044715 * (x * x * x))))


def _sigmoid(x):
    return 1.0 / (1.0 + jnp.exp(-x))


def _split_bf16(x):
    hi = x.astype(BF16)
    lo = (x - hi.astype(F32)).astype(BF16)
    return hi, lo


def _dot(a, b):
    return jnp.dot(a, b, preferred_element_type=F32)


def _s5_prep_kernel(are_ref, aim_ref, ldt_ref, bre_ref, bim_ref,
                    abre_ref, abim_ref, bbre_ref, bbim_ref):
    lam_re = are_ref[...]
    lam_im = aim_ref[...]
    dt = jnp.exp(ldt_ref[...])
    mag = jnp.exp(lam_re * dt)
    ab_re = mag * jnp.cos(lam_im * dt)
    ab_im = mag * jnp.sin(lam_im * dt)
    den = lam_re * lam_re + lam_im * lam_im
    nr = ab_re - 1.0
    q_re = (nr * lam_re + ab_im * lam_im) / den
    q_im = (ab_im * lam_re - nr * lam_im) / den
    b_re = bre_ref[...]
    b_im = bim_ref[...]
    abre_ref[...] = ab_re
    abim_ref[...] = ab_im
    bbre_ref[...] = q_re * b_re - q_im * b_im
    bbim_ref[...] = q_re * b_im + q_im * b_re


def _s5_prep(ssm_a_re, ssm_a_im, ssm_log_dt, ssm_b_re, ssm_b_im):
    shp = (DEPTH, NG_A, SSM_GROUP, SSM_STATE)
    rows = DEPTH * NG_A * SSM_GROUP
    bc = lambda a: jnp.broadcast_to(a, shp).reshape(rows, SSM_STATE)
    are = bc(ssm_a_re[:, :, None, :])
    aim = bc(ssm_a_im[:, :, None, :])
    ldt = bc(ssm_log_dt[:, :, None, None])
    bre = jnp.swapaxes(ssm_b_re, 2, 3).reshape(rows, SSM_STATE)
    bim = jnp.swapaxes(ssm_b_im, 2, 3).reshape(rows, SSM_STATE)
    sds = jax.ShapeDtypeStruct((rows, SSM_STATE), F32)
    abre, abim, bbre, bbim = pl.pallas_call(
        _s5_prep_kernel, out_shape=(sds, sds, sds, sds), name="s5_prep",
    )(are, aim, ldt, bre, bim)
    lam_re = abre.reshape(shp)[:, :, 0, :].reshape(DEPTH, 1, N_STATE)
    lam_im = abim.reshape(shp)[:, :, 0, :].reshape(DEPTH, 1, N_STATE)
    lam = jnp.concatenate([lam_re, lam_im], axis=1)
    eye8 = jnp.eye(8, dtype=F32)

    def bblock(bb):
        bb = bb.reshape(DEPTH, 4, 8, SSM_GROUP, 1, SSM_STATE)
        blk = bb * eye8[None, None, :, None, :, None]
        return blk.reshape(DEPTH, 4, 8 * SSM_GROUP, 8 * SSM_STATE)

    bblk = jnp.concatenate([bblock(bbre), bblock(bbim)], axis=-1).astype(BF16)
    return lam, bblk


def _c_blocks(c):
    eye8 = jnp.eye(8, dtype=F32)
    ct = jnp.swapaxes(c, 2, 3).reshape(DEPTH, 4, 8, SSM_STATE, 1, SSM_GROUP)
    blk = ct * eye8[None, None, :, None, :, None]
    return blk.reshape(DEPTH, 4, 8 * SSM_STATE, 8 * SSM_GROUP)


def _glu_blocks(w):
    k = w.shape[-1]
    eye = jnp.eye(NG_A, dtype=F32)
    blk = w.reshape(DEPTH, NG_A, SSM_GROUP, 1, k) * eye[None, :, None, :, None]
    return blk.reshape(DEPTH, NG_A * SSM_GROUP, NG_A * k)


def _inproj_kernel(*refs, n_res):
    g_ref, w_ref, z_ref = refs[n_res:]
    x = refs[0][...]
    for r in refs[1:n_res]:
        x = x + r[...]
    hn = _rms(x, g_ref[...])
    z_ref[...] = _dot(hn.astype(BF16), w_ref[...])


def _inproj(xs, g, w, tm):
    m = xs[0].shape[0]
    return pl.pallas_call(
        functools.partial(_inproj_kernel, n_res=len(xs)),
        grid=(m // tm,),
        in_specs=[pl.BlockSpec((tm, D_MODEL), lambda i: (i, 0)) for _ in xs] + [
            pl.BlockSpec((1, D_MODEL), lambda i: (0, 0)),
            pl.BlockSpec((D_MODEL, IN_WIDTH), lambda i: (0, 0)),
        ],
        out_specs=pl.BlockSpec((tm, IN_WIDTH), lambda i: (i, 0)),
        out_shape=jax.ShapeDtypeStruct((m, IN_WIDTH), F32),
        compiler_params=pltpu.CompilerParams(
            dimension_semantics=("parallel",), vmem_limit_bytes=VMEM_LIMIT),
        name="inproj",
    )(*xs, g, w)


def _s5_kernel(*refs, nb, t_blk, prompt):
    if prompt:
        (u_ref, bblk_ref, lam_ref, cre_ref, cim_ref, d_ref, wa_ref, wg_ref, ba_ref, bg_ref,
         ya_ref, hre_ref, him_ref, utm, bre, bim, otm) = refs
    else:
        (u_ref, h0re_ref, h0im_ref, bblk_ref, lam_ref, cre_ref, cim_ref, d_ref, wa_ref, wg_ref,
         ba_ref, bg_ref, ya_ref, hre_ref, him_ref, utm, bre, bim, otm, tmp) = refs

    @pl.when(pl.program_id(0) == 0)
    def _():
        if prompt:
            hre_ref[...] = jnp.zeros_like(hre_ref)
            him_ref[...] = jnp.zeros_like(him_ref)
        else:
            hre_ref[...] = h0re_ref[...]
            him_ref[...] = h0im_ref[...]

    for c in range(4):
        ls = slice(c * LANES, (c + 1) * LANES)
        if prompt:
            for b in range(nb):
                utm[c, pl.ds(b, t_blk, stride=nb), :] = u_ref[b, :, ls]
        else:
            tmp[c] = u_ref[:, ls]
            for t in range(t_blk):
                utm[c, t * nb:(t + 1) * nb, :] = tmp[c, pl.ds(t, nb, stride=t_blk), :]

    for c in range(4):
        r = _dot(utm[c].astype(BF16), bblk_ref[c])
        bre[:, c * STATE_COLS:(c + 1) * STATE_COLS] = r[:, :STATE_COLS]
        bim[:, c * STATE_COLS:(c + 1) * STATE_COLS] = r[:, STATE_COLS:]

    for c in range(N_SCOL):
        cs = slice(c * STATE_COLS, (c + 1) * STATE_COLS)
        lr = jnp.broadcast_to(lam_ref[0:1, cs], (SUBLANES, STATE_COLS))
        li = jnp.broadcast_to(lam_ref[1:2, cs], (SUBLANES, STATE_COLS))

        def group_body(bg, carry, cs=cs, lr=lr, li=li):
            r0 = pl.multiple_of(bg * SUBLANES, SUBLANES)

            def t_body(t, h):
                hr, hi = h
                row = pl.multiple_of(t * nb + r0, SUBLANES)
                nr = lr * hr - li * hi + bre[pl.ds(row, SUBLANES), cs]
                ni = lr * hi + li * hr + bim[pl.ds(row, SUBLANES), cs]
                bre[pl.ds(row, SUBLANES), cs] = nr
                bim[pl.ds(row, SUBLANES), cs] = ni
                return nr, ni

            h0 = (hre_ref[pl.ds(r0, SUBLANES), cs], him_ref[pl.ds(r0, SUBLANES), cs])
            hr, hi = lax.fori_loop(0, t_blk, t_body, h0, unroll=min(t_blk, 8))
            hre_ref[pl.ds(r0, SUBLANES), cs] = hr
            him_ref[pl.ds(r0, SUBLANES), cs] = hi
            return carry

        lax.fori_loop(0, nb // SUBLANES, group_body, 0)

    ys = []
    for c in range(4):
        cs = slice(c * STATE_COLS, (c + 1) * STATE_COLS)
        ls = slice(c * LANES, (c + 1) * LANES)
        y = _dot(bre[:, cs].astype(BF16), cre_ref[c]) + _dot(bim[:, cs].astype(BF16), cim_ref[c])
        ys.append(_gelu(y + d_ref[:, ls] * utm[c]).astype(BF16))
    yb = jnp.concatenate(ys, axis=1)
    za = _dot(yb, wa_ref[...]) + ba_ref[...]
    zg = _dot(yb, wg_ref[...]) + bg_ref[...]
    o = za * _sigmoid(zg)

    for c in range(4):
        ls = slice(c * LANES, (c + 1) * LANES)
        if prompt:
            otm[c] = o[:, ls]
            for b in range(nb):
                ya_ref[b, :, ls] = otm[c, pl.ds(b, t_blk, stride=nb), :]
        else:
            for t in range(t_blk):
                otm[c, pl.ds(t, nb, stride=t_blk), :] = o[t * nb:(t + 1) * nb, ls]
            ya_ref[:, ls] = otm[c]


def _s5_weight_specs(fix):
    return [
        pl.BlockSpec((4, LANES, 2 * STATE_COLS), fix(3)),
        pl.BlockSpec((2, N_STATE), fix(2)),
        pl.BlockSpec((4, STATE_COLS, LANES), fix(3)),
        pl.BlockSpec((4, STATE_COLS, LANES), fix(3)),
        pl.BlockSpec((1, MIX_A), fix(2)),
        pl.BlockSpec((MIX_A, MIX_A), fix(2)),
        pl.BlockSpec((MIX_A, MIX_A), fix(2)),
        pl.BlockSpec((1, MIX_A), fix(2)),
        pl.BlockSpec((1, MIX_A), fix(2)),
    ]


def _s5_prompt(z3, weights, t_blk):
    nb, seq = z3.shape[0], z3.shape[1]
    rows = nb * t_blk
    fix = lambda n: (lambda j: (0,) * n)
    return pl.pallas_call(
        functools.partial(_s5_kernel, nb=nb, t_blk=t_blk, prompt=True),
        grid=(seq // t_blk,),
        in_specs=[pl.BlockSpec((nb, t_blk, MIX_A), lambda j: (0, j, 0))] + _s5_weight_specs(fix),
        out_specs=[
            pl.BlockSpec((nb, t_blk, MIX_A), lambda j: (0, j, 0)),
            pl.BlockSpec((nb, N_STATE), lambda j: (0, 0)),
            pl.BlockSpec((nb, N_STATE), lambda j: (0, 0)),
        ],
        out_shape=[
            jax.ShapeDtypeStruct((nb, seq, MIX_A), F32),
            jax.ShapeDtypeStruct((nb, N_STATE), F32),
            jax.ShapeDtypeStruct((nb, N_STATE), F32),
        ],
        scratch_shapes=[
            pltpu.VMEM((4, rows, LANES), F32),
            pltpu.VMEM((rows, N_STATE), F32),
            pltpu.VMEM((rows, N_STATE), F32),
            pltpu.VMEM((4, rows, LANES), F32),
        ],
        compiler_params=pltpu.CompilerParams(
            dimension_semantics=("arbitrary",), vmem_limit_bytes=VMEM_LIMIT),
        name="s5_prompt",
    )(z3, *weights)


def _s5_sample(z, h0re, h0im, weights, nb, t_blk):
    rows = nb * t_blk
    fix = lambda n: (lambda j: (0,) * n)
    return pl.pallas_call(
        functools.partial(_s5_kernel, nb=nb, t_blk=t_blk, prompt=False),
        grid=(1,),
        in_specs=[
            pl.BlockSpec((rows, MIX_A), lambda j: (0, 0)),
            pl.BlockSpec((nb, N_STATE), lambda j: (0, 0)),
            pl.BlockSpec((nb, N_STATE), lambda j: (0, 0)),
        ] + _s5_weight_specs(fix),
        out_specs=[
            pl.BlockSpec((rows, MIX_A), lambda j: (0, 0)),
            pl.BlockSpec((nb, N_STATE), lambda j: (0, 0)),
            pl.BlockSpec((nb, N_STATE), lambda j: (0, 0)),
        ],
        out_shape=[
            jax.ShapeDtypeStruct((rows, MIX_A), F32),
            jax.ShapeDtypeStruct((nb, N_STATE), F32),
            jax.ShapeDtypeStruct((nb, N_STATE), F32),
        ],
        scratch_shapes=[
            pltpu.VMEM((4, rows, LANES), F32),
            pltpu.VMEM((rows, N_STATE), F32),
            pltpu.VMEM((rows, N_STATE), F32),
            pltpu.VMEM((4, rows, LANES), F32),
            pltpu.VMEM((4, rows, LANES), F32),
        ],
        compiler_params=pltpu.CompilerParams(
            dimension_semantics=("arbitrary",), vmem_limit_bytes=VMEM_LIMIT),
        name="s5_sample",
    )(z, h0re, h0im, *weights)


def _group_layernorm(gv, gm_ref, lng_ref, lnb_ref):
    def gmean(a):
        hi, lo = _split_bf16(a)
        return _dot(hi, gm_ref[...]) + _dot(lo, gm_ref[...])

    xc = gv - gmean(gv)
    var = gmean(xc * xc)
    return xc * lax.rsqrt(var + EPS) * lng_ref[...] + lnb_ref[...]


def _sgu_prompt_kernel(u_ref, v_ref, ws_ref, bias_ref, lng_ref, lnb_ref, gm_ref, o_ref, *, n_chunks):
    row = lax.broadcasted_iota(jnp.int32, (CHUNK, 2 * CHUNK), 0)
    col = lax.broadcasted_iota(jnp.int32, (CHUNK, 2 * CHUNK), 1)
    causal = (col % CHUNK) <= row
    lane = lax.broadcasted_iota(jnp.int32, (CHUNK, LANES), 1)
    first_head = lane < HD_B
    wcat = []
    for p in range(NH_B // 2):
        w = jnp.concatenate([ws_ref[2 * p], ws_ref[2 * p + 1]], axis=1)
        wcat.append(jnp.where(causal, w, 0.0).astype(BF16))
    for n in range(n_chunks):
        rs = slice(n * CHUNK, (n + 1) * CHUNK)
        u = _gelu(u_ref[rs, :])
        vn = _group_layernorm(_gelu(v_ref[rs, :]), gm_ref, lng_ref, lnb_ref)
        for p in range(NH_B // 2):
            ls = slice(p * LANES, (p + 1) * LANES)
            vp = vn[:, ls]
            rhs = jnp.concatenate(
                [jnp.where(first_head, vp, 0.0), jnp.where(first_head, 0.0, vp)], axis=0).astype(BF16)
            s = _dot(wcat[p], rhs)
            o_ref[rs, ls] = u[:, ls] * (s + bias_ref[:, ls])


def _sgu_prompt(z, ws, bias, lng, lnb, gm, n_chunks):
    m = z.shape[0]
    tm = n_chunks * CHUNK
    fix2 = lambda i: (0, 0)
    return pl.pallas_call(
        functools.partial(_sgu_prompt_kernel, n_chunks=n_chunks),
        grid=(m // tm,),
        in_specs=[
            pl.BlockSpec((tm, MIX_B), lambda i: (i, 1)),
            pl.BlockSpec((tm, MIX_B), lambda i: (i, 2)),
            pl.BlockSpec((NH_B, CHUNK, CHUNK), lambda i: (0, 0, 0)),
            pl.BlockSpec((CHUNK, MIX_B), fix2),
            pl.BlockSpec((1, MIX_B), fix2),
            pl.BlockSpec((1, MIX_B), fix2),
            pl.BlockSpec((MIX_B, MIX_B), fix2),
        ],
        out_specs=pl.BlockSpec((tm, MIX_B), lambda i: (i, 0)),
        out_shape=jax.ShapeDtypeStruct((m, MIX_B), F32),
        compiler_params=pltpu.CompilerParams(
            dimension_semantics=("parallel",), vmem_limit_bytes=VMEM_LIMIT),
        name="sgu_prompt",
    )(z, z, ws, bias, lng, lnb, gm)


def _sgu_sample_kernel(u_ref, v_ref, coef_ref, bias_ref, lng_ref, lnb_ref, gm_ref, o_ref, vn_ref,
                       usc, vsc, osc, *, nb, t_blk):
    vn = _group_layernorm(_gelu(v_ref[...]), gm_ref, lng_ref, lnb_ref)
    vn_ref[...] = vn
    u = _gelu(u_ref[...])
    for c in range(4):
        ls = slice(c * LANES, (c + 1) * LANES)
        vsc[c] = vn[:, ls]
        usc[c] = u[:, ls]
        for q in range(t_blk):
            s = bias_ref[q:q + 1, ls]
            for k in range(q + 1):
                s = s + coef_ref[q * t_blk + k:q * t_blk + k + 1, ls] * vsc[c, pl.ds(k, nb, stride=t_blk), :]
            osc[c, pl.ds(q, nb, stride=t_blk), :] = usc[c, pl.ds(q, nb, stride=t_blk), :] * s
        o_ref[:, ls] = osc[c]


def _sgu_sample(z, coef, bias, lng, lnb, gm, nb, t_blk):
    m = nb * t_blk
    fix2 = lambda i: (0, 0)
    return pl.pallas_call(
        functools.partial(_sgu_sample_kernel, nb=nb, t_blk=t_blk),
        grid=(1,),
        in_specs=[
            pl.BlockSpec((m, MIX_B), lambda i: (0, 1)),
            pl.BlockSpec((m, MIX_B), lambda i: (0, 2)),
            pl.BlockSpec((t_blk * t_blk, MIX_B), fix2),
            pl.BlockSpec((t_blk, MIX_B), fix2),
            pl.BlockSpec((1, MIX_B), fix2),
            pl.BlockSpec((1, MIX_B), fix2),
            pl.BlockSpec((MIX_B, MIX_B), fix2),
        ],
        out_specs=[pl.BlockSpec((m, MIX_B), fix2), pl.BlockSpec((m, MIX_B), fix2)],
        out_shape=[jax.ShapeDtypeStruct((m, MIX_B), F32), jax.ShapeDtypeStruct((m, MIX_B), F32)],
        scratch_shapes=[pltpu.VMEM((4, m, LANES), F32)] * 3,
        compiler_params=pltpu.CompilerParams(
            dimension_semantics=("arbitrary",), vmem_limit_bytes=VMEM_LIMIT),
        name="sgu_sample",
    )(z, z, coef, bias, lng, lnb, gm)


def _top2(logits):
    lane = lax.broadcasted_iota(jnp.int32, logits.shape, 1).astype(F32)
    neg = jnp.float32(-jnp.inf)
    lg = jnp.where(lane < N_EXPERTS, logits, neg)
    m1 = jnp.max(lg, axis=1, keepdims=True)
    i1 = jnp.min(jnp.where(lg == m1, lane, float(LANES)), axis=1, keepdims=True)
    lg2 = jnp.where(lane == i1, neg, lg)
    m2 = jnp.max(lg2, axis=1, keepdims=True)
    i2 = jnp.min(jnp.where(lg2 == m2, lane, float(LANES)), axis=1, keepdims=True)
    ex = jnp.exp(m2 - m1)
    w1 = 1.0 / (1.0 + ex)
    w2 = ex / (1.0 + ex)
    gates = jnp.where(lane == i1, w1, 0.0) + jnp.where(lane == i2, w2, 0.0)
    mask = jnp.where((lane == i1) | (lane == i2), 1.0, 0.0)
    return gates, mask


def _swiglu(x, wgu_ref, wd_ref):
    h = _dot(x, wgu_ref[...])
    hg = h[:, :FF_PAD]
    a = (hg * _sigmoid(hg) * h[:, FF_PAD:]).astype(BF16)
    return _dot(a, wd_ref[...])


def _out_proj(res_refs, ya_ref, yb_ref, ga_ref, gb_ref, wout_ref):
    x = res_refs[0][...]
    for r in res_refs[1:]:
        x = x + r[...]
    na = _rms(ya_ref[...], ga_ref[...]).astype(BF16)
    nb = _rms(yb_ref[...], gb_ref[...]).astype(BF16)
    return x + _dot(na, wout_ref[0:MIX_A, :]) + _dot(nb, wout_ref[MIX_A:, :])


def _ffn_kernel(*refs, n_res, n_e):
    (ya_ref, yb_ref, ga_ref, gb_ref, wout_ref, gf_ref, wgu_ref, wd_ref,
     o_ref, acc_ref, hn_ref) = refs[n_res:]
    e = pl.program_id(1)

    @pl.when(e == 0)
    def _():
        x1 = _out_proj(refs[:n_res], ya_ref, yb_ref, ga_ref, gb_ref, wout_ref)
        hn_ref[...] = _rms(x1, gf_ref[...]).astype(BF16)
        acc_ref[...] = x1

    acc_ref[...] += _swiglu(hn_ref[...], wgu_ref, wd_ref)

    @pl.when(e == n_e - 1)
    def _():
        o_ref[...] = acc_ref[...]


def _ffn(xs, ya, yb, ga, gb, wout, gf, wgu, wd, *, tm):
    m = xs[0].shape[0]
    n_e = wgu.shape[0]
    fix2 = lambda i, e: (0, 0)
    return pl.pallas_call(
        functools.partial(_ffn_kernel, n_res=len(xs), n_e=n_e),
        grid=(m // tm, n_e),
        in_specs=[pl.BlockSpec((tm, D_MODEL), lambda i, e: (i, 0)) for _ in xs] + [
            pl.BlockSpec((tm, MIX_A), lambda i, e: (i, 0)),
            pl.BlockSpec((tm, MIX_B), lambda i, e: (i, 0)),
            pl.BlockSpec((1, MIX_A), fix2),
            pl.BlockSpec((1, MIX_B), fix2),
            pl.BlockSpec((D_MODEL, D_MODEL), fix2),
            pl.BlockSpec((1, D_MODEL), fix2),
            pl.BlockSpec((None, D_MODEL, 2 * FF_PAD), lambda i, e: (e, 0, 0)),
            pl.BlockSpec((None, FF_PAD, D_MODEL), lambda i, e: (e, 0, 0)),
        ],
        out_specs=pl.BlockSpec((tm, D_MODEL), lambda i, e: (i, 0)),
        out_shape=jax.ShapeDtypeStruct((m, D_MODEL), F32),
        scratch_shapes=[
            pltpu.VMEM((tm, D_MODEL), F32),
            pltpu.VMEM((tm, D_MODEL), BF16),
        ],
        compiler_params=pltpu.CompilerParams(
            dimension_semantics=("parallel", "arbitrary"), vmem_limit_bytes=VMEM_LIMIT),
        name="ffn_dense",
    )(*xs, ya, yb, ga, gb, wout, gf, wgu, wd)


MOE_WIN = 256
SEG = 16
SORT_ROWS = 2 * MOE_WIN + N_EXPERTS * SEG
ROW_TILE = 256
NOT_ROUTED = -1.0e6


def _pre_moe_kernel(*refs, n_res, n_win):
    (ya_ref, yb_ref, ga_ref, gb_ref, wout_ref, gf_ref, rwh_ref, rwl_ref, rb_ref,
     x1_ref, hn_ref, gates_ref, pm_ref, pmt_ref, cnt_ref) = refs[n_res:]
    x1 = _out_proj(refs[:n_res], ya_ref, yb_ref, ga_ref, gb_ref, wout_ref)
    x1_ref[...] = x1
    hn = _rms(x1, gf_ref[...])
    hn_ref[...] = hn.astype(BF16)
    hi, lo = _split_bf16(hn)
    logits = _dot(hi, rwh_ref[...]) + _dot(lo, rwh_ref[...]) + _dot(hi, rwl_ref[...])
    gates, mask = _top2(logits + rb_ref[...])
    gates_ref[...] = gates
    row = lax.broadcasted_iota(jnp.int32, (MOE_WIN, MOE_WIN), 0)
    col = lax.broadcasted_iota(jnp.int32, (MOE_WIN, MOE_WIN), 1)
    earlier = jnp.where(col < row, 1.0, 0.0).astype(BF16)
    for w in range(n_win):
        rs = slice(w * MOE_WIN, (w + 1) * MOE_WIN)
        mw = mask[rs, :]
        rank = _dot(earlier, mw.astype(BF16))
        pm = jnp.where(mw > 0.0, rank, NOT_ROUTED)
        pm_ref[rs, :] = pm
        pmt_ref[w] = pm.T[:SUBLANES, :]
        cnt_ref[w] = jnp.broadcast_to(jnp.sum(mw, axis=0, keepdims=True), (SUBLANES, LANES))


def _pre_moe(xs, ya, yb, ga, gb, wout, gf, rwh, rwl, rb, *, tm):
    m = xs[0].shape[0]
    n_win = tm // MOE_WIN
    fix2 = lambda i: (0, 0)
    return pl.pallas_call(
        functools.partial(_pre_moe_kernel, n_res=len(xs), n_win=n_win),
        grid=(m // tm,),
        in_specs=[pl.BlockSpec((tm, D_MODEL), lambda i: (i, 0)) for _ in xs] + [
            pl.BlockSpec((tm, MIX_A), lambda i: (i, 0)),
            pl.BlockSpec((tm, MIX_B), lambda i: (i, 0)),
            pl.BlockSpec((1, MIX_A), fix2),
            pl.BlockSpec((1, MIX_B), fix2),
            pl.BlockSpec((D_MODEL, D_MODEL), fix2),
            pl.BlockSpec((1, D_MODEL), fix2),
            pl.BlockSpec((D_MODEL, LANES), fix2),
            pl.BlockSpec((D_MODEL, LANES), fix2),
            pl.BlockSpec((1, LANES), fix2),
        ],
        out_specs=[
            pl.BlockSpec((tm, D_MODEL), lambda i: (i, 0)),
            pl.BlockSpec((tm, D_MODEL), lambda i: (i, 0)),
            pl.BlockSpec((tm, LANES), lambda i: (i, 0)),
            pl.BlockSpec((tm, LANES), lambda i: (i, 0)),
            pl.BlockSpec((n_win, SUBLANES, MOE_WIN), lambda i: (i, 0, 0)),
            pl.BlockSpec((n_win, SUBLANES, LANES), lambda i: (i, 0, 0)),
        ],
        out_shape=[
            jax.ShapeDtypeStruct((m, D_MODEL), F32),
            jax.ShapeDtypeStruct((m, D_MODEL), BF16),
            jax.ShapeDtypeStruct((m, LANES), F32),
            jax.ShapeDtypeStruct((m, LANES), F32),
            jax.ShapeDtypeStruct((m // MOE_WIN, SUBLANES, MOE_WIN), F32),
            jax.ShapeDtypeStruct((m // MOE_WIN, SUBLANES, LANES), F32),
        ],
        compiler_params=pltpu.CompilerParams(
            dimension_semantics=("parallel",), vmem_limit_bytes=VMEM_LIMIT),
        name="pre_moe",
    )(*xs, ya, yb, ga, gb, wout, gf, rwh, rwl, rb)


def _moe_kernel(cnt_sm, hn_ref, gates_ref, pm_ref, pmt_ref, wgu_ref, wd_ref, o_ref,
                xs, gs, sb, gsb, pn_sm, s_sm, off_sm, est_sm, tot_sm, *, n_win):
    blk = pl.program_id(0)
    e = pl.program_id(1)

    def seg_copy(w, ee, to_sorted):
        s0 = s_sm[w * N_EXPERTS + ee]
        o0 = off_sm[w * N_EXPERTS + ee]

        def body(i, carry):
            src = pl.multiple_of(s0 + i * SEG, SEG)
            dst = pl.multiple_of(o0 + i * SEG, SEG)
            if to_sorted:
                xs[pl.ds(dst, SEG), :] = sb[pl.ds(src, SEG), :]
                gs[pl.ds(dst, SEG), :] = gsb[pl.ds(src, SEG), :]
            else:
                sb[pl.ds(src, SEG), :] = xs[pl.ds(dst, SEG), :]
            return carry

        lax.fori_loop(0, pn_sm[w * N_EXPERTS + ee] // SEG, body, 0)

    @pl.when(e == 0)
    def _dispatch():
        pn = [[None] * N_EXPERTS for _ in range(n_win)]
        for w in range(n_win):
            run = jnp.int32(0)
            for ee in range(N_EXPERTS):
                n = cnt_sm[(blk * n_win + w) * N_EXPERTS + ee]
                pn[w][ee] = jnp.bitwise_and(n + (SEG - 1), -SEG)
                pn_sm[w * N_EXPERTS + ee] = pn[w][ee]
                s_sm[w * N_EXPERTS + ee] = run
                run = run + pn[w][ee]
        run = jnp.int32(0)
        for ee in range(N_EXPERTS):
            est_sm[ee] = run
            start = run
            for w in range(n_win):
                off_sm[w * N_EXPERTS + ee] = run
                run = run + pn[w][ee]
            tot_sm[ee] = run - start
        xs[...] = jnp.zeros_like(xs)
        gs[...] = jnp.zeros_like(gs)
        riota = lax.broadcasted_iota(jnp.int32, (SORT_ROWS, MOE_WIN), 0).astype(F32)
        for w in range(n_win):
            rs = slice(w * MOE_WIN, (w + 1) * MOE_WIN)
            g = jnp.zeros((SORT_ROWS, MOE_WIN), F32)
            for ee in range(N_EXPERTS):
                dest = pmt_ref[w, ee:ee + 1, :] + s_sm[w * N_EXPERTS + ee].astype(F32)
                g = jnp.where(riota == dest, 1.0, g)
            gb = g.astype(BF16)
            sb[...] = _dot(gb, hn_ref[rs, :]).astype(BF16)
            gh, gl = _split_bf16(gates_ref[rs, :])
            gsb[...] = _dot(gb, gh) + _dot(gb, gl)
            for ee in range(N_EXPERTS):
                seg_copy(w, ee, True)

    start = est_sm[e]
    tot = tot_sm[e]

    def row_tile(r0, size, valid):
        r0 = pl.multiple_of(r0, SEG)
        xt = xs[pl.ds(r0, size), :]
        y = _swiglu(xt, wgu_ref, wd_ref)
        lane = lax.broadcasted_iota(jnp.int32, (size, LANES), 1)
        gate = jnp.sum(jnp.where(lane == e, gs[pl.ds(r0, size), :], 0.0), axis=1, keepdims=True)
        keep = lax.broadcasted_iota(jnp.int32, (size, D_MODEL), 0) < valid
        xs[pl.ds(r0, size), :] = jnp.where(keep, (y * gate).astype(BF16), xt)

    n_full = tot // ROW_TILE

    def full_body(i, carry):
        row_tile(start + i * ROW_TILE, ROW_TILE, ROW_TILE)
        return carry

    lax.fori_loop(0, n_full, full_body, 0)
    rem = tot - n_full * ROW_TILE
    tail = start + n_full * ROW_TILE

    @pl.when(rem > ROW_TILE // 2)
    def _():
        row_tile(tail, ROW_TILE, rem)

    @pl.when((rem > 0) & (rem <= ROW_TILE // 2))
    def _():
        row_tile(tail, ROW_TILE // 2, rem)

    @pl.when(e == N_EXPERTS - 1)
    def _combine():
        liota = lax.broadcasted_iota(jnp.int32, (MOE_WIN, SORT_ROWS), 1).astype(F32)
        for w in range(n_win):
            rs = slice(w * MOE_WIN, (w + 1) * MOE_WIN)
            for ee in range(N_EXPERTS):
                seg_copy(w, ee, False)
            g = jnp.zeros((MOE_WIN, SORT_ROWS), F32)
            for ee in range(N_EXPERTS):
                dest = pm_ref[rs, ee:ee + 1] + s_sm[w * N_EXPERTS + ee].astype(F32)
                g = jnp.where(liota == dest, 1.0, g)
            o_ref[rs, :] = _dot(g.astype(BF16), sb[...])


def _moe(cnt, hn, gates, pm, pmt, wgu, wd, *, tb):
    m = hn.shape[0]
    n_win = tb // MOE_WIN
    xs_rows = 2 * tb + n_win * N_EXPERTS * SEG + ROW_TILE
    one = pl.Buffered(1)
    grid_spec = pltpu.PrefetchScalarGridSpec(
        num_scalar_prefetch=1,
        grid=(m // tb, N_EXPERTS),
        in_specs=[
            pl.BlockSpec((tb, D_MODEL), lambda i, e, c: (i, 0), pipeline_mode=one),
            pl.BlockSpec((tb, LANES), lambda i, e, c: (i, 0), pipeline_mode=one),
            pl.BlockSpec((tb, LANES), lambda i, e, c: (i, 0), pipeline_mode=one),
            pl.BlockSpec((n_win, SUBLANES, MOE_WIN), lambda i, e, c: (i, 0, 0), pipeline_mode=one),
            pl.BlockSpec((None, D_MODEL, 2 * FF_PAD), lambda i, e, c: (e, 0, 0)),
            pl.BlockSpec((None, FF_PAD, D_MODEL), lambda i, e, c: (e, 0, 0)),
        ],
        out_specs=pl.BlockSpec((tb, D_MODEL), lambda i, e, c: (i, 0), pipeline_mode=one),
        scratch_shapes=[
            pltpu.VMEM((xs_rows, D_MODEL), BF16),
            pltpu.VMEM((xs_rows, LANES), F32),
            pltpu.VMEM((SORT_ROWS, D_MODEL), BF16),
            pltpu.VMEM((SORT_ROWS, LANES), F32),
            pltpu.SMEM((n_win * N_EXPERTS,), jnp.int32),
            pltpu.SMEM((n_win * N_EXPERTS,), jnp.int32),
            pltpu.SMEM((n_win * N_EXPERTS,), jnp.int32),
            pltpu.SMEM((N_EXPERTS,), jnp.int32),
            pltpu.SMEM((N_EXPERTS,), jnp.int32),
        ],
    )
    return pl.pallas_call(
        functools.partial(_moe_kernel, n_win=n_win),
        grid_spec=grid_spec,
        out_shape=jax.ShapeDtypeStruct((m, D_MODEL), F32),
        compiler_params=pltpu.CompilerParams(
            dimension_semantics=("arbitrary", "arbitrary"), vmem_limit_bytes=MOE_VMEM_LIMIT),
        name="moe",
    )(cnt, hn, gates, pm, pmt, wgu, wd)


def _final_norm_kernel(*refs, n_res):
    g_ref, o_ref = refs[n_res:]
    x = refs[0][...]
    for r in refs[1:n_res]:
        x = x + r[...]
    o_ref[...] = _rms(x, g_ref[...])


def _final_norm(xs, g, tm):
    m = xs[0].shape[0]
    return pl.pallas_call(
        functools.partial(_final_norm_kernel, n_res=len(xs)),
        grid=(m // tm,),
        in_specs=[pl.BlockSpec((tm, D_MODEL), lambda i: (i, 0)) for _ in xs]
        + [pl.BlockSpec((1, D_MODEL), lambda i: (0, 0))],
        out_specs=pl.BlockSpec((tm, D_MODEL), lambda i: (i, 0)),
        out_shape=jax.ShapeDtypeStruct((m, D_MODEL), F32),
        compiler_params=pltpu.CompilerParams(
            dimension_semantics=("parallel",), vmem_limit_bytes=VMEM_LIMIT),
        name="final_norm",
    )(*xs, g)


def _pack_ffn(wg, wu, wd):
    pad = FF_PAD - wg.shape[-1]
    wgu = jnp.concatenate(
        [jnp.pad(wg, ((0, 0), (0, 0), (0, pad))), jnp.pad(wu, ((0, 0), (0, 0), (0, pad)))], axis=-1)
    return wgu.astype(BF16), jnp.pad(wd, ((0, 0), (0, pad), (0, 0))).astype(BF16)


def kernel(x_prompt, x_sample, state_ssm_re, state_ssm_im, norm_mix, w_in, ssm_a_re, ssm_a_im, ssm_log_dt, ssm_b_re, ssm_b_im, ssm_c_re, ssm_c_im, ssm_d, glu_w, glu_b, sgu_w, sgu_b, sgu_ln_g, sgu_ln_b, out_norm_a, out_norm_b, w_out, norm_ffn, ffn_w_gate, ffn_w_up, ffn_w_down, router_w, router_b, moe_w_gate, moe_w_up, moe_w_down, norm_final):
    batch, seq = x_prompt.shape[0], x_prompt.shape[1]
    dec_batch, dec_seq = x_sample.shape[0], x_sample.shape[1]
    mp, ms = batch * seq, dec_batch * dec_seq

    lam, bblk = _s5_prep(ssm_a_re, ssm_a_im, ssm_log_dt, ssm_b_re, ssm_b_im)
    cre = _c_blocks(ssm_c_re).astype(BF16)
    cim = (-_c_blocks(ssm_c_im)).astype(BF16)
    dskip = ssm_d.reshape(DEPTH, 1, MIX_A)
    wa = _glu_blocks(glu_w[..., :SSM_GROUP]).astype(BF16)
    wg = _glu_blocks(glu_w[..., SSM_GROUP:]).astype(BF16)
    ba = glu_b[..., :SSM_GROUP].reshape(DEPTH, 1, MIX_A)
    bg = glu_b[..., SSM_GROUP:].reshape(DEPTH, 1, MIX_A)
    w_in_b = w_in.astype(BF16)
    w_out_b = w_out.astype(BF16)
    gm = jnp.kron(jnp.eye(NH_B, dtype=F32), jnp.full((HD_B, HD_B), 1.0 / HD_B, F32)).astype(BF16)
    sgu_bias = jnp.repeat(jnp.swapaxes(sgu_b, 1, 2), HD_B, axis=2)
    lng = sgu_ln_g.reshape(DEPTH, 1, MIX_B)
    lnb = sgu_ln_b.reshape(DEPTH, 1, MIX_B)
    coef_s = jnp.repeat(
        jnp.transpose(sgu_w[:, :, :dec_seq, :dec_seq], (0, 2, 3, 1)).reshape(DEPTH, dec_seq * dec_seq, NH_B),
        HD_B, axis=2)
    dense_gu, dense_d = _pack_ffn(
        ffn_w_gate.reshape(-1, D_MODEL, 2, D_FF_EXPERT).transpose(0, 2, 1, 3).reshape(-1, D_MODEL, D_FF_EXPERT),
        ffn_w_up.reshape(-1, D_MODEL, 2, D_FF_EXPERT).transpose(0, 2, 1, 3).reshape(-1, D_MODEL, D_FF_EXPERT),
        ffn_w_down.reshape(-1, D_FF_EXPERT, D_MODEL))
    dense_gu = dense_gu.reshape(-1, 2, D_MODEL, 2 * FF_PAD)
    dense_d = dense_d.reshape(-1, 2, FF_PAD, D_MODEL)
    n_moe = moe_w_gate.shape[0]
    moe_gu, moe_d = _pack_ffn(
        moe_w_gate.reshape(-1, D_MODEL, D_FF_EXPERT), moe_w_up.reshape(-1, D_MODEL, D_FF_EXPERT),
        moe_w_down.reshape(-1, D_FF_EXPERT, D_MODEL))
    moe_gu = moe_gu.reshape(n_moe, N_EXPERTS, D_MODEL, 2 * FF_PAD)
    moe_d = moe_d.reshape(n_moe, N_EXPERTS, FF_PAD, D_MODEL)
    rw = jnp.pad(router_w, ((0, 0), (0, 0), (0, LANES - N_EXPERTS)))
    rwh = rw.astype(BF16)
    rwl = (rw - rwh.astype(F32)).astype(BF16)
    rb = jnp.pad(router_b, ((0, 0), (0, LANES - N_EXPERTS))).reshape(n_moe, 1, LANES)
    gfin = norm_final.reshape(1, D_MODEL)

    xp = x_prompt.reshape(mp, D_MODEL)
    xs = x_sample.reshape(ms, D_MODEL)
    h0re = state_ssm_re.reshape(DEPTH, dec_batch, N_STATE)
    h0im = state_ssm_im.reshape(DEPTH, dec_batch, N_STATE)

    def post_mixer(xres, ya, yb, l, tm, tb):
        common = (out_norm_a[l].reshape(1, MIX_A), out_norm_b[l].reshape(1, MIX_B), w_out_b[l],
                  norm_ffn[l].reshape(1, D_MODEL))
        j = l // 2
        if l % 2 == 0:
            return (_ffn(xres, ya, yb, *common, dense_gu[j], dense_d[j], tm=tm),)
        x1, hn, gates, pm, pmt, cnt = _pre_moe(xres, ya, yb, *common, rwh[j], rwl[j], rb[j], tm=tm)
        cnt = cnt[:, 0, :N_EXPERTS].astype(jnp.int32).reshape(-1)
        return (x1, _moe(cnt, hn, gates, pm, pmt, moe_gu[j], moe_d[j], tb=tb))

    xp = (xp,)
    xs = (xs,)
    re_p, im_p, re_s, im_s, v_s = [], [], [], [], []
    for l in range(DEPTH):
        g_mix = norm_mix[l].reshape(1, D_MODEL)
        s5w = (bblk[l], lam[l], cre[l], cim[l], dskip[l], wa[l], wg[l], ba[l], bg[l])
        zp = _inproj(xp, g_mix, w_in_b[l], tm=1024)
        zs = _inproj(xs, g_mix, w_in_b[l], tm=ms)
        ya_p, hre_p, him_p = _s5_prompt(zp.reshape(batch, seq, IN_WIDTH), s5w, t_blk=64)
        ya_s, hre_s, him_s = _s5_sample(zs, h0re[l], h0im[l], s5w, nb=dec_batch, t_blk=dec_seq)
        yb_p = _sgu_prompt(zp, sgu_w[l], sgu_bias[l], lng[l], lnb[l], gm, n_chunks=4)
        yb_s, vn_s = _sgu_sample(zs, coef_s[l], sgu_bias[l, :dec_seq], lng[l], lnb[l], gm,
                                 nb=dec_batch, t_blk=dec_seq)
        xp = post_mixer(xp, ya_p.reshape(mp, MIX_A), yb_p, l, tm=512, tb=seq)
        xs = post_mixer(xs, ya_s, yb_s, l, tm=ms, tb=ms)
        re_p.append(hre_p)
        im_p.append(him_p)
        re_s.append(hre_s)
        im_s.append(him_s)
        v_s.append(vn_s)
    xp = _final_norm(xp, gfin, tm=1024)
    xs = _final_norm(xs, gfin, tm=ms)

    st = lambda hs, b: jnp.stack(hs).reshape(DEPTH, b, NG_A, SSM_STATE)
    return (xp.reshape(batch, seq, D_MODEL), xs.reshape(dec_batch, dec_seq, D_MODEL),
            st(re_p, batch), st(im_p, batch), st(re_s, dec_batch), st(im_s, dec_batch),
            jnp.stack(v_s).reshape(DEPTH, dec_batch, dec_seq, NH_B, HD_B))
```

```python
import functools
import math

import jax
import jax.numpy as jnp
from jax import lax
from jax.experimental import pallas as pl
from jax.experimental.pallas import tpu as pltpu

F32 = jnp.float32
BF16 = jnp.bfloat16

D_MODEL = 1024
DEPTH = 4
MIX_A = 512
SSM_GROUP = 16
NG_A = 32
SSM_STATE = 64
N_STATE = NG_A * SSM_STATE
MIX_B = 512
HD_B = 64
NH_B = 8
CHUNK = 128
IN_WIDTH = MIX_A + 2 * MIX_B
D_FF = 2752
N_EXPERTS = 8
D_FF_EXPERT = D_FF // 2
FF_PAD = 1408
EPS = 1e-6

LANES = 128
SUBLANES = 8
MXU_DIM = 256
STATE_COLS = 512
N_SCOL = N_STATE // STATE_COLS
GROUPS_PER_CHUNK = STATE_COLS // SSM_STATE
VMEM_LIMIT = 52 * 1024 * 1024
MOE_VMEM_LIMIT = 58 * 1024 * 1024


def _rms(x, g):
    return x * lax.rsqrt(jnp.mean(x * x, axis=-1, keepdims=True) + EPS) * g


def _gelu(x):
    c = math.sqrt(2.0 / math.pi)
    return 0.5 * x * (1.0 + jnp.tanh(c * (x + 0.044715 * (x * x * x))))


def _sigmoid(x):
    return 1.0 / (1.0 + jnp.exp(-x))


def _split_bf16(x):
    hi = x.astype(BF16)
    lo = (x - hi.astype(F32)).astype(BF16)
    return hi, lo


def _dot(a, b):
    return jnp.dot(a, b, preferred_element_type=F32)


def _log2(n):
    assert n & (n - 1) == 0
    return n.bit_length() - 1


def _block_diag_tile(src, n_rep, rows_per_group):
    r_n, k = src.shape
    n = k * n_rep
    t_row = lax.broadcasted_iota(jnp.int32, (k, n), 0)
    t_col = lax.broadcasted_iota(jnp.int32, (k, n), 1)
    tile = jnp.where(jnp.bitwise_and(t_col, k - 1) == t_row, 1.0, 0.0).astype(BF16)
    full = _dot(src.astype(BF16), tile)
    row = lax.broadcasted_iota(jnp.int32, (r_n, n), 0)
    col = lax.broadcasted_iota(jnp.int32, (r_n, n), 1)
    same = lax.shift_right_logical(row, _log2(rows_per_group)) == lax.shift_right_logical(col, _log2(k))
    return jnp.where(same, full, 0.0).astype(BF16)


def _pack_cols_kernel(w_ref, o_ref, *, widths):
    for i, width in enumerate(widths):
        o_ref[i, :, :width] = w_ref[:, i * FF_PAD:i * FF_PAD + width].astype(BF16)
        if width < FF_PAD:
            o_ref[i, :, width:] = jnp.zeros((o_ref.shape[1], FF_PAD - width), BF16)


def _pack_rows_kernel(w_ref, o_ref, *, widths):
    for i, width in enumerate(widths):
        o_ref[i, :width, :] = w_ref[i * FF_PAD:i * FF_PAD + width, :].astype(BF16)
        if width < FF_PAD:
            o_ref[i, width:, :] = jnp.zeros((FF_PAD - width, o_ref.shape[2]), BF16)


def _pack(w, widths, cols):
    n_e, n_parts = w.shape[0], len(widths)
    out_blk = (n_parts, D_MODEL, FF_PAD) if cols else (n_parts, FF_PAD, D_MODEL)
    out = pl.pallas_call(
        functools.partial(_pack_cols_kernel if cols else _pack_rows_kernel, widths=widths),
        grid=(n_e,),
        in_specs=[pl.BlockSpec((None,) + w.shape[1:], lambda e: (e, 0, 0))],
        out_specs=pl.BlockSpec((None,) + out_blk, lambda e: (e, 0, 0, 0)),
        out_shape=jax.ShapeDtypeStruct((n_e,) + out_blk, BF16),
        compiler_params=pltpu.CompilerParams(
            dimension_semantics=("parallel",), vmem_limit_bytes=VMEM_LIMIT),
        name="pack_cols" if cols else "pack_rows",
    )(w)
    return out.reshape((n_e * n_parts,) + out_blk[1:])


def _s5_prep_kernel(are_ref, aim_ref, ldt_ref, bre_ref, bim_ref,
                    abre_ref, abim_ref, bbre_ref, bbim_ref):
    lam_re = are_ref[...]
    lam_im = aim_ref[...]
    dt = jnp.exp(ldt_ref[...])
    mag = jnp.exp(lam_re * dt)
    ab_re = mag * jnp.cos(lam_im * dt)
    ab_im = mag * jnp.sin(lam_im * dt)
    den = lam_re * lam_re + lam_im * lam_im
    nr = ab_re - 1.0
    q_re = (nr * lam_re + ab_im * lam_im) / den
    q_im = (ab_im * lam_re - nr * lam_im) / den
    b_re = bre_ref[...]
    b_im = bim_ref[...]
    abre_ref[...] = ab_re
    abim_ref[...] = ab_im
    bbre_ref[...] = q_re * b_re - q_im * b_im
    bbim_ref[...] = q_re * b_im + q_im * b_re


def _s5_prep(ssm_a_re, ssm_a_im, ssm_log_dt, ssm_b_re, ssm_b_im):
    shp = (DEPTH, NG_A, SSM_GROUP, SSM_STATE)
    rows = DEPTH * NG_A * SSM_GROUP
    bc = lambda a: jnp.broadcast_to(a, shp).reshape(rows, SSM_STATE)
    are = bc(ssm_a_re[:, :, None, :])
    aim = bc(ssm_a_im[:, :, None, :])
    ldt = bc(ssm_log_dt[:, :, None, None])
    bre = jnp.swapaxes(ssm_b_re, 2, 3).reshape(rows, SSM_STATE)
    bim = jnp.swapaxes(ssm_b_im, 2, 3).reshape(rows, SSM_STATE)
    sds = jax.ShapeDtypeStruct((rows, SSM_STATE), F32)
    abre, abim, bbre, bbim = pl.pallas_call(
        _s5_prep_kernel, out_shape=(sds, sds, sds, sds), name="s5_prep",
    )(are, aim, ldt, bre, bim)
    lam_re = abre.reshape(shp)[:, :, 0, :].reshape(DEPTH, 1, N_STATE)
    lam_im = abim.reshape(shp)[:, :, 0, :].reshape(DEPTH, 1, N_STATE)
    lam = jnp.concatenate([lam_re, lam_im], axis=1)
    return lam, bbre, bbim


def _inproj_kernel(*refs, n_res):
    g_ref, w_ref, z_ref = refs[n_res:]
    x = refs[0][...]
    for r in refs[1:n_res]:
        x = x + r[...]
    hn = _rms(x, g_ref[...])
    z_ref[...] = _dot(hn.astype(BF16), w_ref[...])


def _inproj(xs, g, w, l, tm):
    m = xs[0].shape[0]
    return pl.pallas_call(
        functools.partial(_inproj_kernel, n_res=len(xs)),
        grid=(m // tm,),
        in_specs=[pl.BlockSpec((tm, D_MODEL), lambda i: (i, 0)) for _ in xs] + [
            pl.BlockSpec((None, 1, D_MODEL), lambda i: (l, 0, 0)),
            pl.BlockSpec((None, D_MODEL, IN_WIDTH), lambda i: (l, 0, 0)),
        ],
        out_specs=pl.BlockSpec((tm, IN_WIDTH), lambda i: (i, 0)),
        out_shape=jax.ShapeDtypeStruct((m, IN_WIDTH), F32),
        compiler_params=pltpu.CompilerParams(
            dimension_semantics=("parallel",), vmem_limit_bytes=VMEM_LIMIT),
        name="inproj",
    )(*xs, g, w)


def _s5_kernel(*refs, nb, t_blk, prompt):
    n_in = 1 if prompt else 3
    (bbre_ref, bbim_ref, lam_ref, ctre_ref, ctim_ref, d_ref, glu_ref, gbias_ref,
     ya_ref, hre_ref, him_ref, utm, bre, bim, otm, bblk, cre, cimn, wa, wg) = refs[n_in:n_in + 20]
    u_ref = refs[0]

    @pl.when(pl.program_id(0) == 0)
    def _():
        if prompt:
            hre_ref[...] = jnp.zeros_like(hre_ref)
            him_ref[...] = jnp.zeros_like(him_ref)
        else:
            hre_ref[...] = refs[1][...]
            him_ref[...] = refs[2][...]
        for c in range(4):
            rs = slice(c * LANES, (c + 1) * LANES)
            bblk[c, :, :STATE_COLS] = _block_diag_tile(bbre_ref[rs, :], GROUPS_PER_CHUNK, SSM_GROUP)
            bblk[c, :, STATE_COLS:] = _block_diag_tile(bbim_ref[rs, :], GROUPS_PER_CHUNK, SSM_GROUP)
            ss = slice(c * STATE_COLS, (c + 1) * STATE_COLS)
            cre[c] = _block_diag_tile(ctre_ref[ss, :], GROUPS_PER_CHUNK, SSM_STATE)
            cimn[c] = _block_diag_tile(-ctim_ref[ss, :], GROUPS_PER_CHUNK, SSM_STATE)
        for b in range(MIX_A // MXU_DIM):
            ms = slice(b * MXU_DIM, (b + 1) * MXU_DIM)
            wa[b] = _block_diag_tile(glu_ref[ms, :SSM_GROUP], MXU_DIM // SSM_GROUP, SSM_GROUP)
            wg[b] = _block_diag_tile(glu_ref[ms, SSM_GROUP:], MXU_DIM // SSM_GROUP, SSM_GROUP)

    for c in range(4):
        ls = slice(c * LANES, (c + 1) * LANES)
        if prompt:
            for b in range(nb):
                utm[c, pl.ds(b, t_blk, stride=nb), :] = u_ref[b, :, ls]
        else:
            tmp = refs[n_in + 20]
            tmp[c] = u_ref[:, ls]
            for t in range(t_blk):
                utm[c, t * nb:(t + 1) * nb, :] = tmp[c, pl.ds(t, nb, stride=t_blk), :]

    for c in range(4):
        r = _dot(utm[c].astype(BF16), bblk[c])
        bre[:, c * STATE_COLS:(c + 1) * STATE_COLS] = r[:, :STATE_COLS]
        bim[:, c * STATE_COLS:(c + 1) * STATE_COLS] = r[:, STATE_COLS:]

    for c in range(N_SCOL):
        cs = slice(c * STATE_COLS, (c + 1) * STATE_COLS)
        lr = jnp.broadcast_to(lam_ref[0:1, cs], (SUBLANES, STATE_COLS))
        li = jnp.broadcast_to(lam_ref[1:2, cs], (SUBLANES, STATE_COLS))

        def group_body(bg, carry, cs=cs, lr=lr, li=li):
            r0 = pl.multiple_of(bg * SUBLANES, SUBLANES)

            def t_body(t, h):
                hr, hi = h
                row = pl.multiple_of(t * nb + r0, SUBLANES)
                nr = lr * hr - li * hi + bre[pl.ds(row, SUBLANES), cs]
                ni = lr * hi + li * hr + bim[pl.ds(row, SUBLANES), cs]
                bre[pl.ds(row, SUBLANES), cs] = nr
                bim[pl.ds(row, SUBLANES), cs] = ni
                return nr, ni

            h0 = (hre_ref[pl.ds(r0, SUBLANES), cs], him_ref[pl.ds(r0, SUBLANES), cs])
            hr, hi = lax.fori_loop(0, t_blk, t_body, h0, unroll=min(t_blk, 8))
            hre_ref[pl.ds(r0, SUBLANES), cs] = hr
            him_ref[pl.ds(r0, SUBLANES), cs] = hi
            return carry

        lax.fori_loop(0, nb // SUBLANES, group_body, 0)

    ys = []
    for c in range(4):
        cs = slice(c * STATE_COLS, (c + 1) * STATE_COLS)
        ls = slice(c * LANES, (c + 1) * LANES)
        y = _dot(bre[:, cs].astype(BF16), cre[c]) + _dot(bim[:, cs].astype(BF16), cimn[c])
        ys.append(_gelu(y + d_ref[:, ls] * utm[c]).astype(BF16))

    os_ = []
    for b in range(MIX_A // MXU_DIM):
        ms = slice(b * MXU_DIM, (b + 1) * MXU_DIM)
        yb = jnp.concatenate(ys[2 * b:2 * b + 2], axis=1)
        za = _dot(yb, wa[b]) + gbias_ref[0:1, ms]
        zg = _dot(yb, wg[b]) + gbias_ref[1:2, ms]
        os_.append(za * _sigmoid(zg))
    o = jnp.concatenate(os_, axis=1)

    for c in range(4):
        ls = slice(c * LANES, (c + 1) * LANES)
        if prompt:
            otm[c] = o[:, ls]
            for b in range(nb):
                ya_ref[b, :, ls] = otm[c, pl.ds(b, t_blk, stride=nb), :]
        else:
            for t in range(t_blk):
                otm[c, pl.ds(t, nb, stride=t_blk), :] = o[t * nb:(t + 1) * nb, ls]
            ya_ref[:, ls] = otm[c]


def _s5_call(u_specs, u_args, out_block, out_shape, weights, l, nb, t_blk, prompt, n_steps):
    rows = nb * t_blk
    lam, bbre, bbim, ctre, ctim, dskip, glu, gbias = weights
    w_specs = [
        pl.BlockSpec((NG_A * SSM_GROUP, SSM_STATE), lambda j: (l, 0)),
        pl.BlockSpec((NG_A * SSM_GROUP, SSM_STATE), lambda j: (l, 0)),
        pl.BlockSpec((None, 2, N_STATE), lambda j: (l, 0, 0)),
        pl.BlockSpec((N_STATE, SSM_GROUP), lambda j: (l, 0)),
        pl.BlockSpec((N_STATE, SSM_GROUP), lambda j: (l, 0)),
        pl.BlockSpec((None, 1, MIX_A), lambda j: (l, 0, 0)),
        pl.BlockSpec((MIX_A, 2 * SSM_GROUP), lambda j: (l, 0)),
        pl.BlockSpec((None, 2, MIX_A), lambda j: (l, 0, 0)),
    ]
    scratch = [
        pltpu.VMEM((4, rows, LANES), F32),
        pltpu.VMEM((rows, N_STATE), F32),
        pltpu.VMEM((rows, N_STATE), F32),
        pltpu.VMEM((4, rows, LANES), F32),
        pltpu.VMEM((4, LANES, 2 * STATE_COLS), BF16),
        pltpu.VMEM((4, STATE_COLS, LANES), BF16),
        pltpu.VMEM((4, STATE_COLS, LANES), BF16),
        pltpu.VMEM((MIX_A // MXU_DIM, MXU_DIM, MXU_DIM), BF16),
        pltpu.VMEM((MIX_A // MXU_DIM, MXU_DIM, MXU_DIM), BF16),
    ]
    if not prompt:
        scratch.append(pltpu.VMEM((4, rows, LANES), F32))
    return pl.pallas_call(
        functools.partial(_s5_kernel, nb=nb, t_blk=t_blk, prompt=prompt),
        grid=(n_steps,),
        in_specs=u_specs + w_specs,
        out_specs=[
            out_block,
            pl.BlockSpec((nb, N_STATE), lambda j: (0, 0)),
            pl.BlockSpec((nb, N_STATE), lambda j: (0, 0)),
        ],
        out_shape=[
            out_shape,
            jax.ShapeDtypeStruct((nb, N_STATE), F32),
            jax.ShapeDtypeStruct((nb, N_STATE), F32),
        ],
        scratch_shapes=scratch,
        compiler_params=pltpu.CompilerParams(
            dimension_semantics=("arbitrary",), vmem_limit_bytes=VMEM_LIMIT),
        name="s5_prompt" if prompt else "s5_sample",
    )(*u_args, bbre, bbim, lam, ctre, ctim, dskip, glu, gbias)


def _s5_prompt(z3, weights, l, t_blk):
    nb, seq = z3.shape[0], z3.shape[1]
    blk = pl.BlockSpec((nb, t_blk, MIX_A), lambda j: (0, j, 0))
    return _s5_call([blk], [z3], blk, jax.ShapeDtypeStruct((nb, seq, MIX_A), F32),
                    weights, l, nb, t_blk, True, seq // t_blk)


def _s5_sample(z, h0re, h0im, weights, l, nb, t_blk):
    rows = nb * t_blk
    blk = pl.BlockSpec((rows, MIX_A), lambda j: (0, 0))
    st = pl.BlockSpec((None, nb, N_STATE), lambda j: (l, 0, 0))
    return _s5_call([blk, st, st], [z, h0re, h0im], blk, jax.ShapeDtypeStruct((rows, MIX_A), F32),
                    weights, l, nb, t_blk, False, 1)


def _group_layernorm(gv, lng_ref, lnb_ref):
    row = lax.broadcasted_iota(jnp.int32, (MXU_DIM, MXU_DIM), 0)
    col = lax.broadcasted_iota(jnp.int32, (MXU_DIM, MXU_DIM), 1)
    same = lax.shift_right_logical(row, _log2(HD_B)) == lax.shift_right_logical(col, _log2(HD_B))
    gm = jnp.where(same, 1.0 / HD_B, 0.0).astype(BF16)

    def gmean(a):
        hi, lo = _split_bf16(a)
        parts = []
        for b in range(MIX_B // MXU_DIM):
            ms = slice(b * MXU_DIM, (b + 1) * MXU_DIM)
            parts.append(_dot(hi[:, ms], gm) + _dot(lo[:, ms], gm))
        return jnp.concatenate(parts, axis=1)

    xc = gv - gmean(gv)
    var = gmean(xc * xc)
    return xc * lax.rsqrt(var + EPS) * lng_ref[...] + lnb_ref[...]


def _sgu_prompt_kernel(u_ref, v_ref, ws_ref, bias_ref, lng_ref, lnb_ref, o_ref, *, n_chunks):
    row = lax.broadcasted_iota(jnp.int32, (CHUNK, 2 * CHUNK), 0)
    col = lax.broadcasted_iota(jnp.int32, (CHUNK, 2 * CHUNK), 1)
    causal = jnp.bitwise_and(col, CHUNK - 1) <= row
    lane = lax.broadcasted_iota(jnp.int32, (CHUNK, LANES), 1)
    first_head = lane < HD_B
    wcat = []
    for p in range(NH_B // 2):
        w = jnp.concatenate([ws_ref[2 * p], ws_ref[2 * p + 1]], axis=1)
        wcat.append(jnp.where(causal, w, 0.0).astype(BF16))
    for n in range(n_chunks):
        rs = slice(n * CHUNK, (n + 1) * CHUNK)
        u = _gelu(u_ref[rs, :])
        vn = _group_layernorm(_gelu(v_ref[rs, :]), lng_ref, lnb_ref)
        for p in range(NH_B // 2):
            ls = slice(p * LANES, (p + 1) * LANES)
            vp = vn[:, ls]
            rhs = jnp.concatenate(
                [jnp.where(first_head, vp, 0.0), jnp.where(first_head, 0.0, vp)], axis=0).astype(BF16)
            s = _dot(wcat[p], rhs)
            o_ref[rs, ls] = u[:, ls] * (s + bias_ref[:, ls])


def _sgu_prompt(z, ws, bias, lng, lnb, l, n_chunks):
    m = z.shape[0]
    tm = n_chunks * CHUNK
    lay3 = lambda i: (l, 0, 0)
    return pl.pallas_call(
        functools.partial(_sgu_prompt_kernel, n_chunks=n_chunks),
        grid=(m // tm,),
        in_specs=[
            pl.BlockSpec((tm, MIX_B), lambda i: (i, 1)),
            pl.BlockSpec((tm, MIX_B), lambda i: (i, 2)),
            pl.BlockSpec((None, NH_B, CHUNK, CHUNK), lambda i: (l, 0, 0, 0)),
            pl.BlockSpec((None, CHUNK, MIX_B), lay3),
            pl.BlockSpec((None, 1, MIX_B), lay3),
            pl.BlockSpec((None, 1, MIX_B), lay3),
        ],
        out_specs=pl.BlockSpec((tm, MIX_B), lambda i: (i, 0)),
        out_shape=jax.ShapeDtypeStruct((m, MIX_B), F32),
        compiler_params=pltpu.CompilerParams(
            dimension_semantics=("parallel",), vmem_limit_bytes=VMEM_LIMIT),
        name="sgu_prompt",
    )(z, z, ws, bias, lng, lnb)


def _sgu_sample_kernel(u_ref, v_ref, coef_ref, bias_ref, lng_ref, lnb_ref, o_ref, vn_ref,
                       usc, vsc, osc, *, nb, t_blk):
    vn = _group_layernorm(_gelu(v_ref[...]), lng_ref, lnb_ref)
    vn_ref[...] = vn
    u = _gelu(u_ref[...])
    for c in range(4):
        ls = slice(c * LANES, (c + 1) * LANES)
        vsc[c] = vn[:, ls]
        usc[c] = u[:, ls]
        for q in range(t_blk):
            s = bias_ref[q:q + 1, ls]
            for k in range(q + 1):
                s = s + coef_ref[q * t_blk + k:q * t_blk + k + 1, ls] * vsc[c, pl.ds(k, nb, stride=t_blk), :]
            osc[c, pl.ds(q, nb, stride=t_blk), :] = usc[c, pl.ds(q, nb, stride=t_blk), :] * s
        o_ref[:, ls] = osc[c]


def _sgu_sample(z, coef, bias, lng, lnb, l, nb, t_blk):
    m = nb * t_blk
    fix2 = lambda i: (0, 0)
    lay3 = lambda i: (l, 0, 0)
    return pl.pallas_call(
        functools.partial(_sgu_sample_kernel, nb=nb, t_blk=t_blk),
        grid=(1,),
        in_specs=[
            pl.BlockSpec((m, MIX_B), lambda i: (0, 1)),
            pl.BlockSpec((m, MIX_B), lambda i: (0, 2)),
            pl.BlockSpec((None, t_blk * t_blk, MIX_B), lay3),
            pl.BlockSpec((None, t_blk, MIX_B), lay3),
            pl.BlockSpec((None, 1, MIX_B), lay3),
            pl.BlockSpec((None, 1, MIX_B), lay3),
        ],
        out_specs=[pl.BlockSpec((m, MIX_B), fix2), pl.BlockSpec((m, MIX_B), fix2)],
        out_shape=[jax.ShapeDtypeStruct((m, MIX_B), F32), jax.ShapeDtypeStruct((m, MIX_B), F32)],
        scratch_shapes=[pltpu.VMEM((4, m, LANES), F32)] * 3,
        compiler_params=pltpu.CompilerParams(
            dimension_semantics=("arbitrary",), vmem_limit_bytes=VMEM_LIMIT),
        name="sgu_sample",
    )(z, z, coef, bias, lng, lnb)


def _swiglu(x, wg_ref, wu_ref, wd_ref):
    hg = _dot(x, wg_ref[...])
    a = (hg * _sigmoid(hg) * _dot(x, wu_ref[...])).astype(BF16)
    return _dot(a, wd_ref[...])


def _out_proj(res_refs, ya_ref, yb_ref, ga_ref, gb_ref, wout_ref, rs):
    x = res_refs[0][rs, :]
    for r in res_refs[1:]:
        x = x + r[rs, :]
    na = _rms(ya_ref[rs, :], ga_ref[...]).astype(BF16)
    nb = _rms(yb_ref[rs, :], gb_ref[...]).astype(BF16)
    return x + _dot(na, wout_ref[0:MIX_A, :]) + _dot(nb, wout_ref[MIX_A:, :])


FFN_SUB = 512


def _ffn_kernel(*refs, n_res, n_e, tm):
    (ya_ref, yb_ref, ga_ref, gb_ref, wout_ref, gf_ref, wg_ref, wu_ref, wd_ref,
     o_ref, hn_ref) = refs[n_res:]
    e = pl.program_id(1)
    for s in range(tm // FFN_SUB):
        rs = slice(s * FFN_SUB, (s + 1) * FFN_SUB)

        @pl.when(e == 0)
        def _():
            x1 = _out_proj(refs[:n_res], ya_ref, yb_ref, ga_ref, gb_ref, wout_ref, rs)
            hn_ref[rs, :] = _rms(x1, gf_ref[...]).astype(BF16)
            o_ref[rs, :] = x1

        o_ref[rs, :] += _swiglu(hn_ref[rs, :], wg_ref, wu_ref, wd_ref)


def _ffn(xs, ya, yb, ga, gb, wout, gf, wg, wu, wd, *, l, tm):
    m = xs[0].shape[0]
    n_e = 2
    j = l // 2
    one = pl.Buffered(1)
    lay3 = lambda i, e: (l, 0, 0)
    wsel = lambda i, e: (j * n_e + e, 0, 0)
    return pl.pallas_call(
        functools.partial(_ffn_kernel, n_res=len(xs), n_e=n_e, tm=tm),
        grid=(m // tm, n_e),
        in_specs=[pl.BlockSpec((tm, D_MODEL), lambda i, e: (i, 0), pipeline_mode=one) for _ in xs] + [
            pl.BlockSpec((tm, MIX_A), lambda i, e: (i, 0), pipeline_mode=one),
            pl.BlockSpec((tm, MIX_B), lambda i, e: (i, 0), pipeline_mode=one),
            pl.BlockSpec((None, 1, MIX_A), lay3),
            pl.BlockSpec((None, 1, MIX_B), lay3),
            pl.BlockSpec((None, D_MODEL, D_MODEL), lay3),
            pl.BlockSpec((None, 1, D_MODEL), lay3),
            pl.BlockSpec((None, D_MODEL, FF_PAD), wsel),
            pl.BlockSpec((None, D_MODEL, FF_PAD), wsel),
            pl.BlockSpec((None, FF_PAD, D_MODEL), wsel),
        ],
        out_specs=pl.BlockSpec((tm, D_MODEL), lambda i, e: (i, 0), pipeline_mode=one),
        out_shape=jax.ShapeDtypeStruct((m, D_MODEL), F32),
        scratch_shapes=[pltpu.VMEM((tm, D_MODEL), BF16)],
        compiler_params=pltpu.CompilerParams(
            dimension_semantics=("parallel", "arbitrary"), vmem_limit_bytes=VMEM_LIMIT),
        name="ffn_dense",
    )(*xs, ya, yb, ga, gb, wout, gf, wg, wu, wd)


MOE_WIN = 256
SEG = 16
SORT_ROWS = 2 * MOE_WIN + N_EXPERTS * SEG
ROW_TILE = 256
NOT_ROUTED = -1.0e6


def _top2(logits):
    lane = lax.broadcasted_iota(jnp.int32, logits.shape, 1).astype(F32)
    neg = jnp.float32(-jnp.inf)
    lg = jnp.where(lane < N_EXPERTS, logits, neg)
    m1 = jnp.max(lg, axis=1, keepdims=True)
    i1 = jnp.min(jnp.where(lg == m1, lane, float(LANES)), axis=1, keepdims=True)
    lg2 = jnp.where(lane == i1, neg, lg)
    m2 = jnp.max(lg2, axis=1, keepdims=True)
    i2 = jnp.min(jnp.where(lg2 == m2, lane, float(LANES)), axis=1, keepdims=True)
    ex = jnp.exp(m2 - m1)
    w1 = 1.0 / (1.0 + ex)
    w2 = ex / (1.0 + ex)
    gates = jnp.where(lane == i1, w1, 0.0) + jnp.where(lane == i2, w2, 0.0)
    mask = jnp.where((lane == i1) | (lane == i2), 1.0, 0.0)
    return gates, mask


def _pre_moe_kernel(*refs, n_res, n_win):
    (ya_ref, yb_ref, ga_ref, gb_ref, wout_ref, gf_ref, rwh_ref, rwl_ref, rb_ref,
     x1_ref, hn_ref, gates_ref, pm_ref, pmt_ref, cnt_ref) = refs[n_res:]
    x1 = _out_proj(refs[:n_res], ya_ref, yb_ref, ga_ref, gb_ref, wout_ref, slice(None))
    x1_ref[...] = x1
    hn = _rms(x1, gf_ref[...])
    hn_ref[...] = hn.astype(BF16)
    hi, lo = _split_bf16(hn)
    logits = _dot(hi, rwh_ref[...]) + _dot(lo, rwh_ref[...]) + _dot(hi, rwl_ref[...])
    gates, mask = _top2(logits + rb_ref[...])
    gates_ref[...] = gates
    row = lax.broadcasted_iota(jnp.int32, (MOE_WIN, MOE_WIN), 0)
    col = lax.broadcasted_iota(jnp.int32, (MOE_WIN, MOE_WIN), 1)
    earlier = jnp.where(col < row, 1.0, 0.0).astype(BF16)
    for w in range(n_win):
        rs = slice(w * MOE_WIN, (w + 1) * MOE_WIN)
        mw = mask[rs, :]
        rank = _dot(earlier, mw.astype(BF16))
        pm = jnp.where(mw > 0.0, rank, NOT_ROUTED)
        pm_ref[rs, :] = pm
        pmt_ref[w] = pm.T[:SUBLANES, :]
        cnt_ref[w] = jnp.broadcast_to(jnp.sum(mw, axis=0, keepdims=True), (SUBLANES, LANES))


def _pre_moe(xs, ya, yb, ga, gb, wout, gf, rwh, rwl, rb, *, l, tm):
    m = xs[0].shape[0]
    n_win = tm // MOE_WIN
    j = l // 2
    lay3 = lambda i: (l, 0, 0)
    moe3 = lambda i: (j, 0, 0)
    return pl.pallas_call(
        functools.partial(_pre_moe_kernel, n_res=len(xs), n_win=n_win),
        grid=(m // tm,),
        in_specs=[pl.BlockSpec((tm, D_MODEL), lambda i: (i, 0)) for _ in xs] + [
            pl.BlockSpec((tm, MIX_A), lambda i: (i, 0)),
            pl.BlockSpec((tm, MIX_B), lambda i: (i, 0)),
            pl.BlockSpec((None, 1, MIX_A), lay3),
            pl.BlockSpec((None, 1, MIX_B), lay3),
            pl.BlockSpec((None, D_MODEL, D_MODEL), lay3),
            pl.BlockSpec((None, 1, D_MODEL), lay3),
            pl.BlockSpec((None, D_MODEL, LANES), moe3),
            pl.BlockSpec((None, D_MODEL, LANES), moe3),
            pl.BlockSpec((None, 1, LANES), moe3),
        ],
        out_specs=[
            pl.BlockSpec((tm, D_MODEL), lambda i: (i, 0)),
            pl.BlockSpec((tm, D_MODEL), lambda i: (i, 0)),
            pl.BlockSpec((tm, LANES), lambda i: (i, 0)),
            pl.BlockSpec((tm, LANES), lambda i: (i, 0)),
            pl.BlockSpec((n_win, SUBLANES, MOE_WIN), lambda i: (i, 0, 0)),
            pl.BlockSpec((n_win, SUBLANES, LANES), lambda i: (i, 0, 0)),
        ],
        out_shape=[
            jax.ShapeDtypeStruct((m, D_MODEL), F32),
            jax.ShapeDtypeStruct((m, D_MODEL), BF16),
            jax.ShapeDtypeStruct((m, LANES), F32),
            jax.ShapeDtypeStruct((m, LANES), F32),
            jax.ShapeDtypeStruct((m // MOE_WIN, SUBLANES, MOE_WIN), F32),
            jax.ShapeDtypeStruct((m // MOE_WIN, SUBLANES, LANES), F32),
        ],
        compiler_params=pltpu.CompilerParams(
            dimension_semantics=("parallel",), vmem_limit_bytes=VMEM_LIMIT),
        name="pre_moe",
    )(*xs, ya, yb, ga, gb, wout, gf, rwh, rwl, rb)


def _moe_kernel(cnt_sm, hn_ref, gates_ref, pm_ref, pmt_ref, wg_ref, wu_ref, wd_ref, o_ref,
                xs, gs, sb, gsb, pn_sm, s_sm, off_sm, est_sm, tot_sm, *, n_win):
    blk = pl.program_id(0)
    e = pl.program_id(1)

    def seg_copy(w, ee, to_sorted):
        s0 = s_sm[w * N_EXPERTS + ee]
        o0 = off_sm[w * N_EXPERTS + ee]

        def body(i, carry):
            src = pl.multiple_of(s0 + i * SEG, SEG)
            dst = pl.multiple_of(o0 + i * SEG, SEG)
            if to_sorted:
                xs[pl.ds(dst, SEG), :] = sb[pl.ds(src, SEG), :]
                gs[pl.ds(dst, SEG), :] = gsb[pl.ds(src, SEG), :]
            else:
                sb[pl.ds(src, SEG), :] = xs[pl.ds(dst, SEG), :]
            return carry

        lax.fori_loop(0, pn_sm[w * N_EXPERTS + ee] // SEG, body, 0)

    @pl.when(e == 0)
    def _dispatch():
        pn = [[None] * N_EXPERTS for _ in range(n_win)]
        for w in range(n_win):
            run = jnp.int32(0)
            for ee in range(N_EXPERTS):
                n = cnt_sm[(blk * n_win + w) * N_EXPERTS + ee]
                pn[w][ee] = jnp.bitwise_and(n + (SEG - 1), -SEG)
                pn_sm[w * N_EXPERTS + ee] = pn[w][ee]
                s_sm[w * N_EXPERTS + ee] = run
                run = run + pn[w][ee]
        run = jnp.int32(0)
        for ee in range(N_EXPERTS):
            est_sm[ee] = run
            start = run
            for w in range(n_win):
                off_sm[w * N_EXPERTS + ee] = run
                run = run + pn[w][ee]
            tot_sm[ee] = run - start
        xs[...] = jnp.zeros_like(xs)
        gs[...] = jnp.zeros_like(gs)
        riota = lax.broadcasted_iota(jnp.int32, (SORT_ROWS, MOE_WIN), 0).astype(F32)
        for w in range(n_win):
            rs = slice(w * MOE_WIN, (w + 1) * MOE_WIN)
            g = jnp.zeros((SORT_ROWS, MOE_WIN), F32)
            for ee in range(N_EXPERTS):
                dest = pmt_ref[w, ee:ee + 1, :] + s_sm[w * N_EXPERTS + ee].astype(F32)
                g = jnp.where(riota == dest, 1.0, g)
            gb = g.astype(BF16)
            sb[...] = _dot(gb, hn_ref[rs, :]).astype(BF16)
            gh, gl = _split_bf16(gates_ref[rs, :])
            gsb[...] = _dot(gb, gh) + _dot(gb, gl)
            for ee in range(N_EXPERTS):
                seg_copy(w, ee, True)

    start = est_sm[e]
    tot = tot_sm[e]

    def row_tile(r0, size, valid):
        r0 = pl.multiple_of(r0, SEG)
        xt = xs[pl.ds(r0, size), :]
        y = _swiglu(xt, wg_ref, wu_ref, wd_ref)
        lane = lax.broadcasted_iota(jnp.int32, (size, LANES), 1)
        gate = jnp.sum(jnp.where(lane == e, gs[pl.ds(r0, size), :], 0.0), axis=1, keepdims=True)
        keep = lax.broadcasted_iota(jnp.int32, (size, D_MODEL), 0) < valid
        xs[pl.ds(r0, size), :] = jnp.where(keep, (y * gate).astype(BF16), xt)

    n_full = tot // ROW_TILE

    def full_body(i, carry):
        row_tile(start + i * ROW_TILE, ROW_TILE, ROW_TILE)
        return carry

    lax.fori_loop(0, n_full, full_body, 0)
    rem = tot - n_full * ROW_TILE
    tail = start + n_full * ROW_TILE

    @pl.when(rem > ROW_TILE // 2)
    def _():
        row_tile(tail, ROW_TILE, rem)

    @pl.when((rem > 0) & (rem <= ROW_TILE // 2))
    def _():
        row_tile(tail, ROW_TILE // 2, rem)

    @pl.when(e == N_EXPERTS - 1)
    def _combine():
        liota = lax.broadcasted_iota(jnp.int32, (MOE_WIN, SORT_ROWS), 1).astype(F32)
        for w in range(n_win):
            rs = slice(w * MOE_WIN, (w + 1) * MOE_WIN)
            for ee in range(N_EXPERTS):
                seg_copy(w, ee, False)
            g = jnp.zeros((MOE_WIN, SORT_ROWS), F32)
            for ee in range(N_EXPERTS):
                dest = pm_ref[rs, ee:ee + 1] + s_sm[w * N_EXPERTS + ee].astype(F32)
                g = jnp.where(liota == dest, 1.0, g)
            o_ref[rs, :] = _dot(g.astype(BF16), sb[...])


def _moe(cnt, hn, gates, pm, pmt, wg, wu, wd, *, l, tb):
    m = hn.shape[0]
    n_win = tb // MOE_WIN
    j = l // 2
    xs_rows = 2 * tb + n_win * N_EXPERTS * SEG + ROW_TILE
    one = pl.Buffered(1)
    wsel = lambda i, e, c: (j * N_EXPERTS + e, 0, 0)
    grid_spec = pltpu.PrefetchScalarGridSpec(
        num_scalar_prefetch=1,
        grid=(m // tb, N_EXPERTS),
        in_specs=[
            pl.BlockSpec((tb, D_MODEL), lambda i, e, c: (i, 0), pipeline_mode=one),
            pl.BlockSpec((tb, LANES), lambda i, e, c: (i, 0), pipeline_mode=one),
            pl.BlockSpec((tb, LANES), lambda i, e, c: (i, 0), pipeline_mode=one),
            pl.BlockSpec((n_win, SUBLANES, MOE_WIN), lambda i, e, c: (i, 0, 0), pipeline_mode=one),
            pl.BlockSpec((None, D_MODEL, FF_PAD), wsel),
            pl.BlockSpec((None, D_MODEL, FF_PAD), wsel),
            pl.BlockSpec((None, FF_PAD, D_MODEL), wsel),
        ],
        out_specs=pl.BlockSpec((tb, D_MODEL), lambda i, e, c: (i, 0), pipeline_mode=one),
        scratch_shapes=[
            pltpu.VMEM((xs_rows, D_MODEL), BF16),
            pltpu.VMEM((xs_rows, LANES), F32),
            pltpu.VMEM((SORT_ROWS, D_MODEL), BF16),
            pltpu.VMEM((SORT_ROWS, LANES), F32),
            pltpu.SMEM((n_win * N_EXPERTS,), jnp.int32),
            pltpu.SMEM((n_win * N_EXPERTS,), jnp.int32),
            pltpu.SMEM((n_win * N_EXPERTS,), jnp.int32),
            pltpu.SMEM((N_EXPERTS,), jnp.int32),
            pltpu.SMEM((N_EXPERTS,), jnp.int32),
        ],
    )
    return pl.pallas_call(
        functools.partial(_moe_kernel, n_win=n_win),
        grid_spec=grid_spec,
        out_shape=jax.ShapeDtypeStruct((m, D_MODEL), F32),
        compiler_params=pltpu.CompilerParams(
            dimension_semantics=("arbitrary", "arbitrary"), vmem_limit_bytes=MOE_VMEM_LIMIT),
        name="moe",
    )(cnt, hn, gates, pm, pmt, wg, wu, wd)


def _final_norm_kernel(*refs, n_res):
    g_ref, o_ref = refs[n_res:]
    x = refs[0][...]
    for r in refs[1:n_res]:
        x = x + r[...]
    o_ref[...] = _rms(x, g_ref[...])


def _final_norm(xs, g, tm):
    m = xs[0].shape[0]
    return pl.pallas_call(
        functools.partial(_final_norm_kernel, n_res=len(xs)),
        grid=(m // tm,),
        in_specs=[pl.BlockSpec((tm, D_MODEL), lambda i: (i, 0)) for _ in xs]
        + [pl.BlockSpec((1, D_MODEL), lambda i: (0, 0))],
        out_specs=pl.BlockSpec((tm, D_MODEL), lambda i: (i, 0)),
        out_shape=jax.ShapeDtypeStruct((m, D_MODEL), F32),
        compiler_params=pltpu.CompilerParams(
            dimension_semantics=("parallel",), vmem_limit_bytes=VMEM_LIMIT),
        name="final_norm",
    )(*xs, g)


def kernel(x_prompt, x_sample, state_ssm_re, state_ssm_im, norm_mix, w_in, ssm_a_re, ssm_a_im, ssm_log_dt, ssm_b_re, ssm_b_im, ssm_c_re, ssm_c_im, ssm_d, glu_w, glu_b, sgu_w, sgu_b, sgu_ln_g, sgu_ln_b, out_norm_a, out_norm_b, w_out, norm_ffn, ffn_w_gate, ffn_w_up, ffn_w_down, router_w, router_b, moe_w_gate, moe_w_up, moe_w_down, norm_final):
    batch, seq = x_prompt.shape[0], x_prompt.shape[1]
    dec_batch, dec_seq = x_sample.shape[0], x_sample.shape[1]
    mp, ms = batch * seq, dec_batch * dec_seq
    n_moe = moe_w_gate.shape[0]

    lam, bbre, bbim = _s5_prep(ssm_a_re, ssm_a_im, ssm_log_dt, ssm_b_re, ssm_b_im)
    ctre = jnp.swapaxes(ssm_c_re, 2, 3).reshape(DEPTH * N_STATE, SSM_GROUP)
    ctim = jnp.swapaxes(ssm_c_im, 2, 3).reshape(DEPTH * N_STATE, SSM_GROUP)
    dskip = ssm_d.reshape(DEPTH, 1, MIX_A)
    glu = glu_w.reshape(DEPTH * MIX_A, 2 * SSM_GROUP)
    gbias = jnp.stack([glu_b[..., :SSM_GROUP].reshape(DEPTH, MIX_A),
                       glu_b[..., SSM_GROUP:].reshape(DEPTH, MIX_A)], axis=1)
    s5w = (lam, bbre, bbim, ctre, ctim, dskip, glu, gbias)
    w_in_b = w_in.astype(BF16)
    w_out_b = w_out.astype(BF16)
    g_mix = norm_mix.reshape(DEPTH, 1, D_MODEL)
    g_a = out_norm_a.reshape(DEPTH, 1, MIX_A)
    g_b = out_norm_b.reshape(DEPTH, 1, MIX_B)
    g_ffn = norm_ffn.reshape(DEPTH, 1, D_MODEL)
    sgu_bias = jnp.repeat(jnp.swapaxes(sgu_b, 1, 2), HD_B, axis=2)
    sgu_bias_s = sgu_bias[:, :dec_seq]
    lng = sgu_ln_g.reshape(DEPTH, 1, MIX_B)
    lnb = sgu_ln_b.reshape(DEPTH, 1, MIX_B)
    coef_s = jnp.repeat(
        jnp.transpose(sgu_w[:, :, :dec_seq, :dec_seq], (0, 2, 3, 1)).reshape(DEPTH, dec_seq * dec_seq, NH_B),
        HD_B, axis=2)
    dense_split = (FF_PAD, D_FF - FF_PAD)
    dense_g = _pack(ffn_w_gate, dense_split, cols=True)
    dense_u = _pack(ffn_w_up, dense_split, cols=True)
    dense_d = _pack(ffn_w_down, dense_split, cols=False)
    moe_g = _pack(moe_w_gate.reshape(-1, D_MODEL, D_FF_EXPERT), (D_FF_EXPERT,), cols=True)
    moe_u = _pack(moe_w_up.reshape(-1, D_MODEL, D_FF_EXPERT), (D_FF_EXPERT,), cols=True)
    moe_d = _pack(moe_w_down.reshape(-1, D_FF_EXPERT, D_MODEL), (D_FF_EXPERT,), cols=False)
    rw = jnp.pad(router_w, ((0, 0), (0, 0), (0, LANES - N_EXPERTS)))
    rwh = rw.astype(BF16)
    rwl = (rw - rwh.astype(F32)).astype(BF16)
    rb = jnp.pad(router_b, ((0, 0), (0, LANES - N_EXPERTS))).reshape(n_moe, 1, LANES)
    gfin = norm_final.reshape(1, D_MODEL)

    h0re = state_ssm_re.reshape(DEPTH, dec_batch, N_STATE)
    h0im = state_ssm_im.reshape(DEPTH, dec_batch, N_STATE)

    def post_mixer(xres, ya, yb, l, tm, tb):
        common = (g_a, g_b, w_out_b, g_ffn)
        if l % 2 == 0:
            return (_ffn(xres, ya, yb, *common, dense_g, dense_u, dense_d, l=l, tm=tm),)
        x1, hn, gates, pm, pmt, cnt = _pre_moe(xres, ya, yb, *common, rwh, rwl, rb, l=l, tm=min(tm, 512))
        cnt = cnt[:, 0, :N_EXPERTS].astype(jnp.int32).reshape(-1)
        return (x1, _moe(cnt, hn, gates, pm, pmt, moe_g, moe_u, moe_d, l=l, tb=tb))

    xp = (x_prompt.reshape(mp, D_MODEL),)
    xs = (x_sample.reshape(ms, D_MODEL),)
    re_p, im_p, re_s, im_s, v_s = [], [], [], [], []
    for l in range(DEPTH):
        zp = _inproj(xp, g_mix, w_in_b, l, tm=1024)
        zs = _inproj(xs, g_mix, w_in_b, l, tm=ms)
        ya_p, hre_p, him_p = _s5_prompt(zp.reshape(batch, seq, IN_WIDTH), s5w, l, t_blk=64)
        ya_s, hre_s, him_s = _s5_sample(zs, h0re, h0im, s5w, l, nb=dec_batch, t_blk=dec_seq)
        yb_p = _sgu_prompt(zp, sgu_w, sgu_bias, lng, lnb, l, n_chunks=4)
        yb_s, vn_s = _sgu_sample(zs, coef_s, sgu_bias_s, lng, lnb, l, nb=dec_batch, t_blk=dec_seq)
        xp = post_mixer(xp, ya_p.reshape(mp, MIX_A), yb_p, l, tm=1024, tb=seq)
        xs = post_mixer(xs, ya_s, yb_s, l, tm=ms, tb=ms)
        re_p.append(hre_p)
        im_p.append(him_p)
        re_s.append(hre_s)
        im_s.append(him_s)
        v_s.append(vn_s)
    xp = _final_norm(xp, gfin, tm=1024)
    xs = _final_norm(xs, gfin, tm=ms)

    st = lambda hs, b: jnp.stack(hs).reshape(DEPTH, b, NG_A, SSM_STATE)
    return (xp.reshape(batch, seq, D_MODEL), xs.reshape(dec_batch, dec_seq, D_MODEL),
            st(re_p, batch), st(im_p, batch), st(re_s, dec_batch), st(im_s, dec_batch),
            jnp.stack(v_s).reshape(DEPTH, dec_batch, dec_seq, NH_B, HD_B))
```

```python
import functools
import math

import jax
import jax.numpy as jnp
from jax import lax
from jax.experimental import pallas as pl
from jax.experimental.pallas import tpu as pltpu

F32 = jnp.float32
BF16 = jnp.bfloat16

D_MODEL = 1024
DEPTH = 4
MIX_A = 512
SSM_GROUP = 16
NG_A = 32
SSM_STATE = 64
N_STATE = NG_A * SSM_STATE
MIX_B = 512
HD_B = 64
NH_B = 8
CHUNK = 128
IN_WIDTH = MIX_A + 2 * MIX_B
D_FF = 2752
N_EXPERTS = 8
D_FF_EXPERT = D_FF // 2
FF_PAD = 1408
EPS = 1e-6

LANES = 128
SUBLANES = 8
MXU_DIM = 256
STATE_COLS = 512
N_SCOL = N_STATE // STATE_COLS
GROUPS_PER_CHUNK = STATE_COLS // SSM_STATE
VMEM_LIMIT = 52 * 1024 * 1024
MOE_VMEM_LIMIT = 58 * 1024 * 1024


def _rms(x, g):
    return x * lax.rsqrt(jnp.mean(x * x, axis=-1, keepdims=True) + EPS) * g


def _gelu(x):
    c = math.sqrt(2.0 / math.pi)
    return 0.5 * x * (1.0 + jnp.tanh(c * (x + 0.044715 * (x * x * x))))


def _sigmoid(x):
    return 1.0 / (1.0 + jnp.exp(-x))


def _split_bf16(x):
    hi = x.astype(BF16)
    lo = (x - hi.astype(F32)).astype(BF16)
    return hi, lo


def _dot(a, b):
    return jnp.dot(a, b, preferred_element_type=F32)


def _log2(n):
    assert n & (n - 1) == 0
    return n.bit_length() - 1


def _block_diag_tile(src, n_rep, rows_per_group):
    r_n, k = src.shape
    n = k * n_rep
    t_row = lax.broadcasted_iota(jnp.int32, (k, n), 0)
    t_col = lax.broadcasted_iota(jnp.int32, (k, n), 1)
    tile = jnp.where(jnp.bitwise_and(t_col, k - 1) == t_row, 1.0, 0.0).astype(BF16)
    full = _dot(src.astype(BF16), tile)
    row = lax.broadcasted_iota(jnp.int32, (r_n, n), 0)
    col = lax.broadcasted_iota(jnp.int32, (r_n, n), 1)
    same = lax.shift_right_logical(row, _log2(rows_per_group)) == lax.shift_right_logical(col, _log2(k))
    return jnp.where(same, full, 0.0).astype(BF16)


def _pack_kernel(w_ref, o_ref, *, widths, transpose):
    for i, width in enumerate(widths):
        w = w_ref[i * FF_PAD:i * FF_PAD + width, :]
        if width < FF_PAD:
            w = jnp.concatenate([w, jnp.zeros((FF_PAD - width, w.shape[1]), F32)], axis=0)
        o_ref[i] = (w.T if transpose else w).astype(BF16)


def _pack(w, widths, transpose):
    n_e, n_parts = w.shape[0], len(widths)
    out_blk = (n_parts, D_MODEL, FF_PAD) if transpose else (n_parts, FF_PAD, D_MODEL)
    out = pl.pallas_call(
        functools.partial(_pack_kernel, widths=widths, transpose=transpose),
        grid=(n_e,),
        in_specs=[pl.BlockSpec((None,) + w.shape[1:], lambda e: (e, 0, 0))],
        out_specs=pl.BlockSpec((None,) + out_blk, lambda e: (e, 0, 0, 0)),
        out_shape=jax.ShapeDtypeStruct((n_e,) + out_blk, BF16),
        compiler_params=pltpu.CompilerParams(
            dimension_semantics=("parallel",), vmem_limit_bytes=VMEM_LIMIT),
        name="pack_t" if transpose else "pack",
    )(w)
    return out.reshape((n_e * n_parts,) + out_blk[1:])


def _s5_prep_kernel(are_ref, aim_ref, ldt_ref, bre_ref, bim_ref,
                    abre_ref, abim_ref, bbre_ref, bbim_ref):
    lam_re = are_ref[...]
    lam_im = aim_ref[...]
    dt = jnp.exp(ldt_ref[...])
    mag = jnp.exp(lam_re * dt)
    ab_re = mag * jnp.cos(lam_im * dt)
    ab_im = mag * jnp.sin(lam_im * dt)
    den = lam_re * lam_re + lam_im * lam_im
    nr = ab_re - 1.0
    q_re = (nr * lam_re + ab_im * lam_im) / den
    q_im = (ab_im * lam_re - nr * lam_im) / den
    b_re = bre_ref[...]
    b_im = bim_ref[...]
    abre_ref[...] = ab_re
    abim_ref[...] = ab_im
    bbre_ref[...] = q_re * b_re - q_im * b_im
    bbim_ref[...] = q_re * b_im + q_im * b_re


def _s5_prep(ssm_a_re, ssm_a_im, ssm_log_dt, ssm_b_re, ssm_b_im):
    shp = (DEPTH, NG_A, SSM_GROUP, SSM_STATE)
    rows = DEPTH * NG_A * SSM_GROUP
    bc = lambda a: jnp.broadcast_to(a, shp).reshape(rows, SSM_STATE)
    are = bc(ssm_a_re[:, :, None, :])
    aim = bc(ssm_a_im[:, :, None, :])
    ldt = bc(ssm_log_dt[:, :, None, None])
    bre = jnp.swapaxes(ssm_b_re, 2, 3).reshape(rows, SSM_STATE)
    bim = jnp.swapaxes(ssm_b_im, 2, 3).reshape(rows, SSM_STATE)
    sds = jax.ShapeDtypeStruct((rows, SSM_STATE), F32)
    abre, abim, bbre, bbim = pl.pallas_call(
        _s5_prep_kernel, out_shape=(sds, sds, sds, sds), name="s5_prep",
    )(are, aim, ldt, bre, bim)
    lam_re = abre.reshape(shp)[:, :, 0, :].reshape(DEPTH, 1, N_STATE)
    lam_im = abim.reshape(shp)[:, :, 0, :].reshape(DEPTH, 1, N_STATE)
    lam = jnp.concatenate([lam_re, lam_im], axis=1)
    return lam, bbre, bbim


def _inproj_kernel(*refs, n_res):
    g_ref, w_ref, z_ref = refs[n_res:]
    x = refs[0][...]
    for r in refs[1:n_res]:
        x = x + r[...]
    hn = _rms(x, g_ref[...])
    z_ref[...] = _dot(hn.astype(BF16), w_ref[...])


def _inproj(xs, g, w, l, tm):
    m = xs[0].shape[0]
    return pl.pallas_call(
        functools.partial(_inproj_kernel, n_res=len(xs)),
        grid=(m // tm,),
        in_specs=[pl.BlockSpec((tm, D_MODEL), lambda i: (i, 0)) for _ in xs] + [
            pl.BlockSpec((None, 1, D_MODEL), lambda i: (l, 0, 0)),
            pl.BlockSpec((None, D_MODEL, IN_WIDTH), lambda i: (l, 0, 0)),
        ],
        out_specs=pl.BlockSpec((tm, IN_WIDTH), lambda i: (i, 0)),
        out_shape=jax.ShapeDtypeStruct((m, IN_WIDTH), F32),
        compiler_params=pltpu.CompilerParams(
            dimension_semantics=("parallel",), vmem_limit_bytes=VMEM_LIMIT),
        name="inproj",
    )(*xs, g, w)


def _s5_kernel(*refs, nb, t_blk, prompt):
    n_in = 1 if prompt else 3
    (bbre_ref, bbim_ref, lam_ref, ctre_ref, ctim_ref, d_ref, glu_ref, gbias_ref,
     ya_ref, hre_ref, him_ref, utm, bre, bim, otm, bblk, cre, cimn, wa, wg) = refs[n_in:n_in + 20]
    u_ref = refs[0]

    @pl.when(pl.program_id(0) == 0)
    def _():
        if prompt:
            hre_ref[...] = jnp.zeros_like(hre_ref)
            him_ref[...] = jnp.zeros_like(him_ref)
        else:
            hre_ref[...] = refs[1][...]
            him_ref[...] = refs[2][...]
        for c in range(4):
            rs = slice(c * LANES, (c + 1) * LANES)
            bblk[c, :, :STATE_COLS] = _block_diag_tile(bbre_ref[rs, :], GROUPS_PER_CHUNK, SSM_GROUP)
            bblk[c, :, STATE_COLS:] = _block_diag_tile(bbim_ref[rs, :], GROUPS_PER_CHUNK, SSM_GROUP)
            ss = slice(c * STATE_COLS, (c + 1) * STATE_COLS)
            cre[c] = _block_diag_tile(ctre_ref[ss, :], GROUPS_PER_CHUNK, SSM_STATE)
            cimn[c] = _block_diag_tile(-ctim_ref[ss, :], GROUPS_PER_CHUNK, SSM_STATE)
        for b in range(MIX_A // MXU_DIM):
            ms = slice(b * MXU_DIM, (b + 1) * MXU_DIM)
            wa[b] = _block_diag_tile(glu_ref[ms, :SSM_GROUP], MXU_DIM // SSM_GROUP, SSM_GROUP)
            wg[b] = _block_diag_tile(glu_ref[ms, SSM_GROUP:], MXU_DIM // SSM_GROUP, SSM_GROUP)

    for c in range(4):
        ls = slice(c * LANES, (c + 1) * LANES)
        if prompt:
            for b in range(nb):
                utm[c, pl.ds(b, t_blk, stride=nb), :] = u_ref[b, :, ls]
        else:
            tmp = refs[n_in + 20]
            tmp[c] = u_ref[:, ls]
            for t in range(t_blk):
                utm[c, t * nb:(t + 1) * nb, :] = tmp[c, pl.ds(t, nb, stride=t_blk), :]

    for c in range(4):
        r = _dot(utm[c].astype(BF16), bblk[c])
        bre[:, c * STATE_COLS:(c + 1) * STATE_COLS] = r[:, :STATE_COLS]
        bim[:, c * STATE_COLS:(c + 1) * STATE_COLS] = r[:, STATE_COLS:]

    for c in range(N_SCOL):
        cs = slice(c * STATE_COLS, (c + 1) * STATE_COLS)
        lr = jnp.broadcast_to(lam_ref[0:1, cs], (SUBLANES, STATE_COLS))
        li = jnp.broadcast_to(lam_ref[1:2, cs], (SUBLANES, STATE_COLS))

        def group_body(bg, carry, cs=cs, lr=lr, li=li):
            r0 = pl.multiple_of(bg * SUBLANES, SUBLANES)

            def t_body(t, h):
                hr, hi = h
                row = pl.multiple_of(t * nb + r0, SUBLANES)
                nr = lr * hr - li * hi + bre[pl.ds(row, SUBLANES), cs]
                ni = lr * hi + li * hr + bim[pl.ds(row, SUBLANES), cs]
                bre[pl.ds(row, SUBLANES), cs] = nr
                bim[pl.ds(row, SUBLANES), cs] = ni
                return nr, ni

            h0 = (hre_ref[pl.ds(r0, SUBLANES), cs], him_ref[pl.ds(r0, SUBLANES), cs])
            hr, hi = lax.fori_loop(0, t_blk, t_body, h0, unroll=min(t_blk, 8))
            hre_ref[pl.ds(r0, SUBLANES), cs] = hr
            him_ref[pl.ds(r0, SUBLANES), cs] = hi
            return carry

        lax.fori_loop(0, nb // SUBLANES, group_body, 0)

    ys = []
    for c in range(4):
        cs = slice(c * STATE_COLS, (c + 1) * STATE_COLS)
        ls = slice(c * LANES, (c + 1) * LANES)
        y = _dot(bre[:, cs].astype(BF16), cre[c]) + _dot(bim[:, cs].astype(BF16), cimn[c])
        ys.append(_gelu(y + d_ref[:, ls] * utm[c]).astype(BF16))

    os_ = []
    for b in range(MIX_A // MXU_DIM):
        ms = slice(b * MXU_DIM, (b + 1) * MXU_DIM)
        yb = jnp.concatenate(ys[2 * b:2 * b + 2], axis=1)
        za = _dot(yb, wa[b]) + gbias_ref[0:1, ms]
        zg = _dot(yb, wg[b]) + gbias_ref[1:2, ms]
        os_.append(za * _sigmoid(zg))
    o = jnp.concatenate(os_, axis=1)

    for c in range(4):
        ls = slice(c * LANES, (c + 1) * LANES)
        if prompt:
            otm[c] = o[:, ls]
            for b in range(nb):
                ya_ref[b, :, ls] = otm[c, pl.ds(b, t_blk, stride=nb), :]
        else:
            for t in range(t_blk):
                otm[c, pl.ds(t, nb, stride=t_blk), :] = o[t * nb:(t + 1) * nb, ls]
            ya_ref[:, ls] = otm[c]


def _s5_call(u_specs, u_args, out_block, out_shape, weights, l, nb, t_blk, prompt, n_steps):
    rows = nb * t_blk
    lam, bbre, bbim, ctre, ctim, dskip, glu, gbias = weights
    w_specs = [
        pl.BlockSpec((NG_A * SSM_GROUP, SSM_STATE), lambda j: (l, 0)),
        pl.BlockSpec((NG_A * SSM_GROUP, SSM_STATE), lambda j: (l, 0)),
        pl.BlockSpec((None, 2, N_STATE), lambda j: (l, 0, 0)),
        pl.BlockSpec((N_STATE, SSM_GROUP), lambda j: (l, 0)),
        pl.BlockSpec((N_STATE, SSM_GROUP), lambda j: (l, 0)),
        pl.BlockSpec((None, 1, MIX_A), lambda j: (l, 0, 0)),
        pl.BlockSpec((MIX_A, 2 * SSM_GROUP), lambda j: (l, 0)),
        pl.BlockSpec((None, 2, MIX_A), lambda j: (l, 0, 0)),
    ]
    scratch = [
        pltpu.VMEM((4, rows, LANES), F32),
        pltpu.VMEM((rows, N_STATE), F32),
        pltpu.VMEM((rows, N_STATE), F32),
        pltpu.VMEM((4, rows, LANES), F32),
        pltpu.VMEM((4, LANES, 2 * STATE_COLS), BF16),
        pltpu.VMEM((4, STATE_COLS, LANES), BF16),
        pltpu.VMEM((4, STATE_COLS, LANES), BF16),
        pltpu.VMEM((MIX_A // MXU_DIM, MXU_DIM, MXU_DIM), BF16),
        pltpu.VMEM((MIX_A // MXU_DIM, MXU_DIM, MXU_DIM), BF16),
    ]
    if not prompt:
        scratch.append(pltpu.VMEM((4, rows, LANES), F32))
    return pl.pallas_call(
        functools.partial(_s5_kernel, nb=nb, t_blk=t_blk, prompt=prompt),
        grid=(n_steps,),
        in_specs=u_specs + w_specs,
        out_specs=[
            out_block,
            pl.BlockSpec((nb, N_STATE), lambda j: (0, 0)),
            pl.BlockSpec((nb, N_STATE), lambda j: (0, 0)),
        ],
        out_shape=[
            out_shape,
            jax.ShapeDtypeStruct((nb, N_STATE), F32),
            jax.ShapeDtypeStruct((nb, N_STATE), F32),
        ],
        scratch_shapes=scratch,
        compiler_params=pltpu.CompilerParams(
            dimension_semantics=("arbitrary",), vmem_limit_bytes=VMEM_LIMIT),
        name="s5_prompt" if prompt else "s5_sample",
    )(*u_args, bbre, bbim, lam, ctre, ctim, dskip, glu, gbias)


def _s5_prompt(z3, weights, l, t_blk):
    nb, seq = z3.shape[0], z3.shape[1]
    blk = pl.BlockSpec((nb, t_blk, MIX_A), lambda j: (0, j, 0))
    return _s5_call([blk], [z3], blk, jax.ShapeDtypeStruct((nb, seq, MIX_A), F32),
                    weights, l, nb, t_blk, True, seq // t_blk)


def _s5_sample(z, h0re, h0im, weights, l, nb, t_blk):
    rows = nb * t_blk
    blk = pl.BlockSpec((rows, MIX_A), lambda j: (0, 0))
    st = pl.BlockSpec((None, nb, N_STATE), lambda j: (l, 0, 0))
    return _s5_call([blk, st, st], [z, h0re, h0im], blk, jax.ShapeDtypeStruct((rows, MIX_A), F32),
                    weights, l, nb, t_blk, False, 1)


def _group_layernorm(gv, lng_ref, lnb_ref):
    row = lax.broadcasted_iota(jnp.int32, (MXU_DIM, MXU_DIM), 0)
    col = lax.broadcasted_iota(jnp.int32, (MXU_DIM, MXU_DIM), 1)
    same = lax.shift_right_logical(row, _log2(HD_B)) == lax.shift_right_logical(col, _log2(HD_B))
    gm = jnp.where(same, 1.0 / HD_B, 0.0).astype(BF16)

    def gmean(a):
        hi, lo = _split_bf16(a)
        parts = []
        for b in range(MIX_B // MXU_DIM):
            ms = slice(b * MXU_DIM, (b + 1) * MXU_DIM)
            parts.append(_dot(hi[:, ms], gm) + _dot(lo[:, ms], gm))
        return jnp.concatenate(parts, axis=1)

    xc = gv - gmean(gv)
    var = gmean(xc * xc)
    return xc * lax.rsqrt(var + EPS) * lng_ref[...] + lnb_ref[...]


def _sgu_prompt_kernel(u_ref, v_ref, ws_ref, bias_ref, lng_ref, lnb_ref, o_ref, *, n_chunks):
    row = lax.broadcasted_iota(jnp.int32, (CHUNK, 2 * CHUNK), 0)
    col = lax.broadcasted_iota(jnp.int32, (CHUNK, 2 * CHUNK), 1)
    causal = jnp.bitwise_and(col, CHUNK - 1) <= row
    lane = lax.broadcasted_iota(jnp.int32, (CHUNK, LANES), 1)
    first_head = lane < HD_B
    wcat = []
    for p in range(NH_B // 2):
        w = jnp.concatenate([ws_ref[2 * p], ws_ref[2 * p + 1]], axis=1)
        wcat.append(jnp.where(causal, w, 0.0).astype(BF16))
    for n in range(n_chunks):
        rs = slice(n * CHUNK, (n + 1) * CHUNK)
        u = _gelu(u_ref[rs, :])
        vn = _group_layernorm(_gelu(v_ref[rs, :]), lng_ref, lnb_ref)
        for p in range(NH_B // 2):
            ls = slice(p * LANES, (p + 1) * LANES)
            vp = vn[:, ls]
            rhs = jnp.concatenate(
                [jnp.where(first_head, vp, 0.0), jnp.where(first_head, 0.0, vp)], axis=0).astype(BF16)
            s = _dot(wcat[p], rhs)
            o_ref[rs, ls] = u[:, ls] * (s + bias_ref[:, ls])


def _sgu_prompt(z, ws, bias, lng, lnb, l, n_chunks):
    m = z.shape[0]
    tm = n_chunks * CHUNK
    lay3 = lambda i: (l, 0, 0)
    return pl.pallas_call(
        functools.partial(_sgu_prompt_kernel, n_chunks=n_chunks),
        grid=(m // tm,),
        in_specs=[
            pl.BlockSpec((tm, MIX_B), lambda i: (i, 1)),
            pl.BlockSpec((tm, MIX_B), lambda i: (i, 2)),
            pl.BlockSpec((None, NH_B, CHUNK, CHUNK), lambda i: (l, 0, 0, 0)),
            pl.BlockSpec((None, CHUNK, MIX_B), lay3),
            pl.BlockSpec((None, 1, MIX_B), lay3),
            pl.BlockSpec((None, 1, MIX_B), lay3),
        ],
        out_specs=pl.BlockSpec((tm, MIX_B), lambda i: (i, 0)),
        out_shape=jax.ShapeDtypeStruct((m, MIX_B), F32),
        compiler_params=pltpu.CompilerParams(
            dimension_semantics=("parallel",), vmem_limit_bytes=VMEM_LIMIT),
        name="sgu_prompt",
    )(z, z, ws, bias, lng, lnb)


def _sgu_sample_kernel(u_ref, v_ref, coef_ref, bias_ref, lng_ref, lnb_ref, o_ref, vn_ref,
                       usc, vsc, osc, *, nb, t_blk):
    vn = _group_layernorm(_gelu(v_ref[...]), lng_ref, lnb_ref)
    vn_ref[...] = vn
    u = _gelu(u_ref[...])
    for c in range(4):
        ls = slice(c * LANES, (c + 1) * LANES)
        vsc[c] = vn[:, ls]
        usc[c] = u[:, ls]
        for q in range(t_blk):
            s = bias_ref[q:q + 1, ls]
            for k in range(q + 1):
                s = s + coef_ref[q * t_blk + k:q * t_blk + k + 1, ls] * vsc[c, pl.ds(k, nb, stride=t_blk), :]
            osc[c, pl.ds(q, nb, stride=t_blk), :] = usc[c, pl.ds(q, nb, stride=t_blk), :] * s
        o_ref[:, ls] = osc[c]


def _sgu_sample(z, coef, bias, lng, lnb, l, nb, t_blk):
    m = nb * t_blk
    fix2 = lambda i: (0, 0)
    lay3 = lambda i: (l, 0, 0)
    return pl.pallas_call(
        functools.partial(_sgu_sample_kernel, nb=nb, t_blk=t_blk),
        grid=(1,),
        in_specs=[
            pl.BlockSpec((m, MIX_B), lambda i: (0, 1)),
            pl.BlockSpec((m, MIX_B), lambda i: (0, 2)),
            pl.BlockSpec((None, t_blk * t_blk, MIX_B), lay3),
            pl.BlockSpec((None, t_blk, MIX_B), lay3),
            pl.BlockSpec((None, 1, MIX_B), lay3),
            pl.BlockSpec((None, 1, MIX_B), lay3),
        ],
        out_specs=[pl.BlockSpec((m, MIX_B), fix2), pl.BlockSpec((m, MIX_B), fix2)],
        out_shape=[jax.ShapeDtypeStruct((m, MIX_B), F32), jax.ShapeDtypeStruct((m, MIX_B), F32)],
        scratch_shapes=[pltpu.VMEM((4, m, LANES), F32)] * 3,
        compiler_params=pltpu.CompilerParams(
            dimension_semantics=("arbitrary",), vmem_limit_bytes=VMEM_LIMIT),
        name="sgu_sample",
    )(z, z, coef, bias, lng, lnb)


def _swiglu(x, wg_ref, wu_ref, wd_ref):
    hg = _dot(x, wg_ref[...])
    a = (hg * _sigmoid(hg) * _dot(x, wu_ref[...])).astype(BF16)
    return _dot(a, wd_ref[...])


def _out_proj(res_refs, ya_ref, yb_ref, ga_ref, gb_ref, wout_ref, rs):
    x = res_refs[0][rs, :]
    for r in res_refs[1:]:
        x = x + r[rs, :]
    na = _rms(ya_ref[rs, :], ga_ref[...]).astype(BF16)
    nb = _rms(yb_ref[rs, :], gb_ref[...]).astype(BF16)
    return x + _dot(na, wout_ref[0:MIX_A, :]) + _dot(nb, wout_ref[MIX_A:, :])


N_DENSE_PARTS = 2


def _ffn_kernel(*refs, n_res):
    (ya_ref, yb_ref, ga_ref, gb_ref, wout_ref, gf_ref, wg_ref, wu_ref, wd_ref, o_ref) = refs[n_res:]
    x1 = _out_proj(refs[:n_res], ya_ref, yb_ref, ga_ref, gb_ref, wout_ref, slice(None))
    hn = _rms(x1, gf_ref[...]).astype(BF16)
    o_ref[...] = x1
    for p in range(N_DENSE_PARTS):
        o_ref[...] += _swiglu(hn, wg_ref.at[p], wu_ref.at[p], wd_ref.at[p])


def _ffn(xs, ya, yb, ga, gb, wout, gf, wg, wu, wd, *, l, tm):
    m = xs[0].shape[0]
    j = l // 2
    one = pl.Buffered(1)
    lay3 = lambda i: (l, 0, 0)
    wsel = lambda i: (j, 0, 0)
    return pl.pallas_call(
        functools.partial(_ffn_kernel, n_res=len(xs)),
        grid=(m // tm,),
        in_specs=[pl.BlockSpec((tm, D_MODEL), lambda i: (i, 0)) for _ in xs] + [
            pl.BlockSpec((tm, MIX_A), lambda i: (i, 0)),
            pl.BlockSpec((tm, MIX_B), lambda i: (i, 0)),
            pl.BlockSpec((None, 1, MIX_A), lay3),
            pl.BlockSpec((None, 1, MIX_B), lay3),
            pl.BlockSpec((None, D_MODEL, D_MODEL), lay3, pipeline_mode=one),
            pl.BlockSpec((None, 1, D_MODEL), lay3),
            pl.BlockSpec((N_DENSE_PARTS, D_MODEL, FF_PAD), wsel, pipeline_mode=one),
            pl.BlockSpec((N_DENSE_PARTS, D_MODEL, FF_PAD), wsel, pipeline_mode=one),
            pl.BlockSpec((N_DENSE_PARTS, FF_PAD, D_MODEL), wsel, pipeline_mode=one),
        ],
        out_specs=pl.BlockSpec((tm, D_MODEL), lambda i: (i, 0)),
        out_shape=jax.ShapeDtypeStruct((m, D_MODEL), F32),
        compiler_params=pltpu.CompilerParams(
            dimension_semantics=("parallel",), vmem_limit_bytes=VMEM_LIMIT),
        name="ffn_dense",
    )(*xs, ya, yb, ga, gb, wout, gf, wg, wu, wd)


MOE_WIN = 256
SEG = 16
SORT_ROWS = 2 * MOE_WIN + N_EXPERTS * SEG
ROW_TILE = 256
NOT_ROUTED = -1.0e6


def _top2(logits):
    lane = lax.broadcasted_iota(jnp.int32, logits.shape, 1).astype(F32)
    neg = jnp.float32(-jnp.inf)
    lg = jnp.where(lane < N_EXPERTS, logits, neg)
    m1 = jnp.max(lg, axis=1, keepdims=True)
    i1 = jnp.min(jnp.where(lg == m1, lane, float(LANES)), axis=1, keepdims=True)
    lg2 = jnp.where(lane == i1, neg, lg)
    m2 = jnp.max(lg2, axis=1, keepdims=True)
    i2 = jnp.min(jnp.where(lg2 == m2, lane, float(LANES)), axis=1, keepdims=True)
    ex = jnp.exp(m2 - m1)
    w1 = 1.0 / (1.0 + ex)
    w2 = ex / (1.0 + ex)
    gates = jnp.where(lane == i1, w1, 0.0) + jnp.where(lane == i2, w2, 0.0)
    mask = jnp.where((lane == i1) | (lane == i2), 1.0, 0.0)
    return gates, mask


def _pre_moe_kernel(*refs, n_res, n_win):
    (ya_ref, yb_ref, ga_ref, gb_ref, wout_ref, gf_ref, rwh_ref, rwl_ref, rb_ref,
     x1_ref, hn_ref, gates_ref, pm_ref, pmt_ref, cnt_ref) = refs[n_res:]
    x1 = _out_proj(refs[:n_res], ya_ref, yb_ref, ga_ref, gb_ref, wout_ref, slice(None))
    x1_ref[...] = x1
    hn = _rms(x1, gf_ref[...])
    hn_ref[...] = hn.astype(BF16)
    hi, lo = _split_bf16(hn)
    logits = _dot(hi, rwh_ref[...]) + _dot(lo, rwh_ref[...]) + _dot(hi, rwl_ref[...])
    gates, mask = _top2(logits + rb_ref[...])
    gates_ref[...] = gates
    row = lax.broadcasted_iota(jnp.int32, (MOE_WIN, MOE_WIN), 0)
    col = lax.broadcasted_iota(jnp.int32, (MOE_WIN, MOE_WIN), 1)
    earlier = jnp.where(col < row, 1.0, 0.0).astype(BF16)
    for w in range(n_win):
        rs = slice(w * MOE_WIN, (w + 1) * MOE_WIN)
        mw = mask[rs, :]
        rank = _dot(earlier, mw.astype(BF16))
        pm = jnp.where(mw > 0.0, rank, NOT_ROUTED)
        pm_ref[rs, :] = pm
        pmt_ref[w] = pm.T[:SUBLANES, :]
        cnt_ref[w] = jnp.broadcast_to(jnp.sum(mw, axis=0, keepdims=True), (SUBLANES, LANES))


def _pre_moe(xs, ya, yb, ga, gb, wout, gf, rwh, rwl, rb, *, l, tm):
    m = xs[0].shape[0]
    n_win = tm // MOE_WIN
    j = l // 2
    lay3 = lambda i: (l, 0, 0)
    moe3 = lambda i: (j, 0, 0)
    return pl.pallas_call(
        functools.partial(_pre_moe_kernel, n_res=len(xs), n_win=n_win),
        grid=(m // tm,),
        in_specs=[pl.BlockSpec((tm, D_MODEL), lambda i: (i, 0)) for _ in xs] + [
            pl.BlockSpec((tm, MIX_A), lambda i: (i, 0)),
            pl.BlockSpec((tm, MIX_B), lambda i: (i, 0)),
            pl.BlockSpec((None, 1, MIX_A), lay3),
            pl.BlockSpec((None, 1, MIX_B), lay3),
            pl.BlockSpec((None, D_MODEL, D_MODEL), lay3),
            pl.BlockSpec((None, 1, D_MODEL), lay3),
            pl.BlockSpec((None, D_MODEL, LANES), moe3),
            pl.BlockSpec((None, D_MODEL, LANES), moe3),
            pl.BlockSpec((None, 1, LANES), moe3),
        ],
        out_specs=[
            pl.BlockSpec((tm, D_MODEL), lambda i: (i, 0)),
            pl.BlockSpec((tm, D_MODEL), lambda i: (i, 0)),
            pl.BlockSpec((tm, LANES), lambda i: (i, 0)),
            pl.BlockSpec((tm, LANES), lambda i: (i, 0)),
            pl.BlockSpec((n_win, SUBLANES, MOE_WIN), lambda i: (i, 0, 0)),
            pl.BlockSpec((n_win, SUBLANES, LANES), lambda i: (i, 0, 0)),
        ],
        out_shape=[
            jax.ShapeDtypeStruct((m, D_MODEL), F32),
            jax.ShapeDtypeStruct((m, D_MODEL), BF16),
            jax.ShapeDtypeStruct((m, LANES), F32),
            jax.ShapeDtypeStruct((m, LANES), F32),
            jax.ShapeDtypeStruct((m // MOE_WIN, SUBLANES, MOE_WIN), F32),
            jax.ShapeDtypeStruct((m // MOE_WIN, SUBLANES, LANES), F32),
        ],
        compiler_params=pltpu.CompilerParams(
            dimension_semantics=("parallel",), vmem_limit_bytes=VMEM_LIMIT),
        name="pre_moe",
    )(*xs, ya, yb, ga, gb, wout, gf, rwh, rwl, rb)


def _moe_kernel(cnt_sm, hn_ref, gates_ref, pm_ref, pmt_ref, wg_ref, wu_ref, wd_ref, o_ref,
                xs, gs, sb, gsb, pn_sm, s_sm, off_sm, est_sm, tot_sm, *, n_win):
    blk = pl.program_id(0)
    e = pl.program_id(1)

    def seg_copy(w, ee, to_sorted):
        s0 = s_sm[w * N_EXPERTS + ee]
        o0 = off_sm[w * N_EXPERTS + ee]

        def body(i, carry):
            src = pl.multiple_of(s0 + i * SEG, SEG)
            dst = pl.multiple_of(o0 + i * SEG, SEG)
            if to_sorted:
                xs[pl.ds(dst, SEG), :] = sb[pl.ds(src, SEG), :]
                gs[pl.ds(dst, SEG), :] = gsb[pl.ds(src, SEG), :]
            else:
                sb[pl.ds(src, SEG), :] = xs[pl.ds(dst, SEG), :]
            return carry

        lax.fori_loop(0, pn_sm[w * N_EXPERTS + ee] // SEG, body, 0)

    @pl.when(e == 0)
    def _dispatch():
        pn = [[None] * N_EXPERTS for _ in range(n_win)]
        for w in range(n_win):
            run = jnp.int32(0)
            for ee in range(N_EXPERTS):
                n = cnt_sm[(blk * n_win + w) * N_EXPERTS + ee]
                pn[w][ee] = jnp.bitwise_and(n + (SEG - 1), -SEG)
                pn_sm[w * N_EXPERTS + ee] = pn[w][ee]
                s_sm[w * N_EXPERTS + ee] = run
                run = run + pn[w][ee]
        run = jnp.int32(0)
        for ee in range(N_EXPERTS):
            est_sm[ee] = run
            start = run
            for w in range(n_win):
                off_sm[w * N_EXPERTS + ee] = run
                run = run + pn[w][ee]
            tot_sm[ee] = run - start
        xs[...] = jnp.zeros_like(xs)
        gs[...] = jnp.zeros_like(gs)
        riota = lax.broadcasted_iota(jnp.int32, (SORT_ROWS, MOE_WIN), 0).astype(F32)
        for w in range(n_win):
            rs = slice(w * MOE_WIN, (w + 1) * MOE_WIN)
            g = jnp.zeros((SORT_ROWS, MOE_WIN), F32)
            for ee in range(N_EXPERTS):
                dest = pmt_ref[w, ee:ee + 1, :] + s_sm[w * N_EXPERTS + ee].astype(F32)
                g = jnp.where(riota == dest, 1.0, g)
            gb = g.astype(BF16)
            sb[...] = _dot(gb, hn_ref[rs, :]).astype(BF16)
            gh, gl = _split_bf16(gates_ref[rs, :])
            gsb[...] = _dot(gb, gh) + _dot(gb, gl)
            for ee in range(N_EXPERTS):
                seg_copy(w, ee, True)

    start = est_sm[e]
    tot = tot_sm[e]

    def row_tile(r0, size, valid):
        r0 = pl.multiple_of(r0, SEG)
        xt = xs[pl.ds(r0, size), :]
        y = _swiglu(xt, wg_ref, wu_ref, wd_ref)
        lane = lax.broadcasted_iota(jnp.int32, (size, LANES), 1)
        gate = jnp.sum(jnp.where(lane == e, gs[pl.ds(r0, size), :], 0.0), axis=1, keepdims=True)
        keep = lax.broadcasted_iota(jnp.int32, (size, D_MODEL), 0) < valid
        xs[pl.ds(r0, size), :] = jnp.where(keep, (y * gate).astype(BF16), xt)

    n_full = tot // ROW_TILE

    def full_body(i, carry):
        row_tile(start + i * ROW_TILE, ROW_TILE, ROW_TILE)
        return carry

    lax.fori_loop(0, n_full, full_body, 0)
    rem = tot - n_full * ROW_TILE
    tail = start + n_full * ROW_TILE

    @pl.when(rem > ROW_TILE // 2)
    def _():
        row_tile(tail, ROW_TILE, rem)

    @pl.when((rem > 0) & (rem <= ROW_TILE // 2))
    def _():
        row_tile(tail, ROW_TILE // 2, rem)

    @pl.when(e == N_EXPERTS - 1)
    def _combine():
        liota = lax.broadcasted_iota(jnp.int32, (MOE_WIN, SORT_ROWS), 1).astype(F32)
        for w in range(n_win):
            rs = slice(w * MOE_WIN, (w + 1) * MOE_WIN)
            for ee in range(N_EXPERTS):
                seg_copy(w, ee, False)
            g = jnp.zeros((MOE_WIN, SORT_ROWS), F32)
            for ee in range(N_EXPERTS):
                dest = pm_ref[rs, ee:ee + 1] + s_sm[w * N_EXPERTS + ee].astype(F32)
                g = jnp.where(liota == dest, 1.0, g)
            o_ref[rs, :] = _dot(g.astype(BF16), sb[...])


def _moe(cnt, hn, gates, pm, pmt, wg, wu, wd, *, l, tb):
    m = hn.shape[0]
    n_win = tb // MOE_WIN
    j = l // 2
    xs_rows = 2 * tb + n_win * N_EXPERTS * SEG + ROW_TILE
    one = pl.Buffered(1)
    wsel = lambda i, e, c: (j * N_EXPERTS + e, 0, 0)
    grid_spec = pltpu.PrefetchScalarGridSpec(
        num_scalar_prefetch=1,
        grid=(m // tb, N_EXPERTS),
        in_specs=[
            pl.BlockSpec((tb, D_MODEL), lambda i, e, c: (i, 0), pipeline_mode=one),
            pl.BlockSpec((tb, LANES), lambda i, e, c: (i, 0), pipeline_mode=one),
            pl.BlockSpec((tb, LANES), lambda i, e, c: (i, 0), pipeline_mode=one),
            pl.BlockSpec((n_win, SUBLANES, MOE_WIN), lambda i, e, c: (i, 0, 0), pipeline_mode=one),
            pl.BlockSpec((None, D_MODEL, FF_PAD), wsel),
            pl.BlockSpec((None, D_MODEL, FF_PAD), wsel),
            pl.BlockSpec((None, FF_PAD, D_MODEL), wsel),
        ],
        out_specs=pl.BlockSpec((tb, D_MODEL), lambda i, e, c: (i, 0), pipeline_mode=one),
        scratch_shapes=[
            pltpu.VMEM((xs_rows, D_MODEL), BF16),
            pltpu.VMEM((xs_rows, LANES), F32),
            pltpu.VMEM((SORT_ROWS, D_MODEL), BF16),
            pltpu.VMEM((SORT_ROWS, LANES), F32),
            pltpu.SMEM((n_win * N_EXPERTS,), jnp.int32),
            pltpu.SMEM((n_win * N_EXPERTS,), jnp.int32),
            pltpu.SMEM((n_win * N_EXPERTS,), jnp.int32),
            pltpu.SMEM((N_EXPERTS,), jnp.int32),
            pltpu.SMEM((N_EXPERTS,), jnp.int32),
        ],
    )
    return pl.pallas_call(
        functools.partial(_moe_kernel, n_win=n_win),
        grid_spec=grid_spec,
        out_shape=jax.ShapeDtypeStruct((m, D_MODEL), F32),
        compiler_params=pltpu.CompilerParams(
            dimension_semantics=("arbitrary", "arbitrary"), vmem_limit_bytes=MOE_VMEM_LIMIT),
        name="moe",
    )(cnt, hn, gates, pm, pmt, wg, wu, wd)


def _final_norm_kernel(*refs, n_res):
    g_ref, o_ref = refs[n_res:]
    x = refs[0][...]
    for r in refs[1:n_res]:
        x = x + r[...]
    o_ref[...] = _rms(x, g_ref[...])


def _final_norm(xs, g, tm):
    m = xs[0].shape[0]
    return pl.pallas_call(
        functools.partial(_final_norm_kernel, n_res=len(xs)),
        grid=(m // tm,),
        in_specs=[pl.BlockSpec((tm, D_MODEL), lambda i: (i, 0)) for _ in xs]
        + [pl.BlockSpec((1, D_MODEL), lambda i: (0, 0))],
        out_specs=pl.BlockSpec((tm, D_MODEL), lambda i: (i, 0)),
        out_shape=jax.ShapeDtypeStruct((m, D_MODEL), F32),
        compiler_params=pltpu.CompilerParams(
            dimension_semantics=("parallel",), vmem_limit_bytes=VMEM_LIMIT),
        name="final_norm",
    )(*xs, g)


def kernel(x_prompt, x_sample, state_ssm_re, state_ssm_im, norm_mix, w_in, ssm_a_re, ssm_a_im, ssm_log_dt, ssm_b_re, ssm_b_im, ssm_c_re, ssm_c_im, ssm_d, glu_w, glu_b, sgu_w, sgu_b, sgu_ln_g, sgu_ln_b, out_norm_a, out_norm_b, w_out, norm_ffn, ffn_w_gate, ffn_w_up, ffn_w_down, router_w, router_b, moe_w_gate, moe_w_up, moe_w_down, norm_final):
    batch, seq = x_prompt.shape[0], x_prompt.shape[1]
    dec_batch, dec_seq = x_sample.shape[0], x_sample.shape[1]
    mp, ms = batch * seq, dec_batch * dec_seq
    n_moe = moe_w_gate.shape[0]

    lam, bbre, bbim = _s5_prep(ssm_a_re, ssm_a_im, ssm_log_dt, ssm_b_re, ssm_b_im)
    ctre = jnp.swapaxes(ssm_c_re, 2, 3).reshape(DEPTH * N_STATE, SSM_GROUP)
    ctim = jnp.swapaxes(ssm_c_im, 2, 3).reshape(DEPTH * N_STATE, SSM_GROUP)
    dskip = ssm_d.reshape(DEPTH, 1, MIX_A)
    glu = glu_w.reshape(DEPTH * MIX_A, 2 * SSM_GROUP)
    gbias = jnp.stack([glu_b[..., :SSM_GROUP].reshape(DEPTH, MIX_A),
                       glu_b[..., SSM_GROUP:].reshape(DEPTH, MIX_A)], axis=1)
    s5w = (lam, bbre, bbim, ctre, ctim, dskip, glu, gbias)
    w_in_b = w_in.astype(BF16)
    w_out_b = w_out.astype(BF16)
    g_mix = norm_mix.reshape(DEPTH, 1, D_MODEL)
    g_a = out_norm_a.reshape(DEPTH, 1, MIX_A)
    g_b = out_norm_b.reshape(DEPTH, 1, MIX_B)
    g_ffn = norm_ffn.reshape(DEPTH, 1, D_MODEL)
    sgu_bias = jnp.repeat(jnp.swapaxes(sgu_b, 1, 2), HD_B, axis=2)
    sgu_bias_s = sgu_bias[:, :dec_seq]
    lng = sgu_ln_g.reshape(DEPTH, 1, MIX_B)
    lnb = sgu_ln_b.reshape(DEPTH, 1, MIX_B)
    coef_s = jnp.repeat(
        jnp.transpose(sgu_w[:, :, :dec_seq, :dec_seq], (0, 2, 3, 1)).reshape(DEPTH, dec_seq * dec_seq, NH_B),
        HD_B, axis=2)
    dense_split = (FF_PAD, D_FF - FF_PAD)
    dense_g = _pack(jnp.swapaxes(ffn_w_gate, 1, 2), dense_split, transpose=True)
    dense_u = _pack(jnp.swapaxes(ffn_w_up, 1, 2), dense_split, transpose=True)
    dense_d = _pack(ffn_w_down, dense_split, transpose=False)
    moe_t = lambda w: jnp.swapaxes(w.reshape(-1, D_MODEL, D_FF_EXPERT), 1, 2)
    moe_g = _pack(moe_t(moe_w_gate), (D_FF_EXPERT,), transpose=True)
    moe_u = _pack(moe_t(moe_w_up), (D_FF_EXPERT,), transpose=True)
    moe_d = _pack(moe_w_down.reshape(-1, D_FF_EXPERT, D_MODEL), (D_FF_EXPERT,), transpose=False)
    rw = jnp.pad(router_w, ((0, 0), (0, 0), (0, LANES - N_EXPERTS)))
    rwh = rw.astype(BF16)
    rwl = (rw - rwh.astype(F32)).astype(BF16)
    rb = jnp.pad(router_b, ((0, 0), (0, LANES - N_EXPERTS))).reshape(n_moe, 1, LANES)
    gfin = norm_final.reshape(1, D_MODEL)

    h0re = state_ssm_re.reshape(DEPTH, dec_batch, N_STATE)
    h0im = state_ssm_im.reshape(DEPTH, dec_batch, N_STATE)

    def post_mixer(xres, ya, yb, l, tm, tb):
        common = (g_a, g_b, w_out_b, g_ffn)
        if l % 2 == 0:
            return (_ffn(xres, ya, yb, *common, dense_g, dense_u, dense_d, l=l, tm=tm),)
        x1, hn, gates, pm, pmt, cnt = _pre_moe(xres, ya, yb, *common, rwh, rwl, rb, l=l, tm=min(tm, 512))
        cnt = cnt[:, 0, :N_EXPERTS].astype(jnp.int32).reshape(-1)
        return (x1, _moe(cnt, hn, gates, pm, pmt, moe_g, moe_u, moe_d, l=l, tb=tb))

    xp = (x_prompt.reshape(mp, D_MODEL),)
    xs = (x_sample.reshape(ms, D_MODEL),)
    re_p, im_p, re_s, im_s, v_s = [], [], [], [], []
    for l in range(DEPTH):
        zp = _inproj(xp, g_mix, w_in_b, l, tm=1024)
        zs = _inproj(xs, g_mix, w_in_b, l, tm=ms)
        ya_p, hre_p, him_p = _s5_prompt(zp.reshape(batch, seq, IN_WIDTH), s5w, l, t_blk=64)
        ya_s, hre_s, him_s = _s5_sample(zs, h0re, h0im, s5w, l, nb=dec_batch, t_blk=dec_seq)
        yb_p = _sgu_prompt(zp, sgu_w, sgu_bias, lng, lnb, l, n_chunks=4)
        yb_s, vn_s = _sgu_sample(zs, coef_s, sgu_bias_s, lng, lnb, l, nb=dec_batch, t_blk=dec_seq)
        xp = post_mixer(xp, ya_p.reshape(mp, MIX_A), yb_p, l, tm=512, tb=seq)
        xs = post_mixer(xs, ya_s, yb_s, l, tm=ms, tb=ms)
        re_p.append(hre_p)
        im_p.append(him_p)
        re_s.append(hre_s)
        im_s.append(him_s)
        v_s.append(vn_s)
    xp = _final_norm(xp, gfin, tm=1024)
    xs = _final_norm(xs, gfin, tm=ms)

    st = lambda hs, b: jnp.stack(hs).reshape(DEPTH, b, NG_A, SSM_STATE)
    return (xp.reshape(batch, seq, D_MODEL), xs.reshape(dec_batch, dec_seq, D_MODEL),
            st(re_p, batch), st(im_p, batch), st(re_s, dec_batch), st(im_s, dec_batch),
            jnp.stack(v_s).reshape(DEPTH, dec_batch, dec_seq, NH_B, HD_B))
```

```python
import functools
import math

import jax
import jax.numpy as jnp
from jax import lax
from jax.experimental import pallas as pl
from jax.experimental.pallas import tpu as pltpu

F32 = jnp.float32
BF16 = jnp.bfloat16

D_MODEL = 1024
DEPTH = 4
MIX_A = 512
SSM_GROUP = 16
NG_A = 32
SSM_STATE = 64
N_STATE = NG_A * SSM_STATE
MIX_B = 512
HD_B = 64
NH_B = 8
CHUNK = 128
IN_WIDTH = MIX_A + 2 * MIX_B
D_FF = 2752
N_EXPERTS = 8
D_FF_EXPERT = D_FF // 2
FF_PAD = 1408
EPS = 1e-6

LANES = 128
SUBLANES = 8
MXU_DIM = 256
STATE_COLS = 512
N_SCOL = N_STATE // STATE_COLS
GROUPS_PER_CHUNK = STATE_COLS // SSM_STATE
VMEM_LIMIT = 52 * 1024 * 1024
MOE_VMEM_LIMIT = 58 * 1024 * 1024


def _rms(x, g):
    return x * lax.rsqrt(jnp.mean(x * x, axis=-1, keepdims=True) + EPS) * g


def _gelu(x):
    c = math.sqrt(2.0 / math.pi)
    return 0.5 * x * (1.0 + jnp.tanh(c * (x + 0.044715 * (x * x * x))))


def _sigmoid(x):
    return 1.0 / (1.0 + jnp.exp(-x))


def _split_bf16(x):
    hi = x.astype(BF16)
    lo = (x - hi.astype(F32)).astype(BF16)
    return hi, lo


def _dot(a, b):
    return jnp.dot(a, b, preferred_element_type=F32)


def _log2(n):
    assert n & (n - 1) == 0
    return n.bit_length() - 1


def _block_diag_tile(src, n_rep, rows_per_group):
    r_n, k = src.shape
    n = k * n_rep
    t_row = lax.broadcasted_iota(jnp.int32, (k, n), 0)
    t_col = lax.broadcasted_iota(jnp.int32, (k, n), 1)
    tile = jnp.where(jnp.bitwise_and(t_col, k - 1) == t_row, 1.0, 0.0).astype(BF16)
    full = _dot(src.astype(BF16), tile)
    row = lax.broadcasted_iota(jnp.int32, (r_n, n), 0)
    col = lax.broadcasted_iota(jnp.int32, (r_n, n), 1)
    same = lax.shift_right_logical(row, _log2(rows_per_group)) == lax.shift_right_logical(col, _log2(k))
    return jnp.where(same, full, 0.0).astype(BF16)


def _pack_kernel(w_ref, o_ref, *, widths, transpose):
    for i, width in enumerate(widths):
        w = w_ref[i * FF_PAD:i * FF_PAD + width, :]
        if width < FF_PAD:
            w = jnp.concatenate([w, jnp.zeros((FF_PAD - width, w.shape[1]), F32)], axis=0)
        o_ref[i] = (w.T if transpose else w).astype(BF16)


def _pack(w, widths, transpose):
    n_e, n_parts = w.shape[0], len(widths)
    out_blk = (n_parts, D_MODEL, FF_PAD) if transpose else (n_parts, FF_PAD, D_MODEL)
    out = pl.pallas_call(
        functools.partial(_pack_kernel, widths=widths, transpose=transpose),
        grid=(n_e,),
        in_specs=[pl.BlockSpec((None,) + w.shape[1:], lambda e: (e, 0, 0))],
        out_specs=pl.BlockSpec((None,) + out_blk, lambda e: (e, 0, 0, 0)),
        out_shape=jax.ShapeDtypeStruct((n_e,) + out_blk, BF16),
        compiler_params=pltpu.CompilerParams(
            dimension_semantics=("parallel",), vmem_limit_bytes=VMEM_LIMIT),
        name="pack_t" if transpose else "pack",
    )(w)
    return out.reshape((n_e * n_parts,) + out_blk[1:])


def _s5_prep_kernel(are_ref, aim_ref, ldt_ref, bre_ref, bim_ref,
                    abre_ref, abim_ref, bbre_ref, bbim_ref):
    lam_re = are_ref[...]
    lam_im = aim_ref[...]
    dt = jnp.exp(ldt_ref[...])
    mag = jnp.exp(lam_re * dt)
    ab_re = mag * jnp.cos(lam_im * dt)
    ab_im = mag * jnp.sin(lam_im * dt)
    den = lam_re * lam_re + lam_im * lam_im
    nr = ab_re - 1.0
    q_re = (nr * lam_re + ab_im * lam_im) / den
    q_im = (ab_im * lam_re - nr * lam_im) / den
    b_re = bre_ref[...]
    b_im = bim_ref[...]
    abre_ref[...] = ab_re
    abim_ref[...] = ab_im
    bbre_ref[...] = q_re * b_re - q_im * b_im
    bbim_ref[...] = q_re * b_im + q_im * b_re


def _s5_prep(ssm_a_re, ssm_a_im, ssm_log_dt, ssm_b_re, ssm_b_im):
    shp = (DEPTH, NG_A, SSM_GROUP, SSM_STATE)
    rows = DEPTH * NG_A * SSM_GROUP
    bc = lambda a: jnp.broadcast_to(a, shp).reshape(rows, SSM_STATE)
    are = bc(ssm_a_re[:, :, None, :])
    aim = bc(ssm_a_im[:, :, None, :])
    ldt = bc(ssm_log_dt[:, :, None, None])
    bre = jnp.swapaxes(ssm_b_re, 2, 3).reshape(rows, SSM_STATE)
    bim = jnp.swapaxes(ssm_b_im, 2, 3).reshape(rows, SSM_STATE)
    sds = jax.ShapeDtypeStruct((rows, SSM_STATE), F32)
    abre, abim, bbre, bbim = pl.pallas_call(
        _s5_prep_kernel, out_shape=(sds, sds, sds, sds), name="s5_prep",
    )(are, aim, ldt, bre, bim)
    lam_re = abre.reshape(shp)[:, :, 0, :].reshape(DEPTH, 1, N_STATE)
    lam_im = abim.reshape(shp)[:, :, 0, :].reshape(DEPTH, 1, N_STATE)
    lam = jnp.concatenate([lam_re, lam_im], axis=1)
    return lam, bbre, bbim


def _inproj_kernel(*refs, n_res):
    g_ref, w_ref, z_ref = refs[n_res:]
    x = refs[0][...]
    for r in refs[1:n_res]:
        x = x + r[...]
    hn = _rms(x, g_ref[...])
    z_ref[...] = _dot(hn.astype(BF16), w_ref[...])


def _inproj(xs, g, w, l, tm):
    m = xs[0].shape[0]
    return pl.pallas_call(
        functools.partial(_inproj_kernel, n_res=len(xs)),
        grid=(m // tm,),
        in_specs=[pl.BlockSpec((tm, D_MODEL), lambda i: (i, 0)) for _ in xs] + [
            pl.BlockSpec((None, 1, D_MODEL), lambda i: (l, 0, 0)),
            pl.BlockSpec((None, D_MODEL, IN_WIDTH), lambda i: (l, 0, 0)),
        ],
        out_specs=pl.BlockSpec((tm, IN_WIDTH), lambda i: (i, 0)),
        out_shape=jax.ShapeDtypeStruct((m, IN_WIDTH), F32),
        compiler_params=pltpu.CompilerParams(
            dimension_semantics=("parallel",), vmem_limit_bytes=VMEM_LIMIT),
        name="inproj",
    )(*xs, g, w)


def _s5_kernel(*refs, nb, t_blk, prompt):
    n_in = 1 if prompt else 3
    (bbre_ref, bbim_ref, lam_ref, ctre_ref, ctim_ref, d_ref, glu_ref, gbias_ref,
     ya_ref, hre_ref, him_ref, utm, bre, bim, otm, bblk, cre, cimn, wa, wg) = refs[n_in:n_in + 20]
    u_ref = refs[0]

    @pl.when(pl.program_id(0) == 0)
    def _():
        if prompt:
            hre_ref[...] = jnp.zeros_like(hre_ref)
            him_ref[...] = jnp.zeros_like(him_ref)
        else:
            hre_ref[...] = refs[1][...]
            him_ref[...] = refs[2][...]
        for c in range(4):
            rs = slice(c * LANES, (c + 1) * LANES)
            bblk[c, :, :STATE_COLS] = _block_diag_tile(bbre_ref[rs, :], GROUPS_PER_CHUNK, SSM_GROUP)
            bblk[c, :, STATE_COLS:] = _block_diag_tile(bbim_ref[rs, :], GROUPS_PER_CHUNK, SSM_GROUP)
            ss = slice(c * STATE_COLS, (c + 1) * STATE_COLS)
            cre[c] = _block_diag_tile(ctre_ref[ss, :], GROUPS_PER_CHUNK, SSM_STATE)
            cimn[c] = _block_diag_tile(-ctim_ref[ss, :], GROUPS_PER_CHUNK, SSM_STATE)
        for b in range(MIX_A // MXU_DIM):
            ms = slice(b * MXU_DIM, (b + 1) * MXU_DIM)
            wa[b] = _block_diag_tile(glu_ref[ms, :SSM_GROUP], MXU_DIM // SSM_GROUP, SSM_GROUP)
            wg[b] = _block_diag_tile(glu_ref[ms, SSM_GROUP:], MXU_DIM // SSM_GROUP, SSM_GROUP)

    for c in range(4):
        ls = slice(c * LANES, (c + 1) * LANES)
        if prompt:
            for b in range(nb):
                utm[c, pl.ds(b, t_blk, stride=nb), :] = u_ref[b, :, ls]
        else:
            tmp = refs[n_in + 20]
            tmp[c] = u_ref[:, ls]
            for t in range(t_blk):
                utm[c, t * nb:(t + 1) * nb, :] = tmp[c, pl.ds(t, nb, stride=t_blk), :]

    for c in range(4):
        r = _dot(utm[c].astype(BF16), bblk[c])
        bre[:, c * STATE_COLS:(c + 1) * STATE_COLS] = r[:, :STATE_COLS]
        bim[:, c * STATE_COLS:(c + 1) * STATE_COLS] = r[:, STATE_COLS:]

    for c in range(N_SCOL):
        cs = slice(c * STATE_COLS, (c + 1) * STATE_COLS)
        lr = jnp.broadcast_to(lam_ref[0:1, cs], (SUBLANES, STATE_COLS))
        li = jnp.broadcast_to(lam_ref[1:2, cs], (SUBLANES, STATE_COLS))

        if nb == SUBLANES:
            hr, hi = hre_ref[:, cs], him_ref[:, cs]
            for t in range(t_blk):
                rs = slice(t * SUBLANES, (t + 1) * SUBLANES)
                hr, hi = lr * hr - li * hi + bre[rs, cs], lr * hi + li * hr + bim[rs, cs]
                bre[rs, cs] = hr
                bim[rs, cs] = hi
            hre_ref[:, cs] = hr
            him_ref[:, cs] = hi
            continue

        def group_body(bg, carry, cs=cs, lr=lr, li=li):
            r0 = pl.multiple_of(bg * SUBLANES, SUBLANES)

            def t_body(t, h):
                hr, hi = h
                row = pl.multiple_of(t * nb + r0, SUBLANES)
                nr = lr * hr - li * hi + bre[pl.ds(row, SUBLANES), cs]
                ni = lr * hi + li * hr + bim[pl.ds(row, SUBLANES), cs]
                bre[pl.ds(row, SUBLANES), cs] = nr
                bim[pl.ds(row, SUBLANES), cs] = ni
                return nr, ni

            h0 = (hre_ref[pl.ds(r0, SUBLANES), cs], him_ref[pl.ds(r0, SUBLANES), cs])
            hr, hi = lax.fori_loop(0, t_blk, t_body, h0, unroll=min(t_blk, 8))
            hre_ref[pl.ds(r0, SUBLANES), cs] = hr
            him_ref[pl.ds(r0, SUBLANES), cs] = hi
            return carry

        lax.fori_loop(0, nb // SUBLANES, group_body, 0)

    ys = []
    for c in range(4):
        cs = slice(c * STATE_COLS, (c + 1) * STATE_COLS)
        ls = slice(c * LANES, (c + 1) * LANES)
        y = _dot(bre[:, cs].astype(BF16), cre[c]) + _dot(bim[:, cs].astype(BF16), cimn[c])
        ys.append(_gelu(y + d_ref[:, ls] * utm[c]).astype(BF16))

    os_ = []
    for b in range(MIX_A // MXU_DIM):
        ms = slice(b * MXU_DIM, (b + 1) * MXU_DIM)
        yb = jnp.concatenate(ys[2 * b:2 * b + 2], axis=1)
        za = _dot(yb, wa[b]) + gbias_ref[0:1, ms]
        zg = _dot(yb, wg[b]) + gbias_ref[1:2, ms]
        os_.append(za * _sigmoid(zg))
    o = jnp.concatenate(os_, axis=1)

    for c in range(4):
        ls = slice(c * LANES, (c + 1) * LANES)
        if prompt:
            otm[c] = o[:, ls]
            for b in range(nb):
                ya_ref[b, :, ls] = otm[c, pl.ds(b, t_blk, stride=nb), :]
        else:
            for t in range(t_blk):
                otm[c, pl.ds(t, nb, stride=t_blk), :] = o[t * nb:(t + 1) * nb, ls]
            ya_ref[:, ls] = otm[c]


def _s5_call(u_specs, u_args, out_block, out_shape, weights, l, nb, t_blk, prompt, n_steps):
    rows = nb * t_blk
    lam, bbre, bbim, ctre, ctim, dskip, glu, gbias = weights
    w_specs = [
        pl.BlockSpec((NG_A * SSM_GROUP, SSM_STATE), lambda j: (l, 0)),
        pl.BlockSpec((NG_A * SSM_GROUP, SSM_STATE), lambda j: (l, 0)),
        pl.BlockSpec((None, 2, N_STATE), lambda j: (l, 0, 0)),
        pl.BlockSpec((N_STATE, SSM_GROUP), lambda j: (l, 0)),
        pl.BlockSpec((N_STATE, SSM_GROUP), lambda j: (l, 0)),
        pl.BlockSpec((None, 1, MIX_A), lambda j: (l, 0, 0)),
        pl.BlockSpec((MIX_A, 2 * SSM_GROUP), lambda j: (l, 0)),
        pl.BlockSpec((None, 2, MIX_A), lambda j: (l, 0, 0)),
    ]
    scratch = [
        pltpu.VMEM((4, rows, LANES), F32),
        pltpu.VMEM((rows, N_STATE), F32),
        pltpu.VMEM((rows, N_STATE), F32),
        pltpu.VMEM((4, rows, LANES), F32),
        pltpu.VMEM((4, LANES, 2 * STATE_COLS), BF16),
        pltpu.VMEM((4, STATE_COLS, LANES), BF16),
        pltpu.VMEM((4, STATE_COLS, LANES), BF16),
        pltpu.VMEM((MIX_A // MXU_DIM, MXU_DIM, MXU_DIM), BF16),
        pltpu.VMEM((MIX_A // MXU_DIM, MXU_DIM, MXU_DIM), BF16),
    ]
    if not prompt:
        scratch.append(pltpu.VMEM((4, rows, LANES), F32))
    return pl.pallas_call(
        functools.partial(_s5_kernel, nb=nb, t_blk=t_blk, prompt=prompt),
        grid=(n_steps,),
        in_specs=u_specs + w_specs,
        out_specs=[
            out_block,
            pl.BlockSpec((nb, N_STATE), lambda j: (0, 0)),
            pl.BlockSpec((nb, N_STATE), lambda j: (0, 0)),
        ],
        out_shape=[
            out_shape,
            jax.ShapeDtypeStruct((nb, N_STATE), F32),
            jax.ShapeDtypeStruct((nb, N_STATE), F32),
        ],
        scratch_shapes=scratch,
        compiler_params=pltpu.CompilerParams(
            dimension_semantics=("arbitrary",), vmem_limit_bytes=VMEM_LIMIT),
        name="s5_prompt" if prompt else "s5_sample",
    )(*u_args, bbre, bbim, lam, ctre, ctim, dskip, glu, gbias)


def _s5_prompt(z3, weights, l, t_blk):
    nb, seq = z3.shape[0], z3.shape[1]
    blk = pl.BlockSpec((nb, t_blk, MIX_A), lambda j: (0, j, 0))
    return _s5_call([blk], [z3], blk, jax.ShapeDtypeStruct((nb, seq, MIX_A), F32),
                    weights, l, nb, t_blk, True, seq // t_blk)


def _s5_sample(z, h0re, h0im, weights, l, nb, t_blk):
    rows = nb * t_blk
    blk = pl.BlockSpec((rows, MIX_A), lambda j: (0, 0))
    st = pl.BlockSpec((None, nb, N_STATE), lambda j: (l, 0, 0))
    return _s5_call([blk, st, st], [z, h0re, h0im], blk, jax.ShapeDtypeStruct((rows, MIX_A), F32),
                    weights, l, nb, t_blk, False, 1)


def _group_layernorm(gv, lng_ref, lnb_ref):
    row = lax.broadcasted_iota(jnp.int32, (MXU_DIM, MXU_DIM), 0)
    col = lax.broadcasted_iota(jnp.int32, (MXU_DIM, MXU_DIM), 1)
    same = lax.shift_right_logical(row, _log2(HD_B)) == lax.shift_right_logical(col, _log2(HD_B))
    gm = jnp.where(same, 1.0 / HD_B, 0.0).astype(BF16)

    def gmean(a):
        hi, lo = _split_bf16(a)
        parts = []
        for b in range(MIX_B // MXU_DIM):
            ms = slice(b * MXU_DIM, (b + 1) * MXU_DIM)
            parts.append(_dot(hi[:, ms], gm) + _dot(lo[:, ms], gm))
        return jnp.concatenate(parts, axis=1)

    xc = gv - gmean(gv)
    var = gmean(xc * xc)
    return xc * lax.rsqrt(var + EPS) * lng_ref[...] + lnb_ref[...]


def _inproj_sgu_kernel(*refs, n_res, n_chunks):
    (g_ref, w_ref, ws_ref, bias_ref, lng_ref, lnb_ref, ua_ref, o_ref, zb) = refs[n_res:]
    x = refs[0][...]
    for r in refs[1:n_res]:
        x = x + r[...]
    hn = _rms(x, g_ref[...]).astype(BF16)
    ua_ref[...] = _dot(hn, w_ref[:, :MIX_A])
    zb[...] = _dot(hn, w_ref[:, MIX_A:])

    row = lax.broadcasted_iota(jnp.int32, (CHUNK, 2 * CHUNK), 0)
    col = lax.broadcasted_iota(jnp.int32, (CHUNK, 2 * CHUNK), 1)
    causal = jnp.bitwise_and(col, CHUNK - 1) <= row
    lane = lax.broadcasted_iota(jnp.int32, (CHUNK, LANES), 1)
    first_head = lane < HD_B
    wcat = []
    for p in range(NH_B // 2):
        w = jnp.concatenate([ws_ref[2 * p], ws_ref[2 * p + 1]], axis=1)
        wcat.append(jnp.where(causal, w, 0.0).astype(BF16))
    for n in range(n_chunks):
        rs = slice(n * CHUNK, (n + 1) * CHUNK)
        u = _gelu(zb[rs, :MIX_B])
        vn = _group_layernorm(_gelu(zb[rs, MIX_B:]), lng_ref, lnb_ref)
        for p in range(NH_B // 2):
            ls = slice(p * LANES, (p + 1) * LANES)
            vp = vn[:, ls]
            rhs = jnp.concatenate(
                [jnp.where(first_head, vp, 0.0), jnp.where(first_head, 0.0, vp)], axis=0).astype(BF16)
            s = _dot(wcat[p], rhs)
            o_ref[rs, ls] = u[:, ls] * (s + bias_ref[:, ls])


def _inproj_sgu(xs, g, w, ws, bias, lng, lnb, l, n_chunks):
    m = xs[0].shape[0]
    tm = n_chunks * CHUNK
    lay3 = lambda i: (l, 0, 0)
    return pl.pallas_call(
        functools.partial(_inproj_sgu_kernel, n_res=len(xs), n_chunks=n_chunks),
        grid=(m // tm,),
        in_specs=[pl.BlockSpec((tm, D_MODEL), lambda i: (i, 0)) for _ in xs] + [
            pl.BlockSpec((None, 1, D_MODEL), lay3),
            pl.BlockSpec((None, D_MODEL, IN_WIDTH), lay3),
            pl.BlockSpec((None, NH_B, CHUNK, CHUNK), lambda i: (l, 0, 0, 0)),
            pl.BlockSpec((None, CHUNK, MIX_B), lay3),
            pl.BlockSpec((None, 1, MIX_B), lay3),
            pl.BlockSpec((None, 1, MIX_B), lay3),
        ],
        out_specs=[pl.BlockSpec((tm, MIX_A), lambda i: (i, 0)), pl.BlockSpec((tm, MIX_B), lambda i: (i, 0))],
        out_shape=[jax.ShapeDtypeStruct((m, MIX_A), F32), jax.ShapeDtypeStruct((m, MIX_B), F32)],
        scratch_shapes=[pltpu.VMEM((tm, 2 * MIX_B), F32)],
        compiler_params=pltpu.CompilerParams(
            dimension_semantics=("parallel",), vmem_limit_bytes=VMEM_LIMIT),
        name="inproj_sgu",
    )(*xs, g, w, ws, bias, lng, lnb)


def _sgu_sample_kernel(u_ref, v_ref, coef_ref, bias_ref, lng_ref, lnb_ref, o_ref, vn_ref,
                       usc, vsc, osc, *, nb, t_blk):
    vn = _group_layernorm(_gelu(v_ref[...]), lng_ref, lnb_ref)
    vn_ref[...] = vn
    u = _gelu(u_ref[...])
    for c in range(4):
        ls = slice(c * LANES, (c + 1) * LANES)
        vsc[c] = vn[:, ls]
        usc[c] = u[:, ls]
        for q in range(t_blk):
            s = bias_ref[q:q + 1, ls]
            for k in range(q + 1):
                s = s + coef_ref[q * t_blk + k:q * t_blk + k + 1, ls] * vsc[c, pl.ds(k, nb, stride=t_blk), :]
            osc[c, pl.ds(q, nb, stride=t_blk), :] = usc[c, pl.ds(q, nb, stride=t_blk), :] * s
        o_ref[:, ls] = osc[c]


def _sgu_sample(z, coef, bias, lng, lnb, l, nb, t_blk):
    m = nb * t_blk
    fix2 = lambda i: (0, 0)
    lay3 = lambda i: (l, 0, 0)
    return pl.pallas_call(
        functools.partial(_sgu_sample_kernel, nb=nb, t_blk=t_blk),
        grid=(1,),
        in_specs=[
            pl.BlockSpec((m, MIX_B), lambda i: (0, 1)),
            pl.BlockSpec((m, MIX_B), lambda i: (0, 2)),
            pl.BlockSpec((None, t_blk * t_blk, MIX_B), lay3),
            pl.BlockSpec((None, t_blk, MIX_B), lay3),
            pl.BlockSpec((None, 1, MIX_B), lay3),
            pl.BlockSpec((None, 1, MIX_B), lay3),
        ],
        out_specs=[pl.BlockSpec((m, MIX_B), fix2), pl.BlockSpec((m, MIX_B), fix2)],
        out_shape=[jax.ShapeDtypeStruct((m, MIX_B), F32), jax.ShapeDtypeStruct((m, MIX_B), F32)],
        scratch_shapes=[pltpu.VMEM((4, m, LANES), F32)] * 3,
        compiler_params=pltpu.CompilerParams(
            dimension_semantics=("arbitrary",), vmem_limit_bytes=VMEM_LIMIT),
        name="sgu_sample",
    )(z, z, coef, bias, lng, lnb)


def _swiglu(x, wg_ref, wu_ref, wd_ref):
    hg = _dot(x, wg_ref[...])
    a = (hg * _sigmoid(hg) * _dot(x, wu_ref[...])).astype(BF16)
    return _dot(a, wd_ref[...])


def _out_proj(res_refs, ya_ref, yb_ref, ga_ref, gb_ref, wout_ref, rs):
    x = res_refs[0][rs, :]
    for r in res_refs[1:]:
        x = x + r[rs, :]
    na = _rms(ya_ref[rs, :], ga_ref[...]).astype(BF16)
    nb = _rms(yb_ref[rs, :], gb_ref[...]).astype(BF16)
    return x + _dot(na, wout_ref[0:MIX_A, :]) + _dot(nb, wout_ref[MIX_A:, :])


N_DENSE_PARTS = 2


def _ffn_kernel(*refs, n_res):
    (ya_ref, yb_ref, ga_ref, gb_ref, wout_ref, gf_ref, wg_ref, wu_ref, wd_ref, o_ref) = refs[n_res:]
    x1 = _out_proj(refs[:n_res], ya_ref, yb_ref, ga_ref, gb_ref, wout_ref, slice(None))
    hn = _rms(x1, gf_ref[...]).astype(BF16)
    o_ref[...] = x1
    for p in range(N_DENSE_PARTS):
        o_ref[...] += _swiglu(hn, wg_ref.at[p], wu_ref.at[p], wd_ref.at[p])


def _ffn(xs, ya, yb, ga, gb, wout, gf, wg, wu, wd, *, l, tm):
    m = xs[0].shape[0]
    j = l // 2
    one = pl.Buffered(1)
    lay3 = lambda i: (l, 0, 0)
    wsel = lambda i: (j, 0, 0)
    return pl.pallas_call(
        functools.partial(_ffn_kernel, n_res=len(xs)),
        grid=(m // tm,),
        in_specs=[pl.BlockSpec((tm, D_MODEL), lambda i: (i, 0)) for _ in xs] + [
            pl.BlockSpec((tm, MIX_A), lambda i: (i, 0)),
            pl.BlockSpec((tm, MIX_B), lambda i: (i, 0)),
            pl.BlockSpec((None, 1, MIX_A), lay3),
            pl.BlockSpec((None, 1, MIX_B), lay3),
            pl.BlockSpec((None, D_MODEL, D_MODEL), lay3, pipeline_mode=one),
            pl.BlockSpec((None, 1, D_MODEL), lay3),
            pl.BlockSpec((N_DENSE_PARTS, D_MODEL, FF_PAD), wsel, pipeline_mode=one),
            pl.BlockSpec((N_DENSE_PARTS, D_MODEL, FF_PAD), wsel, pipeline_mode=one),
            pl.BlockSpec((N_DENSE_PARTS, FF_PAD, D_MODEL), wsel, pipeline_mode=one),
        ],
        out_specs=pl.BlockSpec((tm, D_MODEL), lambda i: (i, 0)),
        out_shape=jax.ShapeDtypeStruct((m, D_MODEL), F32),
        compiler_params=pltpu.CompilerParams(
            dimension_semantics=("parallel",), vmem_limit_bytes=VMEM_LIMIT),
        name="ffn_dense",
    )(*xs, ya, yb, ga, gb, wout, gf, wg, wu, wd)


MOE_WIN = 256
SEG = 16
SORT_ROWS = 2 * MOE_WIN + N_EXPERTS * SEG
ROW_TILE = 256
NOT_ROUTED = -1.0e6


def _top2(logits):
    lane = lax.broadcasted_iota(jnp.int32, logits.shape, 1).astype(F32)
    neg = jnp.float32(-jnp.inf)
    lg = jnp.where(lane < N_EXPERTS, logits, neg)
    m1 = jnp.max(lg, axis=1, keepdims=True)
    i1 = jnp.min(jnp.where(lg == m1, lane, float(LANES)), axis=1, keepdims=True)
    lg2 = jnp.where(lane == i1, neg, lg)
    m2 = jnp.max(lg2, axis=1, keepdims=True)
    i2 = jnp.min(jnp.where(lg2 == m2, lane, float(LANES)), axis=1, keepdims=True)
    ex = jnp.exp(m2 - m1)
    w1 = 1.0 / (1.0 + ex)
    w2 = ex / (1.0 + ex)
    gates = jnp.where(lane == i1, w1, 0.0) + jnp.where(lane == i2, w2, 0.0)
    mask = jnp.where((lane == i1) | (lane == i2), 1.0, 0.0)
    return gates, mask


def _pre_moe_kernel(*refs, n_res, n_win):
    (ya_ref, yb_ref, ga_ref, gb_ref, wout_ref, gf_ref, rwh_ref, rwl_ref, rb_ref,
     x1_ref, hn_ref, gates_ref, pm_ref, pmt_ref, cnt_ref) = refs[n_res:]
    x1 = _out_proj(refs[:n_res], ya_ref, yb_ref, ga_ref, gb_ref, wout_ref, slice(None))
    x1_ref[...] = x1
    hn = _rms(x1, gf_ref[...])
    hn_ref[...] = hn.astype(BF16)
    hi, lo = _split_bf16(hn)
    logits = _dot(hi, rwh_ref[...]) + _dot(lo, rwh_ref[...]) + _dot(hi, rwl_ref[...])
    gates, mask = _top2(logits + rb_ref[...])
    gates_ref[...] = gates
    row = lax.broadcasted_iota(jnp.int32, (MOE_WIN, MOE_WIN), 0)
    col = lax.broadcasted_iota(jnp.int32, (MOE_WIN, MOE_WIN), 1)
    earlier = jnp.where(col < row, 1.0, 0.0).astype(BF16)
    for w in range(n_win):
        rs = slice(w * MOE_WIN, (w + 1) * MOE_WIN)
        mw = mask[rs, :]
        rank = _dot(earlier, mw.astype(BF16))
        pm = jnp.where(mw > 0.0, rank, NOT_ROUTED)
        pm_ref[rs, :] = pm
        pmt_ref[w] = pm.T[:SUBLANES, :]
        cnt_ref[w] = jnp.broadcast_to(jnp.sum(mw, axis=0, keepdims=True), (SUBLANES, LANES))


def _pre_moe(xs, ya, yb, ga, gb, wout, gf, rwh, rwl, rb, *, l, tm):
    m = xs[0].shape[0]
    n_win = tm // MOE_WIN
    j = l // 2
    lay3 = lambda i: (l, 0, 0)
    moe3 = lambda i: (j, 0, 0)
    return pl.pallas_call(
        functools.partial(_pre_moe_kernel, n_res=len(xs), n_win=n_win),
        grid=(m // tm,),
        in_specs=[pl.BlockSpec((tm, D_MODEL), lambda i: (i, 0)) for _ in xs] + [
            pl.BlockSpec((tm, MIX_A), lambda i: (i, 0)),
            pl.BlockSpec((tm, MIX_B), lambda i: (i, 0)),
            pl.BlockSpec((None, 1, MIX_A), lay3),
            pl.BlockSpec((None, 1, MIX_B), lay3),
            pl.BlockSpec((None, D_MODEL, D_MODEL), lay3),
            pl.BlockSpec((None, 1, D_MODEL), lay3),
            pl.BlockSpec((None, D_MODEL, LANES), moe3),
            pl.BlockSpec((None, D_MODEL, LANES), moe3),
            pl.BlockSpec((None, 1, LANES), moe3),
        ],
        out_specs=[
            pl.BlockSpec((tm, D_MODEL), lambda i: (i, 0)),
            pl.BlockSpec((tm, D_MODEL), lambda i: (i, 0)),
            pl.BlockSpec((tm, LANES), lambda i: (i, 0)),
            pl.BlockSpec((tm, LANES), lambda i: (i, 0)),
            pl.BlockSpec((n_win, SUBLANES, MOE_WIN), lambda i: (i, 0, 0)),
            pl.BlockSpec((n_win, SUBLANES, LANES), lambda i: (i, 0, 0)),
        ],
        out_shape=[
            jax.ShapeDtypeStruct((m, D_MODEL), F32),
            jax.ShapeDtypeStruct((m, D_MODEL), BF16),
            jax.ShapeDtypeStruct((m, LANES), F32),
            jax.ShapeDtypeStruct((m, LANES), F32),
            jax.ShapeDtypeStruct((m // MOE_WIN, SUBLANES, MOE_WIN), F32),
            jax.ShapeDtypeStruct((m // MOE_WIN, SUBLANES, LANES), F32),
        ],
        compiler_params=pltpu.CompilerParams(
            dimension_semantics=("parallel",), vmem_limit_bytes=VMEM_LIMIT),
        name="pre_moe",
    )(*xs, ya, yb, ga, gb, wout, gf, rwh, rwl, rb)


def _moe_kernel(cnt_sm, hn_ref, gates_ref, pm_ref, pmt_ref, wg_ref, wu_ref, wd_ref, o_ref,
                xs, gs, sb, gsb, pn_sm, s_sm, off_sm, est_sm, tot_sm, *, n_win):
    blk = pl.program_id(0)
    e = pl.program_id(1)

    def seg_copy(w, ee, to_sorted):
        s0 = s_sm[w * N_EXPERTS + ee]
        o0 = off_sm[w * N_EXPERTS + ee]

        def body(i, carry):
            src = pl.multiple_of(s0 + i * SEG, SEG)
            dst = pl.multiple_of(o0 + i * SEG, SEG)
            if to_sorted:
                xs[pl.ds(dst, SEG), :] = sb[pl.ds(src, SEG), :]
                gs[pl.ds(dst, SEG), :] = gsb[pl.ds(src, SEG), :]
            else:
                sb[pl.ds(src, SEG), :] = xs[pl.ds(dst, SEG), :]
            return carry

        lax.fori_loop(0, pn_sm[w * N_EXPERTS + ee] // SEG, body, 0)

    @pl.when(e == 0)
    def _dispatch():
        pn = [[None] * N_EXPERTS for _ in range(n_win)]
        for w in range(n_win):
            run = jnp.int32(0)
            for ee in range(N_EXPERTS):
                n = cnt_sm[(blk * n_win + w) * N_EXPERTS + ee]
                pn[w][ee] = jnp.bitwise_and(n + (SEG - 1), -SEG)
                pn_sm[w * N_EXPERTS + ee] = pn[w][ee]
                s_sm[w * N_EXPERTS + ee] = run
                run = run + pn[w][ee]
        run = jnp.int32(0)
        for ee in range(N_EXPERTS):
            est_sm[ee] = run
            start = run
            for w in range(n_win):
                off_sm[w * N_EXPERTS + ee] = run
                run = run + pn[w][ee]
            tot_sm[ee] = run - start
        end = pl.multiple_of(run, SEG)
        xs[pl.ds(end, ROW_TILE), :] = jnp.zeros((ROW_TILE, D_MODEL), BF16)
        gs[pl.ds(end, ROW_TILE), :] = jnp.zeros((ROW_TILE, LANES), F32)
        riota = lax.broadcasted_iota(jnp.int32, (SORT_ROWS, MOE_WIN), 0).astype(F32)
        for w in range(n_win):
            rs = slice(w * MOE_WIN, (w + 1) * MOE_WIN)
            g = jnp.zeros((SORT_ROWS, MOE_WIN), F32)
            for ee in range(N_EXPERTS):
                dest = pmt_ref[w, ee:ee + 1, :] + s_sm[w * N_EXPERTS + ee].astype(F32)
                g = jnp.where(riota == dest, 1.0, g)
            gb = g.astype(BF16)
            sb[...] = _dot(gb, hn_ref[rs, :]).astype(BF16)
            gh, gl = _split_bf16(gates_ref[rs, :])
            gsb[...] = _dot(gb, gh) + _dot(gb, gl)
            for ee in range(N_EXPERTS):
                seg_copy(w, ee, True)

    start = est_sm[e]
    tot = tot_sm[e]

    def row_tile(r0, size, valid):
        r0 = pl.multiple_of(r0, SEG)
        xt = xs[pl.ds(r0, size), :]
        y = _swiglu(xt, wg_ref, wu_ref, wd_ref)
        lane = lax.broadcasted_iota(jnp.int32, (size, LANES), 1)
        gate = jnp.sum(jnp.where(lane == e, gs[pl.ds(r0, size), :], 0.0), axis=1, keepdims=True)
        keep = lax.broadcasted_iota(jnp.int32, (size, D_MODEL), 0) < valid
        xs[pl.ds(r0, size), :] = jnp.where(keep, (y * gate).astype(BF16), xt)

    n_full = tot // ROW_TILE

    def full_body(i, carry):
        row_tile(start + i * ROW_TILE, ROW_TILE, ROW_TILE)
        return carry

    lax.fori_loop(0, n_full, full_body, 0)
    rem = tot - n_full * ROW_TILE
    tail = start + n_full * ROW_TILE

    @pl.when(rem > ROW_TILE // 2)
    def _():
        row_tile(tail, ROW_TILE, rem)

    @pl.when((rem > 0) & (rem <= ROW_TILE // 2))
    def _():
        row_tile(tail, ROW_TILE // 2, rem)

    @pl.when(e == N_EXPERTS - 1)
    def _combine():
        liota = lax.broadcasted_iota(jnp.int32, (MOE_WIN, SORT_ROWS), 1).astype(F32)
        for w in range(n_win):
            rs = slice(w * MOE_WIN, (w + 1) * MOE_WIN)
            for ee in range(N_EXPERTS):
                seg_copy(w, ee, False)
            g = jnp.zeros((MOE_WIN, SORT_ROWS), F32)
            for ee in range(N_EXPERTS):
                dest = pm_ref[rs, ee:ee + 1] + s_sm[w * N_EXPERTS + ee].astype(F32)
                g = jnp.where(liota == dest, 1.0, g)
            o_ref[rs, :] = _dot(g.astype(BF16), sb[...])


def _moe(cnt, hn, gates, pm, pmt, wg, wu, wd, *, l, tb):
    m = hn.shape[0]
    n_win = tb // MOE_WIN
    j = l // 2
    xs_rows = 2 * tb + n_win * N_EXPERTS * SEG + ROW_TILE
    one = pl.Buffered(1)
    wsel = lambda i, e, c: (j * N_EXPERTS + e, 0, 0)
    grid_spec = pltpu.PrefetchScalarGridSpec(
        num_scalar_prefetch=1,
        grid=(m // tb, N_EXPERTS),
        in_specs=[
            pl.BlockSpec((tb, D_MODEL), lambda i, e, c: (i, 0), pipeline_mode=one),
            pl.BlockSpec((tb, LANES), lambda i, e, c: (i, 0), pipeline_mode=one),
            pl.BlockSpec((tb, LANES), lambda i, e, c: (i, 0), pipeline_mode=one),
            pl.BlockSpec((n_win, SUBLANES, MOE_WIN), lambda i, e, c: (i, 0, 0), pipeline_mode=one),
            pl.BlockSpec((None, D_MODEL, FF_PAD), wsel),
            pl.BlockSpec((None, D_MODEL, FF_PAD), wsel),
            pl.BlockSpec((None, FF_PAD, D_MODEL), wsel),
        ],
        out_specs=pl.BlockSpec((tb, D_MODEL), lambda i, e, c: (i, 0), pipeline_mode=one),
        scratch_shapes=[
            pltpu.VMEM((xs_rows, D_MODEL), BF16),
            pltpu.VMEM((xs_rows, LANES), F32),
            pltpu.VMEM((SORT_ROWS, D_MODEL), BF16),
            pltpu.VMEM((SORT_ROWS, LANES), F32),
            pltpu.SMEM((n_win * N_EXPERTS,), jnp.int32),
            pltpu.SMEM((n_win * N_EXPERTS,), jnp.int32),
            pltpu.SMEM((n_win * N_EXPERTS,), jnp.int32),
            pltpu.SMEM((N_EXPERTS,), jnp.int32),
            pltpu.SMEM((N_EXPERTS,), jnp.int32),
        ],
    )
    return pl.pallas_call(
        functools.partial(_moe_kernel, n_win=n_win),
        grid_spec=grid_spec,
        out_shape=jax.ShapeDtypeStruct((m, D_MODEL), F32),
        compiler_params=pltpu.CompilerParams(
            dimension_semantics=("arbitrary", "arbitrary"), vmem_limit_bytes=MOE_VMEM_LIMIT),
        name="moe",
    )(cnt, hn, gates, pm, pmt, wg, wu, wd)


def _final_norm_kernel(*refs, n_res):
    g_ref, o_ref = refs[n_res:]
    x = refs[0][...]
    for r in refs[1:n_res]:
        x = x + r[...]
    o_ref[...] = _rms(x, g_ref[...])


def _final_norm(xs, g, tm):
    m = xs[0].shape[0]
    return pl.pallas_call(
        functools.partial(_final_norm_kernel, n_res=len(xs)),
        grid=(m // tm,),
        in_specs=[pl.BlockSpec((tm, D_MODEL), lambda i: (i, 0)) for _ in xs]
        + [pl.BlockSpec((1, D_MODEL), lambda i: (0, 0))],
        out_specs=pl.BlockSpec((tm, D_MODEL), lambda i: (i, 0)),
        out_shape=jax.ShapeDtypeStruct((m, D_MODEL), F32),
        compiler_params=pltpu.CompilerParams(
            dimension_semantics=("parallel",), vmem_limit_bytes=VMEM_LIMIT),
        name="final_norm",
    )(*xs, g)


def kernel(x_prompt, x_sample, state_ssm_re, state_ssm_im, norm_mix, w_in, ssm_a_re, ssm_a_im, ssm_log_dt, ssm_b_re, ssm_b_im, ssm_c_re, ssm_c_im, ssm_d, glu_w, glu_b, sgu_w, sgu_b, sgu_ln_g, sgu_ln_b, out_norm_a, out_norm_b, w_out, norm_ffn, ffn_w_gate, ffn_w_up, ffn_w_down, router_w, router_b, moe_w_gate, moe_w_up, moe_w_down, norm_final):
    batch, seq = x_prompt.shape[0], x_prompt.shape[1]
    dec_batch, dec_seq = x_sample.shape[0], x_sample.shape[1]
    mp, ms = batch * seq, dec_batch * dec_seq
    n_moe = moe_w_gate.shape[0]

    lam, bbre, bbim = _s5_prep(ssm_a_re, ssm_a_im, ssm_log_dt, ssm_b_re, ssm_b_im)
    ctre = jnp.swapaxes(ssm_c_re, 2, 3).reshape(DEPTH * N_STATE, SSM_GROUP)
    ctim = jnp.swapaxes(ssm_c_im, 2, 3).reshape(DEPTH * N_STATE, SSM_GROUP)
    dskip = ssm_d.reshape(DEPTH, 1, MIX_A)
    glu = glu_w.reshape(DEPTH * MIX_A, 2 * SSM_GROUP)
    gbias = jnp.stack([glu_b[..., :SSM_GROUP].reshape(DEPTH, MIX_A),
                       glu_b[..., SSM_GROUP:].reshape(DEPTH, MIX_A)], axis=1)
    s5w = (lam, bbre, bbim, ctre, ctim, dskip, glu, gbias)
    w_in_b = w_in.astype(BF16)
    w_out_b = w_out.astype(BF16)
    g_mix = norm_mix.reshape(DEPTH, 1, D_MODEL)
    g_a = out_norm_a.reshape(DEPTH, 1, MIX_A)
    g_b = out_norm_b.reshape(DEPTH, 1, MIX_B)
    g_ffn = norm_ffn.reshape(DEPTH, 1, D_MODEL)
    sgu_bias = jnp.repeat(jnp.swapaxes(sgu_b, 1, 2), HD_B, axis=2)
    sgu_bias_s = sgu_bias[:, :dec_seq]
    lng = sgu_ln_g.reshape(DEPTH, 1, MIX_B)
    lnb = sgu_ln_b.reshape(DEPTH, 1, MIX_B)
    coef_s = jnp.repeat(
        jnp.transpose(sgu_w[:, :, :dec_seq, :dec_seq], (0, 2, 3, 1)).reshape(DEPTH, dec_seq * dec_seq, NH_B),
        HD_B, axis=2)
    dense_split = (FF_PAD, D_FF - FF_PAD)
    dense_g = _pack(jnp.swapaxes(ffn_w_gate, 1, 2), dense_split, transpose=True)
    dense_u = _pack(jnp.swapaxes(ffn_w_up, 1, 2), dense_split, transpose=True)
    dense_d = _pack(ffn_w_down, dense_split, transpose=False)
    moe_t = lambda w: jnp.swapaxes(w.reshape(-1, D_MODEL, D_FF_EXPERT), 1, 2)
    moe_g = _pack(moe_t(moe_w_gate), (D_FF_EXPERT,), transpose=True)
    moe_u = _pack(moe_t(moe_w_up), (D_FF_EXPERT,), transpose=True)
    moe_d = _pack(moe_w_down.reshape(-1, D_FF_EXPERT, D_MODEL), (D_FF_EXPERT,), transpose=False)
    rw = jnp.pad(router_w, ((0, 0), (0, 0), (0, LANES - N_EXPERTS)))
    rwh = rw.astype(BF16)
    rwl = (rw - rwh.astype(F32)).astype(BF16)
    rb = jnp.pad(router_b, ((0, 0), (0, LANES - N_EXPERTS))).reshape(n_moe, 1, LANES)
    gfin = norm_final.reshape(1, D_MODEL)

    h0re = state_ssm_re.reshape(DEPTH, dec_batch, N_STATE)
    h0im = state_ssm_im.reshape(DEPTH, dec_batch, N_STATE)

    def post_mixer(xres, ya, yb, l, tm, tb):
        common = (g_a, g_b, w_out_b, g_ffn)
        if l % 2 == 0:
            return (_ffn(xres, ya, yb, *common, dense_g, dense_u, dense_d, l=l, tm=tm),)
        x1, hn, gates, pm, pmt, cnt = _pre_moe(xres, ya, yb, *common, rwh, rwl, rb, l=l, tm=min(tm, 512))
        cnt = cnt[:, 0, :N_EXPERTS].astype(jnp.int32).reshape(-1)
        return (x1, _moe(cnt, hn, gates, pm, pmt, moe_g, moe_u, moe_d, l=l, tb=tb))

    xp = (x_prompt.reshape(mp, D_MODEL),)
    xs = (x_sample.reshape(ms, D_MODEL),)
    re_p, im_p, re_s, im_s, v_s = [], [], [], [], []
    for l in range(DEPTH):
        ua_p, yb_p = _inproj_sgu(xp, g_mix, w_in_b, sgu_w, sgu_bias, lng, lnb, l, n_chunks=8)
        zs = _inproj(xs, g_mix, w_in_b, l, tm=ms)
        ya_p, hre_p, him_p = _s5_prompt(ua_p.reshape(batch, seq, MIX_A), s5w, l, t_blk=64)
        ya_s, hre_s, him_s = _s5_sample(zs, h0re, h0im, s5w, l, nb=dec_batch, t_blk=dec_seq)
        yb_s, vn_s = _sgu_sample(zs, coef_s, sgu_bias_s, lng, lnb, l, nb=dec_batch, t_blk=dec_seq)
        xp = post_mixer(xp, ya_p.reshape(mp, MIX_A), yb_p, l, tm=512, tb=seq)
        xs = post_mixer(xs, ya_s, yb_s, l, tm=ms, tb=ms)
        re_p.append(hre_p)
        im_p.append(him_p)
        re_s.append(hre_s)
        im_s.append(him_s)
        v_s.append(vn_s)
    xp = _final_norm(xp, gfin, tm=1024)
    xs = _final_norm(xs, gfin, tm=ms)

    st = lambda hs, b: jnp.stack(hs).reshape(DEPTH, b, NG_A, SSM_STATE)
    return (xp.reshape(batch, seq, D_MODEL), xs.reshape(dec_batch, dec_seq, D_MODEL),
            st(re_p, batch), st(im_p, batch), st(re_s, dec_batch), st(im_s, dec_batch),
            jnp.stack(v_s).reshape(DEPTH, dec_batch, dec_seq, NH_B, HD_B))
```

```python
import functools
import math

import jax
import jax.numpy as jnp
from jax import lax
from jax.experimental import pallas as pl
from jax.experimental.pallas import tpu as pltpu

F32 = jnp.float32
BF16 = jnp.bfloat16

D_MODEL = 1024
DEPTH = 4
MIX_A = 512
SSM_GROUP = 16
NG_A = 32
SSM_STATE = 64
N_STATE = NG_A * SSM_STATE
MIX_B = 512
HD_B = 64
NH_B = 8
CHUNK = 128
IN_WIDTH = MIX_A + 2 * MIX_B
D_FF = 2752
N_EXPERTS = 8
D_FF_EXPERT = D_FF // 2
FF_PAD = 1408
EPS = 1e-6

LANES = 128
SUBLANES = 8
MXU_DIM = 256
STATE_COLS = 512
N_SCOL = N_STATE // STATE_COLS
GROUPS_PER_CHUNK = STATE_COLS // SSM_STATE
VMEM_LIMIT = 52 * 1024 * 1024
MOE_VMEM_LIMIT = 62 * 1024 * 1024


def _rms(x, g):
    return x * lax.rsqrt(jnp.mean(x * x, axis=-1, keepdims=True) + EPS) * g


def _gelu(x):
    c = math.sqrt(2.0 / math.pi)
    return 0.5 * x * (1.0 + jnp.tanh(c * (x + 0.044715 * (x * x * x))))


def _sigmoid(x):
    return 1.0 / (1.0 + jnp.exp(-x))


def _split_bf16(x):
    hi = x.astype(BF16)
    lo = (x - hi.astype(F32)).astype(BF16)
    return hi, lo


def _dot(a, b):
    return jnp.dot(a, b, preferred_element_type=F32)


def _log2(n):
    assert n & (n - 1) == 0
    return n.bit_length() - 1


def _block_diag_tile(src, n_rep, rows_per_group):
    r_n, k = src.shape
    n = k * n_rep
    t_row = lax.broadcasted_iota(jnp.int32, (k, n), 0)
    t_col = lax.broadcasted_iota(jnp.int32, (k, n), 1)
    tile = jnp.where(jnp.bitwise_and(t_col, k - 1) == t_row, 1.0, 0.0).astype(BF16)
    full = _dot(src.astype(BF16), tile)
    row = lax.broadcasted_iota(jnp.int32, (r_n, n), 0)
    col = lax.broadcasted_iota(jnp.int32, (r_n, n), 1)
    same = lax.shift_right_logical(row, _log2(rows_per_group)) == lax.shift_right_logical(col, _log2(k))
    return jnp.where(same, full, 0.0).astype(BF16)


def _pack_kernel(w_ref, o_ref, *, widths, transpose):
    for i, width in enumerate(widths):
        w = w_ref[i * FF_PAD:i * FF_PAD + width, :]
        if width < FF_PAD:
            w = jnp.concatenate([w, jnp.zeros((FF_PAD - width, w.shape[1]), F32)], axis=0)
        o_ref[i] = (w.T if transpose else w).astype(BF16)


def _pack(w, widths, transpose):
    n_e, n_parts = w.shape[0], len(widths)
    out_blk = (n_parts, D_MODEL, FF_PAD) if transpose else (n_parts, FF_PAD, D_MODEL)
    out = pl.pallas_call(
        functools.partial(_pack_kernel, widths=widths, transpose=transpose),
        grid=(n_e,),
        in_specs=[pl.BlockSpec((None,) + w.shape[1:], lambda e: (e, 0, 0))],
        out_specs=pl.BlockSpec((None,) + out_blk, lambda e: (e, 0, 0, 0)),
        out_shape=jax.ShapeDtypeStruct((n_e,) + out_blk, BF16),
        compiler_params=pltpu.CompilerParams(
            dimension_semantics=("parallel",), vmem_limit_bytes=VMEM_LIMIT),
        name="pack_t" if transpose else "pack",
    )(w)
    return out.reshape((n_e * n_parts,) + out_blk[1:])


def _s5_prep_kernel(are_ref, aim_ref, ldt_ref, bre_ref, bim_ref,
                    abre_ref, abim_ref, bbre_ref, bbim_ref):
    lam_re = are_ref[...]
    lam_im = aim_ref[...]
    dt = jnp.exp(ldt_ref[...])
    mag = jnp.exp(lam_re * dt)
    ab_re = mag * jnp.cos(lam_im * dt)
    ab_im = mag * jnp.sin(lam_im * dt)
    den = lam_re * lam_re + lam_im * lam_im
    nr = ab_re - 1.0
    q_re = (nr * lam_re + ab_im * lam_im) / den
    q_im = (ab_im * lam_re - nr * lam_im) / den
    b_re = bre_ref[...]
    b_im = bim_ref[...]
    abre_ref[...] = ab_re
    abim_ref[...] = ab_im
    bbre_ref[...] = q_re * b_re - q_im * b_im
    bbim_ref[...] = q_re * b_im + q_im * b_re


def _s5_prep(ssm_a_re, ssm_a_im, ssm_log_dt, ssm_b_re, ssm_b_im):
    shp = (DEPTH, NG_A, SSM_GROUP, SSM_STATE)
    rows = DEPTH * NG_A * SSM_GROUP
    bc = lambda a: jnp.broadcast_to(a, shp).reshape(rows, SSM_STATE)
    are = bc(ssm_a_re[:, :, None, :])
    aim = bc(ssm_a_im[:, :, None, :])
    ldt = bc(ssm_log_dt[:, :, None, None])
    bre = jnp.swapaxes(ssm_b_re, 2, 3).reshape(rows, SSM_STATE)
    bim = jnp.swapaxes(ssm_b_im, 2, 3).reshape(rows, SSM_STATE)
    sds = jax.ShapeDtypeStruct((rows, SSM_STATE), F32)
    abre, abim, bbre, bbim = pl.pallas_call(
        _s5_prep_kernel, out_shape=(sds, sds, sds, sds), name="s5_prep",
    )(are, aim, ldt, bre, bim)
    lam_re = abre.reshape(shp)[:, :, 0, :].reshape(DEPTH, 1, N_STATE)
    lam_im = abim.reshape(shp)[:, :, 0, :].reshape(DEPTH, 1, N_STATE)
    lam = jnp.concatenate([lam_re, lam_im], axis=1)
    return lam, bbre, bbim


def _inproj_kernel(*refs, n_res):
    g_ref, w_ref, z_ref = refs[n_res:]
    x = refs[0][...]
    for r in refs[1:n_res]:
        x = x + r[...]
    hn = _rms(x, g_ref[...])
    z_ref[...] = _dot(hn.astype(BF16), w_ref[...])


def _inproj(xs, g, w, l, tm):
    m = xs[0].shape[0]
    return pl.pallas_call(
        functools.partial(_inproj_kernel, n_res=len(xs)),
        grid=(m // tm,),
        in_specs=[pl.BlockSpec((tm, D_MODEL), lambda i: (i, 0)) for _ in xs] + [
            pl.BlockSpec((None, 1, D_MODEL), lambda i: (l, 0, 0)),
            pl.BlockSpec((None, D_MODEL, IN_WIDTH), lambda i: (l, 0, 0)),
        ],
        out_specs=pl.BlockSpec((tm, IN_WIDTH), lambda i: (i, 0)),
        out_shape=jax.ShapeDtypeStruct((m, IN_WIDTH), F32),
        compiler_params=pltpu.CompilerParams(
            dimension_semantics=("parallel",), vmem_limit_bytes=VMEM_LIMIT),
        name="inproj",
    )(*xs, g, w)


def _s5_expand(bbre_ref, bbim_ref, ctre_ref, ctim_ref, glu_ref, bblk, cre, cimn, wa, wg):
    for c in range(4):
        rs = slice(c * LANES, (c + 1) * LANES)
        bblk[c, :, :STATE_COLS] = _block_diag_tile(bbre_ref[rs, :], GROUPS_PER_CHUNK, SSM_GROUP)
        bblk[c, :, STATE_COLS:] = _block_diag_tile(bbim_ref[rs, :], GROUPS_PER_CHUNK, SSM_GROUP)
        ss = slice(c * STATE_COLS, (c + 1) * STATE_COLS)
        cre[c] = _block_diag_tile(ctre_ref[ss, :], GROUPS_PER_CHUNK, SSM_STATE)
        cimn[c] = _block_diag_tile(-ctim_ref[ss, :], GROUPS_PER_CHUNK, SSM_STATE)
    for b in range(MIX_A // MXU_DIM):
        ms = slice(b * MXU_DIM, (b + 1) * MXU_DIM)
        wa[b] = _block_diag_tile(glu_ref[ms, :SSM_GROUP], MXU_DIM // SSM_GROUP, SSM_GROUP)
        wg[b] = _block_diag_tile(glu_ref[ms, SSM_GROUP:], MXU_DIM // SSM_GROUP, SSM_GROUP)


def _s5_time_major(utm, lam_ref, d_ref, gbias_ref, hre_ref, him_ref, bre, bim, bblk, cre, cimn, wa, wg,
                   nb, t_blk):
    for c in range(4):
        r = _dot(utm[c].astype(BF16), bblk[c])
        bre[:, c * STATE_COLS:(c + 1) * STATE_COLS] = r[:, :STATE_COLS]
        bim[:, c * STATE_COLS:(c + 1) * STATE_COLS] = r[:, STATE_COLS:]

    for c in range(N_SCOL):
        cs = slice(c * STATE_COLS, (c + 1) * STATE_COLS)
        lr = jnp.broadcast_to(lam_ref[0:1, cs], (SUBLANES, STATE_COLS))
        li = jnp.broadcast_to(lam_ref[1:2, cs], (SUBLANES, STATE_COLS))

        if nb == SUBLANES:
            hr, hi = hre_ref[:, cs], him_ref[:, cs]
            for t in range(t_blk):
                rs = slice(t * SUBLANES, (t + 1) * SUBLANES)
                hr, hi = lr * hr - li * hi + bre[rs, cs], lr * hi + li * hr + bim[rs, cs]
                bre[rs, cs] = hr
                bim[rs, cs] = hi
            hre_ref[:, cs] = hr
            him_ref[:, cs] = hi
            continue

        def group_body(bg, carry, cs=cs, lr=lr, li=li):
            r0 = pl.multiple_of(bg * SUBLANES, SUBLANES)

            def t_body(t, h):
                hr, hi = h
                row = pl.multiple_of(t * nb + r0, SUBLANES)
                nr = lr * hr - li * hi + bre[pl.ds(row, SUBLANES), cs]
                ni = lr * hi + li * hr + bim[pl.ds(row, SUBLANES), cs]
                bre[pl.ds(row, SUBLANES), cs] = nr
                bim[pl.ds(row, SUBLANES), cs] = ni
                return nr, ni

            h0 = (hre_ref[pl.ds(r0, SUBLANES), cs], him_ref[pl.ds(r0, SUBLANES), cs])
            hr, hi = lax.fori_loop(0, t_blk, t_body, h0, unroll=min(t_blk, 8))
            hre_ref[pl.ds(r0, SUBLANES), cs] = hr
            him_ref[pl.ds(r0, SUBLANES), cs] = hi
            return carry

        lax.fori_loop(0, nb // SUBLANES, group_body, 0)

    ys = []
    for c in range(4):
        cs = slice(c * STATE_COLS, (c + 1) * STATE_COLS)
        ls = slice(c * LANES, (c + 1) * LANES)
        y = _dot(bre[:, cs].astype(BF16), cre[c]) + _dot(bim[:, cs].astype(BF16), cimn[c])
        ys.append(_gelu(y + d_ref[:, ls] * utm[c]).astype(BF16))

    os_ = []
    for b in range(MIX_A // MXU_DIM):
        ms = slice(b * MXU_DIM, (b + 1) * MXU_DIM)
        yb = jnp.concatenate(ys[2 * b:2 * b + 2], axis=1)
        za = _dot(yb, wa[b]) + gbias_ref[0:1, ms]
        zg = _dot(yb, wg[b]) + gbias_ref[1:2, ms]
        os_.append(za * _sigmoid(zg))
    return jnp.concatenate(os_, axis=1)


def _s5_kernel(*refs, nb, t_blk):
    (u_ref, h0re_ref, h0im_ref, bbre_ref, bbim_ref, lam_ref, ctre_ref, ctim_ref, d_ref, glu_ref, gbias_ref,
     ya_ref, hre_ref, him_ref, utm, bre, bim, otm, bblk, cre, cimn, wa, wg, tmp) = refs
    hre_ref[...] = h0re_ref[...]
    him_ref[...] = h0im_ref[...]
    _s5_expand(bbre_ref, bbim_ref, ctre_ref, ctim_ref, glu_ref, bblk, cre, cimn, wa, wg)

    for c in range(4):
        ls = slice(c * LANES, (c + 1) * LANES)
        tmp[c] = u_ref[:, ls]
        for t in range(t_blk):
            utm[c, t * nb:(t + 1) * nb, :] = tmp[c, pl.ds(t, nb, stride=t_blk), :]

    o = _s5_time_major(utm, lam_ref, d_ref, gbias_ref, hre_ref, him_ref, bre, bim, bblk, cre, cimn, wa, wg,
                       nb, t_blk)

    for c in range(4):
        ls = slice(c * LANES, (c + 1) * LANES)
        for t in range(t_blk):
            otm[c, pl.ds(t, nb, stride=t_blk), :] = o[t * nb:(t + 1) * nb, ls]
        ya_ref[:, ls] = otm[c]


def _s5_weight_specs(l):
    return [
        pl.BlockSpec((NG_A * SSM_GROUP, SSM_STATE), lambda j: (l, 0)),
        pl.BlockSpec((NG_A * SSM_GROUP, SSM_STATE), lambda j: (l, 0)),
        pl.BlockSpec((None, 2, N_STATE), lambda j: (l, 0, 0)),
        pl.BlockSpec((N_STATE, SSM_GROUP), lambda j: (l, 0)),
        pl.BlockSpec((N_STATE, SSM_GROUP), lambda j: (l, 0)),
        pl.BlockSpec((None, 1, MIX_A), lambda j: (l, 0, 0)),
        pl.BlockSpec((MIX_A, 2 * SSM_GROUP), lambda j: (l, 0)),
        pl.BlockSpec((None, 2, MIX_A), lambda j: (l, 0, 0)),
    ]


def _s5_scratch(rows):
    return [
        pltpu.VMEM((4, rows, LANES), F32),
        pltpu.VMEM((rows, N_STATE), F32),
        pltpu.VMEM((rows, N_STATE), F32),
        pltpu.VMEM((4, rows, LANES), F32),
        pltpu.VMEM((4, LANES, 2 * STATE_COLS), BF16),
        pltpu.VMEM((4, STATE_COLS, LANES), BF16),
        pltpu.VMEM((4, STATE_COLS, LANES), BF16),
        pltpu.VMEM((MIX_A // MXU_DIM, MXU_DIM, MXU_DIM), BF16),
        pltpu.VMEM((MIX_A // MXU_DIM, MXU_DIM, MXU_DIM), BF16),
    ]


def _s5_sample(z, h0re, h0im, weights, l, nb, t_blk):
    rows = nb * t_blk
    lam, bbre, bbim, ctre, ctim, dskip, glu, gbias = weights
    blk = pl.BlockSpec((rows, MIX_A), lambda j: (0, 0))
    st_in = pl.BlockSpec((None, nb, N_STATE), lambda j: (l, 0, 0))
    st_out = pl.BlockSpec((nb, N_STATE), lambda j: (0, 0))
    return pl.pallas_call(
        functools.partial(_s5_kernel, nb=nb, t_blk=t_blk),
        grid=(1,),
        in_specs=[blk, st_in, st_in] + _s5_weight_specs(l),
        out_specs=[blk, st_out, st_out],
        out_shape=[
            jax.ShapeDtypeStruct((rows, MIX_A), F32),
            jax.ShapeDtypeStruct((nb, N_STATE), F32),
            jax.ShapeDtypeStruct((nb, N_STATE), F32),
        ],
        scratch_shapes=_s5_scratch(rows) + [pltpu.VMEM((4, rows, LANES), F32)],
        compiler_params=pltpu.CompilerParams(
            dimension_semantics=("arbitrary",), vmem_limit_bytes=VMEM_LIMIT),
        name="s5_sample",
    )(z, h0re, h0im, bbre, bbim, lam, ctre, ctim, dskip, glu, gbias)


def _group_layernorm(gv, lng_ref, lnb_ref):
    row = lax.broadcasted_iota(jnp.int32, (MXU_DIM, MXU_DIM), 0)
    col = lax.broadcasted_iota(jnp.int32, (MXU_DIM, MXU_DIM), 1)
    same = lax.shift_right_logical(row, _log2(HD_B)) == lax.shift_right_logical(col, _log2(HD_B))
    gm = jnp.where(same, 1.0 / HD_B, 0.0).astype(BF16)

    def gmean(a):
        hi, lo = _split_bf16(a)
        parts = []
        for b in range(MIX_B // MXU_DIM):
            ms = slice(b * MXU_DIM, (b + 1) * MXU_DIM)
            parts.append(_dot(hi[:, ms], gm) + _dot(lo[:, ms], gm))
        return jnp.concatenate(parts, axis=1)

    xc = gv - gmean(gv)
    var = gmean(xc * xc)
    return xc * lax.rsqrt(var + EPS) * lng_ref[...] + lnb_ref[...]


def _mixer_kernel(x_ref, g_ref, w_ref, ws_ref, bias_ref, lng_ref, lnb_ref,
                  bbre_ref, bbim_ref, lam_ref, ctre_ref, ctim_ref, d_ref, glu_ref, gbias_ref,
                  ya_ref, yb_ref, hre_ref, him_ref,
                  hnb, ua, zb, utm, bre, bim, otm, bblk, cre, cimn, wa, wg, *, nb):
    @pl.when(pl.program_id(0) == 0)
    def _():
        hre_ref[...] = jnp.zeros_like(hre_ref)
        him_ref[...] = jnp.zeros_like(him_ref)
        _s5_expand(bbre_ref, bbim_ref, ctre_ref, ctim_ref, glu_ref, bblk, cre, cimn, wa, wg)

    for b in range(nb):
        hnb[b * CHUNK:(b + 1) * CHUNK, :] = _rms(x_ref[b], g_ref[...]).astype(BF16)
    ua[...] = _dot(hnb[...], w_ref[:, :MIX_A])
    zb[...] = _dot(hnb[...], w_ref[:, MIX_A:])

    row = lax.broadcasted_iota(jnp.int32, (CHUNK, 2 * CHUNK), 0)
    col = lax.broadcasted_iota(jnp.int32, (CHUNK, 2 * CHUNK), 1)
    causal = jnp.bitwise_and(col, CHUNK - 1) <= row
    lane = lax.broadcasted_iota(jnp.int32, (CHUNK, LANES), 1)
    first_head = lane < HD_B
    wcat = []
    for p in range(NH_B // 2):
        w = jnp.concatenate([ws_ref[2 * p], ws_ref[2 * p + 1]], axis=1)
        wcat.append(jnp.where(causal, w, 0.0).astype(BF16))
    for b in range(nb):
        rs = slice(b * CHUNK, (b + 1) * CHUNK)
        u = _gelu(zb[rs, :MIX_B])
        vn = _group_layernorm(_gelu(zb[rs, MIX_B:]), lng_ref, lnb_ref)
        for p in range(NH_B // 2):
            ls = slice(p * LANES, (p + 1) * LANES)
            vp = vn[:, ls]
            rhs = jnp.concatenate(
                [jnp.where(first_head, vp, 0.0), jnp.where(first_head, 0.0, vp)], axis=0).astype(BF16)
            s = _dot(wcat[p], rhs)
            yb_ref[b, :, ls] = u[:, ls] * (s + bias_ref[:, ls])

    for c in range(4):
        ls = slice(c * LANES, (c + 1) * LANES)
        for b in range(nb):
            utm[c, pl.ds(b, CHUNK, stride=nb), :] = ua[b * CHUNK:(b + 1) * CHUNK, ls]
    o = _s5_time_major(utm, lam_ref, d_ref, gbias_ref, hre_ref, him_ref, bre, bim, bblk, cre, cimn, wa, wg,
                       nb, CHUNK)
    for c in range(4):
        ls = slice(c * LANES, (c + 1) * LANES)
        otm[c] = o[:, ls]
        for b in range(nb):
            ya_ref[b, :, ls] = otm[c, pl.ds(b, CHUNK, stride=nb), :]


def _mixer(x3, g, w, ws, bias, lng, lnb, s5w, l):
    nb, seq = x3.shape[0], x3.shape[1]
    rows = nb * CHUNK
    lam, bbre, bbim, ctre, ctim, dskip, glu, gbias = s5w
    lay3 = lambda j: (l, 0, 0)
    blk = lambda width: pl.BlockSpec((nb, CHUNK, width), lambda j: (0, j, 0))
    st = pl.BlockSpec((nb, N_STATE), lambda j: (0, 0))
    return pl.pallas_call(
        functools.partial(_mixer_kernel, nb=nb),
        grid=(seq // CHUNK,),
        in_specs=[
            blk(D_MODEL),
            pl.BlockSpec((None, 1, D_MODEL), lay3),
            pl.BlockSpec((None, D_MODEL, IN_WIDTH), lay3, pipeline_mode=pl.Buffered(1)),
            pl.BlockSpec((None, NH_B, CHUNK, CHUNK), lambda j: (l, 0, 0, 0)),
            pl.BlockSpec((None, CHUNK, MIX_B), lay3),
            pl.BlockSpec((None, 1, MIX_B), lay3),
            pl.BlockSpec((None, 1, MIX_B), lay3),
        ] + _s5_weight_specs(l),
        out_specs=[blk(MIX_A), blk(MIX_B), st, st],
        out_shape=[
            jax.ShapeDtypeStruct((nb, seq, MIX_A), F32),
            jax.ShapeDtypeStruct((nb, seq, MIX_B), F32),
            jax.ShapeDtypeStruct((nb, N_STATE), F32),
            jax.ShapeDtypeStruct((nb, N_STATE), F32),
        ],
        scratch_shapes=[
            pltpu.VMEM((rows, D_MODEL), BF16),
            pltpu.VMEM((rows, MIX_A), F32),
            pltpu.VMEM((rows, 2 * MIX_B), F32),
        ] + _s5_scratch(rows),
        compiler_params=pltpu.CompilerParams(
            dimension_semantics=("arbitrary",), vmem_limit_bytes=MOE_VMEM_LIMIT),
        name="mixer",
    )(x3, g, w, ws, bias, lng, lnb, bbre, bbim, lam, ctre, ctim, dskip, glu, gbias)


def _sgu_sample_kernel(u_ref, v_ref, coef_ref, bias_ref, lng_ref, lnb_ref, o_ref, vn_ref,
                       usc, vsc, osc, *, nb, t_blk):
    vn = _group_layernorm(_gelu(v_ref[...]), lng_ref, lnb_ref)
    vn_ref[...] = vn
    u = _gelu(u_ref[...])
    for c in range(4):
        ls = slice(c * LANES, (c + 1) * LANES)
        vsc[c] = vn[:, ls]
        usc[c] = u[:, ls]
        for q in range(t_blk):
            s = bias_ref[q:q + 1, ls]
            for k in range(q + 1):
                s = s + coef_ref[q * t_blk + k:q * t_blk + k + 1, ls] * vsc[c, pl.ds(k, nb, stride=t_blk), :]
            osc[c, pl.ds(q, nb, stride=t_blk), :] = usc[c, pl.ds(q, nb, stride=t_blk), :] * s
        o_ref[:, ls] = osc[c]


def _sgu_sample(z, coef, bias, lng, lnb, l, nb, t_blk):
    m = nb * t_blk
    fix2 = lambda i: (0, 0)
    lay3 = lambda i: (l, 0, 0)
    return pl.pallas_call(
        functools.partial(_sgu_sample_kernel, nb=nb, t_blk=t_blk),
        grid=(1,),
        in_specs=[
            pl.BlockSpec((m, MIX_B), lambda i: (0, 1)),
            pl.BlockSpec((m, MIX_B), lambda i: (0, 2)),
            pl.BlockSpec((None, t_blk * t_blk, MIX_B), lay3),
            pl.BlockSpec((None, t_blk, MIX_B), lay3),
            pl.BlockSpec((None, 1, MIX_B), lay3),
            pl.BlockSpec((None, 1, MIX_B), lay3),
        ],
        out_specs=[pl.BlockSpec((m, MIX_B), fix2), pl.BlockSpec((m, MIX_B), fix2)],
        out_shape=[jax.ShapeDtypeStruct((m, MIX_B), F32), jax.ShapeDtypeStruct((m, MIX_B), F32)],
        scratch_shapes=[pltpu.VMEM((4, m, LANES), F32)] * 3,
        compiler_params=pltpu.CompilerParams(
            dimension_semantics=("arbitrary",), vmem_limit_bytes=VMEM_LIMIT),
        name="sgu_sample",
    )(z, z, coef, bias, lng, lnb)


def _swiglu(x, wg_ref, wu_ref, wd_ref):
    hg = _dot(x, wg_ref[...])
    a = (hg * _sigmoid(hg) * _dot(x, wu_ref[...])).astype(BF16)
    return _dot(a, wd_ref[...])


def _out_proj(res_refs, ya_ref, yb_ref, ga_ref, gb_ref, wout_ref, rs):
    x = res_refs[0][rs, :]
    for r in res_refs[1:]:
        x = x + r[rs, :]
    na = _rms(ya_ref[rs, :], ga_ref[...]).astype(BF16)
    nb = _rms(yb_ref[rs, :], gb_ref[...]).astype(BF16)
    return x + _dot(na, wout_ref[0:MIX_A, :]) + _dot(nb, wout_ref[MIX_A:, :])


N_DENSE_PARTS = 2


def _ffn_kernel(*refs, n_res):
    (ya_ref, yb_ref, ga_ref, gb_ref, wout_ref, gf_ref, wg_ref, wu_ref, wd_ref, o_ref) = refs[n_res:]
    x1 = _out_proj(refs[:n_res], ya_ref, yb_ref, ga_ref, gb_ref, wout_ref, slice(None))
    hn = _rms(x1, gf_ref[...]).astype(BF16)
    o_ref[...] = x1
    for p in range(N_DENSE_PARTS):
        o_ref[...] += _swiglu(hn, wg_ref.at[p], wu_ref.at[p], wd_ref.at[p])


def _ffn(xs, ya, yb, ga, gb, wout, gf, wg, wu, wd, *, l, tm):
    m = xs[0].shape[0]
    j = l // 2
    one = pl.Buffered(1)
    lay3 = lambda i: (l, 0, 0)
    wsel = lambda i: (j, 0, 0)
    return pl.pallas_call(
        functools.partial(_ffn_kernel, n_res=len(xs)),
        grid=(m // tm,),
        in_specs=[pl.BlockSpec((tm, D_MODEL), lambda i: (i, 0)) for _ in xs] + [
            pl.BlockSpec((tm, MIX_A), lambda i: (i, 0)),
            pl.BlockSpec((tm, MIX_B), lambda i: (i, 0)),
            pl.BlockSpec((None, 1, MIX_A), lay3),
            pl.BlockSpec((None, 1, MIX_B), lay3),
            pl.BlockSpec((None, D_MODEL, D_MODEL), lay3, pipeline_mode=one),
            pl.BlockSpec((None, 1, D_MODEL), lay3),
            pl.BlockSpec((N_DENSE_PARTS, D_MODEL, FF_PAD), wsel, pipeline_mode=one),
            pl.BlockSpec((N_DENSE_PARTS, D_MODEL, FF_PAD), wsel, pipeline_mode=one),
            pl.BlockSpec((N_DENSE_PARTS, FF_PAD, D_MODEL), wsel, pipeline_mode=one),
        ],
        out_specs=pl.BlockSpec((tm, D_MODEL), lambda i: (i, 0)),
        out_shape=jax.ShapeDtypeStruct((m, D_MODEL), F32),
        compiler_params=pltpu.CompilerParams(
            dimension_semantics=("parallel",), vmem_limit_bytes=VMEM_LIMIT),
        name="ffn_dense",
    )(*xs, ya, yb, ga, gb, wout, gf, wg, wu, wd)


MOE_WIN = 256
SEG = 16
SORT_ROWS = 2 * MOE_WIN + N_EXPERTS * SEG
ROW_TILE = 512
TAIL_TILES = (128, 256, ROW_TILE)
NOT_ROUTED = -1.0e6


def _top2(logits):
    lane = lax.broadcasted_iota(jnp.int32, logits.shape, 1).astype(F32)
    neg = jnp.float32(-jnp.inf)
    lg = jnp.where(lane < N_EXPERTS, logits, neg)
    m1 = jnp.max(lg, axis=1, keepdims=True)
    i1 = jnp.min(jnp.where(lg == m1, lane, float(LANES)), axis=1, keepdims=True)
    lg2 = jnp.where(lane == i1, neg, lg)
    m2 = jnp.max(lg2, axis=1, keepdims=True)
    i2 = jnp.min(jnp.where(lg2 == m2, lane, float(LANES)), axis=1, keepdims=True)
    ex = jnp.exp(m2 - m1)
    w1 = 1.0 / (1.0 + ex)
    w2 = ex / (1.0 + ex)
    gates = jnp.where(lane == i1, w1, 0.0) + jnp.where(lane == i2, w2, 0.0)
    mask = jnp.where((lane == i1) | (lane == i2), 1.0, 0.0)
    return gates, mask


def _pre_moe_kernel(*refs, n_res, n_win):
    (ya_ref, yb_ref, ga_ref, gb_ref, wout_ref, gf_ref, rwh_ref, rwl_ref, rb_ref,
     x1_ref, hn_ref, gates_ref, d_ref, dt_ref, cnt_ref) = refs[n_res:]
    x1 = _out_proj(refs[:n_res], ya_ref, yb_ref, ga_ref, gb_ref, wout_ref, slice(None))
    x1_ref[...] = x1
    hn = _rms(x1, gf_ref[...])
    hn_ref[...] = hn.astype(BF16)
    hi, lo = _split_bf16(hn)
    logits = _dot(hi, rwh_ref[...]) + _dot(lo, rwh_ref[...]) + _dot(hi, rwl_ref[...])
    gates, mask = _top2(logits + rb_ref[...])
    gates_ref[...] = gates
    row = lax.broadcasted_iota(jnp.int32, (MOE_WIN, MOE_WIN), 0)
    col = lax.broadcasted_iota(jnp.int32, (MOE_WIN, MOE_WIN), 1)
    earlier = jnp.where(col < row, 1.0, 0.0).astype(BF16)
    erow = lax.broadcasted_iota(jnp.int32, (LANES, LANES), 0)
    ecol = lax.broadcasted_iota(jnp.int32, (LANES, LANES), 1)
    lower_expert = jnp.where(erow < ecol, 1.0, 0.0).astype(BF16)
    lane = lax.broadcasted_iota(jnp.int32, (MOE_WIN, LANES), 1)
    for w in range(n_win):
        rs = slice(w * MOE_WIN, (w + 1) * MOE_WIN)
        mw = mask[rs, :]
        rank = _dot(earlier, mw.astype(BF16))
        cnt = jnp.broadcast_to(jnp.sum(mw, axis=0, keepdims=True), (SUBLANES, LANES))
        cnt_ref[w] = cnt
        padded = jnp.floor((cnt + (SEG - 1)) * (1.0 / SEG)) * SEG
        seg_start = _dot(padded.astype(BF16), lower_expert)[0:1, :]
        dest = rank + seg_start
        d_lo = jnp.min(jnp.where(mw > 0.0, dest, -NOT_ROUTED), axis=1, keepdims=True)
        d_hi = jnp.max(jnp.where(mw > 0.0, dest, NOT_ROUTED), axis=1, keepdims=True)
        d = jnp.where(lane == 0, d_lo, jnp.where(lane == 1, d_hi, 0.0))
        d_ref[rs, :] = d
        dt_ref[w] = d.T[:SUBLANES, :]


def _pre_moe(xs, ya, yb, ga, gb, wout, gf, rwh, rwl, rb, *, l, tm):
    m = xs[0].shape[0]
    n_win = tm // MOE_WIN
    j = l // 2
    lay3 = lambda i: (l, 0, 0)
    moe3 = lambda i: (j, 0, 0)
    return pl.pallas_call(
        functools.partial(_pre_moe_kernel, n_res=len(xs), n_win=n_win),
        grid=(m // tm,),
        in_specs=[pl.BlockSpec((tm, D_MODEL), lambda i: (i, 0)) for _ in xs] + [
            pl.BlockSpec((tm, MIX_A), lambda i: (i, 0)),
            pl.BlockSpec((tm, MIX_B), lambda i: (i, 0)),
            pl.BlockSpec((None, 1, MIX_A), lay3),
            pl.BlockSpec((None, 1, MIX_B), lay3),
            pl.BlockSpec((None, D_MODEL, D_MODEL), lay3),
            pl.BlockSpec((None, 1, D_MODEL), lay3),
            pl.BlockSpec((None, D_MODEL, LANES), moe3),
            pl.BlockSpec((None, D_MODEL, LANES), moe3),
            pl.BlockSpec((None, 1, LANES), moe3),
        ],
        out_specs=[
            pl.BlockSpec((tm, D_MODEL), lambda i: (i, 0)),
            pl.BlockSpec((tm, D_MODEL), lambda i: (i, 0)),
            pl.BlockSpec((tm, LANES), lambda i: (i, 0)),
            pl.BlockSpec((tm, LANES), lambda i: (i, 0)),
            pl.BlockSpec((n_win, SUBLANES, MOE_WIN), lambda i: (i, 0, 0)),
            pl.BlockSpec((n_win, SUBLANES, LANES), lambda i: (i, 0, 0)),
        ],
        out_shape=[
            jax.ShapeDtypeStruct((m, D_MODEL), F32),
            jax.ShapeDtypeStruct((m, D_MODEL), BF16),
            jax.ShapeDtypeStruct((m, LANES), F32),
            jax.ShapeDtypeStruct((m, LANES), F32),
            jax.ShapeDtypeStruct((m // MOE_WIN, SUBLANES, MOE_WIN), F32),
            jax.ShapeDtypeStruct((m // MOE_WIN, SUBLANES, LANES), F32),
        ],
        compiler_params=pltpu.CompilerParams(
            dimension_semantics=("parallel",), vmem_limit_bytes=VMEM_LIMIT),
        name="pre_moe",
    )(*xs, ya, yb, ga, gb, wout, gf, rwh, rwl, rb)


def _moe_kernel(cnt_sm, hn_ref, gates_ref, d_ref, dt_ref, wg_ref, wu_ref, wd_ref, x1_hbm, gfin_ref, o_ref,
                xs, gs, sb, gsb, x1buf, x1sem, pn_sm, s_sm, off_sm, est_sm, tot_sm, *, n_win, final):
    blk = pl.program_id(0)
    e = pl.program_id(1)

    def x1_copy(w, slot):
        row0 = pl.multiple_of((blk * n_win + w) * MOE_WIN, MOE_WIN)
        return pltpu.make_async_copy(x1_hbm.at[pl.ds(row0, MOE_WIN), :], x1buf.at[slot], x1sem.at[slot])

    def seg_copy(w, ee, to_sorted):
        s0 = s_sm[w * N_EXPERTS + ee]
        o0 = off_sm[w * N_EXPERTS + ee]

        def body(i, carry):
            src = pl.multiple_of(s0 + i * SEG, SEG)
            dst = pl.multiple_of(o0 + i * SEG, SEG)
            if to_sorted:
                xs[pl.ds(dst, SEG), :] = sb[pl.ds(src, SEG), :]
                gs[pl.ds(dst, SEG), :] = gsb[pl.ds(src, SEG), :]
            else:
                sb[pl.ds(src, SEG), :] = xs[pl.ds(dst, SEG), :]
            return carry

        lax.fori_loop(0, pn_sm[w * N_EXPERTS + ee] // SEG, body, 0)

    @pl.when(e == 0)
    def _dispatch():
        pn = [[None] * N_EXPERTS for _ in range(n_win)]
        for w in range(n_win):
            run = jnp.int32(0)
            for ee in range(N_EXPERTS):
                n = cnt_sm[(blk * n_win + w) * N_EXPERTS + ee]
                pn[w][ee] = jnp.bitwise_and(n + (SEG - 1), -SEG)
                pn_sm[w * N_EXPERTS + ee] = pn[w][ee]
                s_sm[w * N_EXPERTS + ee] = run
                run = run + pn[w][ee]
        run = jnp.int32(0)
        for ee in range(N_EXPERTS):
            est_sm[ee] = run
            start = run
            for w in range(n_win):
                off_sm[w * N_EXPERTS + ee] = run
                run = run + pn[w][ee]
            tot_sm[ee] = run - start
        end = pl.multiple_of(run, SEG)
        xs[pl.ds(end, ROW_TILE), :] = jnp.zeros((ROW_TILE, D_MODEL), BF16)
        gs[pl.ds(end, ROW_TILE), :] = jnp.zeros((ROW_TILE, LANES), F32)
        riota = lax.broadcasted_iota(jnp.int32, (SORT_ROWS, MOE_WIN), 0).astype(F32)
        for w in range(n_win):
            rs = slice(w * MOE_WIN, (w + 1) * MOE_WIN)
            g = jnp.where(riota == dt_ref[w, 0:1, :], 1.0, jnp.where(riota == dt_ref[w, 1:2, :], 1.0, 0.0))
            gb = g.astype(BF16)
            sb[...] = _dot(gb, hn_ref[rs, :]).astype(BF16)
            gh, gl = _split_bf16(gates_ref[rs, :])
            gsb[...] = _dot(gb, gh) + _dot(gb, gl)
            for ee in range(N_EXPERTS):
                seg_copy(w, ee, True)

    start = est_sm[e]
    tot = tot_sm[e]

    def row_tile(r0, size, valid):
        r0 = pl.multiple_of(r0, SEG)
        xt = xs[pl.ds(r0, size), :]
        y = _swiglu(xt, wg_ref, wu_ref, wd_ref)
        lane = lax.broadcasted_iota(jnp.int32, (size, LANES), 1)
        gate = jnp.sum(jnp.where(lane == e, gs[pl.ds(r0, size), :], 0.0), axis=1, keepdims=True)
        keep = lax.broadcasted_iota(jnp.int32, (size, D_MODEL), 0) < valid
        xs[pl.ds(r0, size), :] = jnp.where(keep, (y * gate).astype(BF16), xt)

    n_full = tot // ROW_TILE

    def full_body(i, carry):
        row_tile(start + i * ROW_TILE, ROW_TILE, ROW_TILE)
        return carry

    lax.fori_loop(0, n_full, full_body, 0)
    rem = tot - n_full * ROW_TILE
    tail = start + n_full * ROW_TILE

    lo = 0
    for size in TAIL_TILES:
        @pl.when((rem > lo) & (rem <= size))
        def _(size=size):
            row_tile(tail, size, rem)
        lo = size

    @pl.when(e == N_EXPERTS - 1)
    def _combine():
        liota = lax.broadcasted_iota(jnp.int32, (MOE_WIN, SORT_ROWS), 1).astype(F32)
        x1_copy(0, 0).start()
        for w in range(n_win):
            rs = slice(w * MOE_WIN, (w + 1) * MOE_WIN)
            slot = w % 2
            if w + 1 < n_win:
                x1_copy(w + 1, 1 - slot).start()
            for ee in range(N_EXPERTS):
                seg_copy(w, ee, False)
            g = jnp.where(liota == d_ref[rs, 0:1], 1.0, jnp.where(liota == d_ref[rs, 1:2], 1.0, 0.0))
            f = _dot(g.astype(BF16), sb[...])
            x1_copy(w, slot).wait()
            x2 = x1buf[slot] + f
            o_ref[rs, :] = _rms(x2, gfin_ref[...]) if final else x2


def _moe(cnt, hn, gates, d, dt, wg, wu, wd, x1, gfin, *, l, tb, final):
    m = hn.shape[0]
    n_win = tb // MOE_WIN
    j = l // 2
    xs_rows = 2 * tb + n_win * N_EXPERTS * SEG + ROW_TILE
    one = pl.Buffered(1)
    wsel = lambda i, e, c: (j * N_EXPERTS + e, 0, 0)
    grid_spec = pltpu.PrefetchScalarGridSpec(
        num_scalar_prefetch=1,
        grid=(m // tb, N_EXPERTS),
        in_specs=[
            pl.BlockSpec((tb, D_MODEL), lambda i, e, c: (i, 0), pipeline_mode=one),
            pl.BlockSpec((tb, LANES), lambda i, e, c: (i, 0), pipeline_mode=one),
            pl.BlockSpec((tb, LANES), lambda i, e, c: (i, 0), pipeline_mode=one),
            pl.BlockSpec((n_win, SUBLANES, MOE_WIN), lambda i, e, c: (i, 0, 0), pipeline_mode=one),
            pl.BlockSpec((None, D_MODEL, FF_PAD), wsel),
            pl.BlockSpec((None, D_MODEL, FF_PAD), wsel),
            pl.BlockSpec((None, FF_PAD, D_MODEL), wsel),
            pl.BlockSpec(memory_space=pl.ANY),
            pl.BlockSpec((1, D_MODEL), lambda i, e, c: (0, 0)),
        ],
        out_specs=pl.BlockSpec((tb, D_MODEL), lambda i, e, c: (i, 0)),
        scratch_shapes=[
            pltpu.VMEM((xs_rows, D_MODEL), BF16),
            pltpu.VMEM((xs_rows, LANES), F32),
            pltpu.VMEM((SORT_ROWS, D_MODEL), BF16),
            pltpu.VMEM((SORT_ROWS, LANES), F32),
            pltpu.VMEM((2, MOE_WIN, D_MODEL), F32),
            pltpu.SemaphoreType.DMA((2,)),
            pltpu.SMEM((n_win * N_EXPERTS,), jnp.int32),
            pltpu.SMEM((n_win * N_EXPERTS,), jnp.int32),
            pltpu.SMEM((n_win * N_EXPERTS,), jnp.int32),
            pltpu.SMEM((N_EXPERTS,), jnp.int32),
            pltpu.SMEM((N_EXPERTS,), jnp.int32),
        ],
    )
    return pl.pallas_call(
        functools.partial(_moe_kernel, n_win=n_win, final=final),
        grid_spec=grid_spec,
        out_shape=jax.ShapeDtypeStruct((m, D_MODEL), F32),
        compiler_params=pltpu.CompilerParams(
            dimension_semantics=("arbitrary", "arbitrary"), vmem_limit_bytes=MOE_VMEM_LIMIT),
        name="moe",
    )(cnt, hn, gates, d, dt, wg, wu, wd, x1, gfin)


def kernel(x_prompt, x_sample, state_ssm_re, state_ssm_im, norm_mix, w_in, ssm_a_re, ssm_a_im, ssm_log_dt, ssm_b_re, ssm_b_im, ssm_c_re, ssm_c_im, ssm_d, glu_w, glu_b, sgu_w, sgu_b, sgu_ln_g, sgu_ln_b, out_norm_a, out_norm_b, w_out, norm_ffn, ffn_w_gate, ffn_w_up, ffn_w_down, router_w, router_b, moe_w_gate, moe_w_up, moe_w_down, norm_final):
    batch, seq = x_prompt.shape[0], x_prompt.shape[1]
    dec_batch, dec_seq = x_sample.shape[0], x_sample.shape[1]
    mp, ms = batch * seq, dec_batch * dec_seq
    n_moe = moe_w_gate.shape[0]

    lam, bbre, bbim = _s5_prep(ssm_a_re, ssm_a_im, ssm_log_dt, ssm_b_re, ssm_b_im)
    ctre = jnp.swapaxes(ssm_c_re, 2, 3).reshape(DEPTH * N_STATE, SSM_GROUP)
    ctim = jnp.swapaxes(ssm_c_im, 2, 3).reshape(DEPTH * N_STATE, SSM_GROUP)
    dskip = ssm_d.reshape(DEPTH, 1, MIX_A)
    glu = glu_w.reshape(DEPTH * MIX_A, 2 * SSM_GROUP)
    gbias = jnp.stack([glu_b[..., :SSM_GROUP].reshape(DEPTH, MIX_A),
                       glu_b[..., SSM_GROUP:].reshape(DEPTH, MIX_A)], axis=1)
    s5w = (lam, bbre, bbim, ctre, ctim, dskip, glu, gbias)
    w_in_b = w_in.astype(BF16)
    w_out_b = w_out.astype(BF16)
    g_mix = norm_mix.reshape(DEPTH, 1, D_MODEL)
    g_a = out_norm_a.reshape(DEPTH, 1, MIX_A)
    g_b = out_norm_b.reshape(DEPTH, 1, MIX_B)
    g_ffn = norm_ffn.reshape(DEPTH, 1, D_MODEL)
    sgu_bias = jnp.repeat(jnp.swapaxes(sgu_b, 1, 2), HD_B, axis=2)
    sgu_bias_s = sgu_bias[:, :dec_seq]
    lng = sgu_ln_g.reshape(DEPTH, 1, MIX_B)
    lnb = sgu_ln_b.reshape(DEPTH, 1, MIX_B)
    coef_s = jnp.repeat(
        jnp.transpose(sgu_w[:, :, :dec_seq, :dec_seq], (0, 2, 3, 1)).reshape(DEPTH, dec_seq * dec_seq, NH_B),
        HD_B, axis=2)
    dense_split = (FF_PAD, D_FF - FF_PAD)
    dense_g = _pack(jnp.swapaxes(ffn_w_gate, 1, 2), dense_split, transpose=True)
    dense_u = _pack(jnp.swapaxes(ffn_w_up, 1, 2), dense_split, transpose=True)
    dense_d = _pack(ffn_w_down, dense_split, transpose=False)
    moe_t = lambda w: jnp.swapaxes(w.reshape(-1, D_MODEL, D_FF_EXPERT), 1, 2)
    moe_g = _pack(moe_t(moe_w_gate), (D_FF_EXPERT,), transpose=True)
    moe_u = _pack(moe_t(moe_w_up), (D_FF_EXPERT,), transpose=True)
    moe_d = _pack(moe_w_down.reshape(-1, D_FF_EXPERT, D_MODEL), (D_FF_EXPERT,), transpose=False)
    rw = jnp.pad(router_w, ((0, 0), (0, 0), (0, LANES - N_EXPERTS)))
    rwh = rw.astype(BF16)
    rwl = (rw - rwh.astype(F32)).astype(BF16)
    rb = jnp.pad(router_b, ((0, 0), (0, LANES - N_EXPERTS))).reshape(n_moe, 1, LANES)
    gfin = norm_final.reshape(1, D_MODEL)

    h0re = state_ssm_re.reshape(DEPTH, dec_batch, N_STATE)
    h0im = state_ssm_im.reshape(DEPTH, dec_batch, N_STATE)

    assert DEPTH % 2 == 0

    def post_mixer(xres, ya, yb, l, tm, tb):
        common = (g_a, g_b, w_out_b, g_ffn)
        if l % 2 == 0:
            return (_ffn(xres, ya, yb, *common, dense_g, dense_u, dense_d, l=l, tm=tm),)
        x1, hn, gates, d, dt, cnt = _pre_moe(xres, ya, yb, *common, rwh, rwl, rb, l=l, tm=min(tm, 512))
        cnt = cnt[:, 0, :N_EXPERTS].astype(jnp.int32).reshape(-1)
        return (_moe(cnt, hn, gates, d, dt, moe_g, moe_u, moe_d, x1, gfin, l=l, tb=tb, final=(l == DEPTH - 1)),)

    xp = (x_prompt.reshape(mp, D_MODEL),)
    xs = (x_sample.reshape(ms, D_MODEL),)
    re_p, im_p, re_s, im_s, v_s = [], [], [], [], []
    for l in range(DEPTH):
        ya_p, yb_p, hre_p, him_p = _mixer(xp[0].reshape(batch, seq, D_MODEL), g_mix, w_in_b, sgu_w, sgu_bias,
                                          lng, lnb, s5w, l)
        zs = _inproj(xs, g_mix, w_in_b, l, tm=ms)
        ya_s, hre_s, him_s = _s5_sample(zs, h0re, h0im, s5w, l, nb=dec_batch, t_blk=dec_seq)
        yb_s, vn_s = _sgu_sample(zs, coef_s, sgu_bias_s, lng, lnb, l, nb=dec_batch, t_blk=dec_seq)
        xp = post_mixer(xp, ya_p.reshape(mp, MIX_A), yb_p.reshape(mp, MIX_B), l, tm=512, tb=seq)
        xs = post_mixer(xs, ya_s, yb_s, l, tm=ms, tb=ms)
        re_p.append(hre_p)
        im_p.append(him_p)
        re_s.append(hre_s)
        im_s.append(him_s)
        v_s.append(vn_s)
    xp, xs = xp[0], xs[0]

    st = lambda hs, b: jnp.stack(hs).reshape(DEPTH, b, NG_A, SSM_STATE)
    return (xp.reshape(batch, seq, D_MODEL), xs.reshape(dec_batch, dec_seq, D_MODEL),
            st(re_p, batch), st(im_p, batch), st(re_s, dec_batch), st(im_s, dec_batch),
            jnp.stack(v_s).reshape(DEPTH, dec_batch, dec_seq, NH_B, HD_B))
```

```python
import functools
import math

import jax
import jax.numpy as jnp
from jax import lax
from jax.experimental import pallas as pl
from jax.experimental.pallas import tpu as pltpu

F32 = jnp.float32
BF16 = jnp.bfloat16

D_MODEL = 1024
DEPTH = 4
MIX_A = 512
SSM_GROUP = 16
NG_A = 32
SSM_STATE = 64
N_STATE = NG_A * SSM_STATE
MIX_B = 512
HD_B = 64
NH_B = 8
CHUNK = 128
IN_WIDTH = MIX_A + 2 * MIX_B
D_FF = 2752
N_EXPERTS = 8
D_FF_EXPERT = D_FF // 2
FF_PAD = 1408
EPS = 1e-6

LANES = 128
SUBLANES = 8
MXU_DIM = 256
STATE_COLS = 512
N_SCOL = N_STATE // STATE_COLS
GROUPS_PER_CHUNK = STATE_COLS // SSM_STATE
VMEM_LIMIT = 52 * 1024 * 1024
MOE_VMEM_LIMIT = 58 * 1024 * 1024


def _rms(x, g):
    return x * lax.rsqrt(jnp.mean(x * x, axis=-1, keepdims=True) + EPS) * g


def _gelu(x):
    c = math.sqrt(2.0 / math.pi)
    return 0.5 * x * (1.0 + jnp.tanh(c * (x + 0.044715 * (x * x * x))))


def _sigmoid(x):
    return 1.0 / (1.0 + jnp.exp(-x))


def _split_bf16(x):
    hi = x.astype(BF16)
    lo = (x - hi.astype(F32)).astype(BF16)
    return hi, lo


def _dot(a, b):
    return jnp.dot(a, b, preferred_element_type=F32)


def _log2(n):
    assert n & (n - 1) == 0
    return n.bit_length() - 1


def _block_diag_tile(src, n_rep, rows_per_group):
    r_n, k = src.shape
    n = k * n_rep
    t_row = lax.broadcasted_iota(jnp.int32, (k, n), 0)
    t_col = lax.broadcasted_iota(jnp.int32, (k, n), 1)
    tile = jnp.where(jnp.bitwise_and(t_col, k - 1) == t_row, 1.0, 0.0).astype(BF16)
    full = _dot(src.astype(BF16), tile)
    row = lax.broadcasted_iota(jnp.int32, (r_n, n), 0)
    col = lax.broadcasted_iota(jnp.int32, (r_n, n), 1)
    same = lax.shift_right_logical(row, _log2(rows_per_group)) == lax.shift_right_logical(col, _log2(k))
    return jnp.where(same, full, 0.0).astype(BF16)


def _pack_kernel(w_ref, o_ref, *, widths, transpose):
    for i, width in enumerate(widths):
        w = w_ref[i * FF_PAD:i * FF_PAD + width, :]
        if width < FF_PAD:
            w = jnp.concatenate([w, jnp.zeros((FF_PAD - width, w.shape[1]), F32)], axis=0)
        o_ref[i] = (w.T if transpose else w).astype(BF16)


def _pack(w, widths, transpose):
    n_e, n_parts = w.shape[0], len(widths)
    out_blk = (n_parts, D_MODEL, FF_PAD) if transpose else (n_parts, FF_PAD, D_MODEL)
    out = pl.pallas_call(
        functools.partial(_pack_kernel, widths=widths, transpose=transpose),
        grid=(n_e,),
        in_specs=[pl.BlockSpec((None,) + w.shape[1:], lambda e: (e, 0, 0))],
        out_specs=pl.BlockSpec((None,) + out_blk, lambda e: (e, 0, 0, 0)),
        out_shape=jax.ShapeDtypeStruct((n_e,) + out_blk, BF16),
        compiler_params=pltpu.CompilerParams(
            dimension_semantics=("parallel",), vmem_limit_bytes=VMEM_LIMIT),
        name="pack_t" if transpose else "pack",
    )(w)
    return out.reshape((n_e * n_parts,) + out_blk[1:])


def _s5_prep_kernel(are_ref, aim_ref, ldt_ref, bre_ref, bim_ref,
                    abre_ref, abim_ref, bbre_ref, bbim_ref):
    lam_re = are_ref[...]
    lam_im = aim_ref[...]
    dt = jnp.exp(ldt_ref[...])
    mag = jnp.exp(lam_re * dt)
    ab_re = mag * jnp.cos(lam_im * dt)
    ab_im = mag * jnp.sin(lam_im * dt)
    den = lam_re * lam_re + lam_im * lam_im
    nr = ab_re - 1.0
    q_re = (nr * lam_re + ab_im * lam_im) / den
    q_im = (ab_im * lam_re - nr * lam_im) / den
    b_re = bre_ref[...]
    b_im = bim_ref[...]
    abre_ref[...] = ab_re
    abim_ref[...] = ab_im
    bbre_ref[...] = q_re * b_re - q_im * b_im
    bbim_ref[...] = q_re * b_im + q_im * b_re


def _s5_prep(ssm_a_re, ssm_a_im, ssm_log_dt, ssm_b_re, ssm_b_im):
    shp = (DEPTH, NG_A, SSM_GROUP, SSM_STATE)
    rows = DEPTH * NG_A * SSM_GROUP
    bc = lambda a: jnp.broadcast_to(a, shp).reshape(rows, SSM_STATE)
    are = bc(ssm_a_re[:, :, None, :])
    aim = bc(ssm_a_im[:, :, None, :])
    ldt = bc(ssm_log_dt[:, :, None, None])
    bre = jnp.swapaxes(ssm_b_re, 2, 3).reshape(rows, SSM_STATE)
    bim = jnp.swapaxes(ssm_b_im, 2, 3).reshape(rows, SSM_STATE)
    sds = jax.ShapeDtypeStruct((rows, SSM_STATE), F32)
    abre, abim, bbre, bbim = pl.pallas_call(
        _s5_prep_kernel, out_shape=(sds, sds, sds, sds), name="s5_prep",
    )(are, aim, ldt, bre, bim)
    lam_re = abre.reshape(shp)[:, :, 0, :].reshape(DEPTH, 1, N_STATE)
    lam_im = abim.reshape(shp)[:, :, 0, :].reshape(DEPTH, 1, N_STATE)
    lam = jnp.concatenate([lam_re, lam_im], axis=1)
    return lam, bbre, bbim


def _inproj_kernel(*refs, n_res):
    g_ref, w_ref, z_ref = refs[n_res:]
    x = refs[0][...]
    for r in refs[1:n_res]:
        x = x + r[...]
    hn = _rms(x, g_ref[...])
    z_ref[...] = _dot(hn.astype(BF16), w_ref[...])


def _inproj(xs, g, w, l, tm):
    m = xs[0].shape[0]
    return pl.pallas_call(
        functools.partial(_inproj_kernel, n_res=len(xs)),
        grid=(m // tm,),
        in_specs=[pl.BlockSpec((tm, D_MODEL), lambda i: (i, 0)) for _ in xs] + [
            pl.BlockSpec((None, 1, D_MODEL), lambda i: (l, 0, 0)),
            pl.BlockSpec((None, D_MODEL, IN_WIDTH), lambda i: (l, 0, 0)),
        ],
        out_specs=pl.BlockSpec((tm, IN_WIDTH), lambda i: (i, 0)),
        out_shape=jax.ShapeDtypeStruct((m, IN_WIDTH), F32),
        compiler_params=pltpu.CompilerParams(
            dimension_semantics=("parallel",), vmem_limit_bytes=VMEM_LIMIT),
        name="inproj",
    )(*xs, g, w)


def _s5_expand(bbre_ref, bbim_ref, ctre_ref, ctim_ref, glu_ref, bblk, cre, cimn, wa, wg):
    for c in range(4):
        rs = slice(c * LANES, (c + 1) * LANES)
        bblk[c, :, :STATE_COLS] = _block_diag_tile(bbre_ref[rs, :], GROUPS_PER_CHUNK, SSM_GROUP)
        bblk[c, :, STATE_COLS:] = _block_diag_tile(bbim_ref[rs, :], GROUPS_PER_CHUNK, SSM_GROUP)
        ss = slice(c * STATE_COLS, (c + 1) * STATE_COLS)
        cre[c] = _block_diag_tile(ctre_ref[ss, :], GROUPS_PER_CHUNK, SSM_STATE)
        cimn[c] = _block_diag_tile(-ctim_ref[ss, :], GROUPS_PER_CHUNK, SSM_STATE)
    for b in range(MIX_A // MXU_DIM):
        ms = slice(b * MXU_DIM, (b + 1) * MXU_DIM)
        wa[b] = _block_diag_tile(glu_ref[ms, :SSM_GROUP], MXU_DIM // SSM_GROUP, SSM_GROUP)
        wg[b] = _block_diag_tile(glu_ref[ms, SSM_GROUP:], MXU_DIM // SSM_GROUP, SSM_GROUP)


def _s5_time_major(utm, lam_ref, d_ref, gbias_ref, hre_ref, him_ref, bre, bim, bblk, cre, cimn, wa, wg,
                   nb, t_blk):
    for c in range(4):
        r = _dot(utm[c].astype(BF16), bblk[c])
        bre[:, c * STATE_COLS:(c + 1) * STATE_COLS] = r[:, :STATE_COLS]
        bim[:, c * STATE_COLS:(c + 1) * STATE_COLS] = r[:, STATE_COLS:]

    for c in range(N_SCOL):
        cs = slice(c * STATE_COLS, (c + 1) * STATE_COLS)
        lr = jnp.broadcast_to(lam_ref[0:1, cs], (SUBLANES, STATE_COLS))
        li = jnp.broadcast_to(lam_ref[1:2, cs], (SUBLANES, STATE_COLS))

        if nb == SUBLANES:
            hr, hi = hre_ref[:, cs], him_ref[:, cs]
            for t in range(t_blk):
                rs = slice(t * SUBLANES, (t + 1) * SUBLANES)
                hr, hi = lr * hr - li * hi + bre[rs, cs], lr * hi + li * hr + bim[rs, cs]
                bre[rs, cs] = hr
                bim[rs, cs] = hi
            hre_ref[:, cs] = hr
            him_ref[:, cs] = hi
            continue

        def group_body(bg, carry, cs=cs, lr=lr, li=li):
            r0 = pl.multiple_of(bg * SUBLANES, SUBLANES)

            def t_body(t, h):
                hr, hi = h
                row = pl.multiple_of(t * nb + r0, SUBLANES)
                nr = lr * hr - li * hi + bre[pl.ds(row, SUBLANES), cs]
                ni = lr * hi + li * hr + bim[pl.ds(row, SUBLANES), cs]
                bre[pl.ds(row, SUBLANES), cs] = nr
                bim[pl.ds(row, SUBLANES), cs] = ni
                return nr, ni

            h0 = (hre_ref[pl.ds(r0, SUBLANES), cs], him_ref[pl.ds(r0, SUBLANES), cs])
            hr, hi = lax.fori_loop(0, t_blk, t_body, h0, unroll=min(t_blk, 8))
            hre_ref[pl.ds(r0, SUBLANES), cs] = hr
            him_ref[pl.ds(r0, SUBLANES), cs] = hi
            return carry

        lax.fori_loop(0, nb // SUBLANES, group_body, 0)

    ys = []
    for c in range(4):
        cs = slice(c * STATE_COLS, (c + 1) * STATE_COLS)
        ls = slice(c * LANES, (c + 1) * LANES)
        y = _dot(bre[:, cs].astype(BF16), cre[c]) + _dot(bim[:, cs].astype(BF16), cimn[c])
        ys.append(_gelu(y + d_ref[:, ls] * utm[c]).astype(BF16))

    os_ = []
    for b in range(MIX_A // MXU_DIM):
        ms = slice(b * MXU_DIM, (b + 1) * MXU_DIM)
        yb = jnp.concatenate(ys[2 * b:2 * b + 2], axis=1)
        za = _dot(yb, wa[b]) + gbias_ref[0:1, ms]
        zg = _dot(yb, wg[b]) + gbias_ref[1:2, ms]
        os_.append(za * _sigmoid(zg))
    return jnp.concatenate(os_, axis=1)


def _s5_kernel(*refs, nb, t_blk):
    (u_ref, h0re_ref, h0im_ref, bbre_ref, bbim_ref, lam_ref, ctre_ref, ctim_ref, d_ref, glu_ref, gbias_ref,
     ya_ref, hre_ref, him_ref, utm, bre, bim, otm, bblk, cre, cimn, wa, wg, tmp) = refs
    hre_ref[...] = h0re_ref[...]
    him_ref[...] = h0im_ref[...]
    _s5_expand(bbre_ref, bbim_ref, ctre_ref, ctim_ref, glu_ref, bblk, cre, cimn, wa, wg)

    for c in range(4):
        ls = slice(c * LANES, (c + 1) * LANES)
        tmp[c] = u_ref[:, ls]
        for t in range(t_blk):
            utm[c, t * nb:(t + 1) * nb, :] = tmp[c, pl.ds(t, nb, stride=t_blk), :]

    o = _s5_time_major(utm, lam_ref, d_ref, gbias_ref, hre_ref, him_ref, bre, bim, bblk, cre, cimn, wa, wg,
                       nb, t_blk)

    for c in range(4):
        ls = slice(c * LANES, (c + 1) * LANES)
        for t in range(t_blk):
            otm[c, pl.ds(t, nb, stride=t_blk), :] = o[t * nb:(t + 1) * nb, ls]
        ya_ref[:, ls] = otm[c]


def _s5_weight_specs(l):
    return [
        pl.BlockSpec((NG_A * SSM_GROUP, SSM_STATE), lambda j: (l, 0)),
        pl.BlockSpec((NG_A * SSM_GROUP, SSM_STATE), lambda j: (l, 0)),
        pl.BlockSpec((None, 2, N_STATE), lambda j: (l, 0, 0)),
        pl.BlockSpec((N_STATE, SSM_GROUP), lambda j: (l, 0)),
        pl.BlockSpec((N_STATE, SSM_GROUP), lambda j: (l, 0)),
        pl.BlockSpec((None, 1, MIX_A), lambda j: (l, 0, 0)),
        pl.BlockSpec((MIX_A, 2 * SSM_GROUP), lambda j: (l, 0)),
        pl.BlockSpec((None, 2, MIX_A), lambda j: (l, 0, 0)),
    ]


def _s5_scratch(rows):
    return [
        pltpu.VMEM((4, rows, LANES), F32),
        pltpu.VMEM((rows, N_STATE), F32),
        pltpu.VMEM((rows, N_STATE), F32),
        pltpu.VMEM((4, rows, LANES), F32),
        pltpu.VMEM((4, LANES, 2 * STATE_COLS), BF16),
        pltpu.VMEM((4, STATE_COLS, LANES), BF16),
        pltpu.VMEM((4, STATE_COLS, LANES), BF16),
        pltpu.VMEM((MIX_A // MXU_DIM, MXU_DIM, MXU_DIM), BF16),
        pltpu.VMEM((MIX_A // MXU_DIM, MXU_DIM, MXU_DIM), BF16),
    ]


def _s5_sample(z, h0re, h0im, weights, l, nb, t_blk):
    rows = nb * t_blk
    lam, bbre, bbim, ctre, ctim, dskip, glu, gbias = weights
    blk = pl.BlockSpec((rows, MIX_A), lambda j: (0, 0))
    st_in = pl.BlockSpec((None, nb, N_STATE), lambda j: (l, 0, 0))
    st_out = pl.BlockSpec((nb, N_STATE), lambda j: (0, 0))
    return pl.pallas_call(
        functools.partial(_s5_kernel, nb=nb, t_blk=t_blk),
        grid=(1,),
        in_specs=[blk, st_in, st_in] + _s5_weight_specs(l),
        out_specs=[blk, st_out, st_out],
        out_shape=[
            jax.ShapeDtypeStruct((rows, MIX_A), F32),
            jax.ShapeDtypeStruct((nb, N_STATE), F32),
            jax.ShapeDtypeStruct((nb, N_STATE), F32),
        ],
        scratch_shapes=_s5_scratch(rows) + [pltpu.VMEM((4, rows, LANES), F32)],
        compiler_params=pltpu.CompilerParams(
            dimension_semantics=("arbitrary",), vmem_limit_bytes=VMEM_LIMIT),
        name="s5_sample",
    )(z, h0re, h0im, bbre, bbim, lam, ctre, ctim, dskip, glu, gbias)


def _group_layernorm(gv, lng_ref, lnb_ref):
    row = lax.broadcasted_iota(jnp.int32, (MXU_DIM, MXU_DIM), 0)
    col = lax.broadcasted_iota(jnp.int32, (MXU_DIM, MXU_DIM), 1)
    same = lax.shift_right_logical(row, _log2(HD_B)) == lax.shift_right_logical(col, _log2(HD_B))
    gm = jnp.where(same, 1.0 / HD_B, 0.0).astype(BF16)

    def gmean(a):
        hi, lo = _split_bf16(a)
        parts = []
        for b in range(MIX_B // MXU_DIM):
            ms = slice(b * MXU_DIM, (b + 1) * MXU_DIM)
            parts.append(_dot(hi[:, ms], gm) + _dot(lo[:, ms], gm))
        return jnp.concatenate(parts, axis=1)

    xc = gv - gmean(gv)
    var = gmean(xc * xc)
    return xc * lax.rsqrt(var + EPS) * lng_ref[...] + lnb_ref[...]


def _mixer_kernel(x_ref, g_ref, w_ref, ws_ref, bias_ref, lng_ref, lnb_ref,
                  bbre_ref, bbim_ref, lam_ref, ctre_ref, ctim_ref, d_ref, glu_ref, gbias_ref,
                  ya_ref, yb_ref, hre_ref, him_ref,
                  hnb, ua, zb, utm, bre, bim, otm, bblk, cre, cimn, wa, wg, *, nb):
    @pl.when(pl.program_id(0) == 0)
    def _():
        hre_ref[...] = jnp.zeros_like(hre_ref)
        him_ref[...] = jnp.zeros_like(him_ref)
        _s5_expand(bbre_ref, bbim_ref, ctre_ref, ctim_ref, glu_ref, bblk, cre, cimn, wa, wg)

    for b in range(nb):
        hnb[b * CHUNK:(b + 1) * CHUNK, :] = _rms(x_ref[b], g_ref[...]).astype(BF16)
    ua[...] = _dot(hnb[...], w_ref[:, :MIX_A])
    for c in range(4):
        ls = slice(c * LANES, (c + 1) * LANES)
        for b in range(nb):
            utm[c, pl.ds(b, CHUNK, stride=nb), :] = ua[b * CHUNK:(b + 1) * CHUNK, ls]
    zb[...] = _dot(hnb[...], w_ref[:, MIX_A:])

    row = lax.broadcasted_iota(jnp.int32, (CHUNK, 2 * CHUNK), 0)
    col = lax.broadcasted_iota(jnp.int32, (CHUNK, 2 * CHUNK), 1)
    causal = jnp.bitwise_and(col, CHUNK - 1) <= row
    lane = lax.broadcasted_iota(jnp.int32, (CHUNK, LANES), 1)
    first_head = lane < HD_B
    wcat = []
    for p in range(NH_B // 2):
        w = jnp.concatenate([ws_ref[2 * p], ws_ref[2 * p + 1]], axis=1)
        wcat.append(jnp.where(causal, w, 0.0).astype(BF16))
    for b in range(nb):
        rs = slice(b * CHUNK, (b + 1) * CHUNK)
        u = _gelu(zb[rs, :MIX_B])
        vn = _group_layernorm(_gelu(zb[rs, MIX_B:]), lng_ref, lnb_ref)
        for p in range(NH_B // 2):
            ls = slice(p * LANES, (p + 1) * LANES)
            vp = vn[:, ls]
            rhs = jnp.concatenate(
                [jnp.where(first_head, vp, 0.0), jnp.where(first_head, 0.0, vp)], axis=0).astype(BF16)
            s = _dot(wcat[p], rhs)
            yb_ref[b, :, ls] = u[:, ls] * (s + bias_ref[:, ls])

    o = _s5_time_major(utm, lam_ref, d_ref, gbias_ref, hre_ref, him_ref, bre, bim, bblk, cre, cimn, wa, wg,
                       nb, CHUNK)
    for c in range(4):
        ls = slice(c * LANES, (c + 1) * LANES)
        otm[c] = o[:, ls]
        for b in range(nb):
            ya_ref[b, :, ls] = otm[c, pl.ds(b, CHUNK, stride=nb), :]


def _mixer(x3, g, w, ws, bias, lng, lnb, s5w, l):
    nb, seq = x3.shape[0], x3.shape[1]
    rows = nb * CHUNK
    lam, bbre, bbim, ctre, ctim, dskip, glu, gbias = s5w
    lay3 = lambda j: (l, 0, 0)
    blk = lambda width: pl.BlockSpec((nb, CHUNK, width), lambda j: (0, j, 0))
    st = pl.BlockSpec((nb, N_STATE), lambda j: (0, 0))
    return pl.pallas_call(
        functools.partial(_mixer_kernel, nb=nb),
        grid=(seq // CHUNK,),
        in_specs=[
            blk(D_MODEL),
            pl.BlockSpec((None, 1, D_MODEL), lay3),
            pl.BlockSpec((None, D_MODEL, IN_WIDTH), lay3, pipeline_mode=pl.Buffered(1)),
            pl.BlockSpec((None, NH_B, CHUNK, CHUNK), lambda j: (l, 0, 0, 0)),
            pl.BlockSpec((None, CHUNK, MIX_B), lay3),
            pl.BlockSpec((None, 1, MIX_B), lay3),
            pl.BlockSpec((None, 1, MIX_B), lay3),
        ] + _s5_weight_specs(l),
        out_specs=[blk(MIX_A), blk(MIX_B), st, st],
        out_shape=[
            jax.ShapeDtypeStruct((nb, seq, MIX_A), F32),
            jax.ShapeDtypeStruct((nb, seq, MIX_B), F32),
            jax.ShapeDtypeStruct((nb, N_STATE), F32),
            jax.ShapeDtypeStruct((nb, N_STATE), F32),
        ],
        scratch_shapes=[
            pltpu.VMEM((rows, D_MODEL), BF16),
            pltpu.VMEM((rows, MIX_A), F32),
            pltpu.VMEM((rows, 2 * MIX_B), F32),
        ] + _s5_scratch(rows),
        compiler_params=pltpu.CompilerParams(
            dimension_semantics=("arbitrary",), vmem_limit_bytes=MOE_VMEM_LIMIT),
        name="mixer",
    )(x3, g, w, ws, bias, lng, lnb, bbre, bbim, lam, ctre, ctim, dskip, glu, gbias)


def _sgu_sample_kernel(u_ref, v_ref, coef_ref, bias_ref, lng_ref, lnb_ref, o_ref, vn_ref,
                       usc, vsc, osc, *, nb, t_blk):
    vn = _group_layernorm(_gelu(v_ref[...]), lng_ref, lnb_ref)
    vn_ref[...] = vn
    u = _gelu(u_ref[...])
    for c in range(4):
        ls = slice(c * LANES, (c + 1) * LANES)
        vsc[c] = vn[:, ls]
        usc[c] = u[:, ls]
        for q in range(t_blk):
            s = bias_ref[q:q + 1, ls]
            for k in range(q + 1):
                s = s + coef_ref[q * t_blk + k:q * t_blk + k + 1, ls] * vsc[c, pl.ds(k, nb, stride=t_blk), :]
            osc[c, pl.ds(q, nb, stride=t_blk), :] = usc[c, pl.ds(q, nb, stride=t_blk), :] * s
        o_ref[:, ls] = osc[c]


def _sgu_sample(z, coef, bias, lng, lnb, l, nb, t_blk):
    m = nb * t_blk
    fix2 = lambda i: (0, 0)
    lay3 = lambda i: (l, 0, 0)
    return pl.pallas_call(
        functools.partial(_sgu_sample_kernel, nb=nb, t_blk=t_blk),
        grid=(1,),
        in_specs=[
            pl.BlockSpec((m, MIX_B), lambda i: (0, 1)),
            pl.BlockSpec((m, MIX_B), lambda i: (0, 2)),
            pl.BlockSpec((None, t_blk * t_blk, MIX_B), lay3),
            pl.BlockSpec((None, t_blk, MIX_B), lay3),
            pl.BlockSpec((None, 1, MIX_B), lay3),
            pl.BlockSpec((None, 1, MIX_B), lay3),
        ],
        out_specs=[pl.BlockSpec((m, MIX_B), fix2), pl.BlockSpec((m, MIX_B), fix2)],
        out_shape=[jax.ShapeDtypeStruct((m, MIX_B), F32), jax.ShapeDtypeStruct((m, MIX_B), F32)],
        scratch_shapes=[pltpu.VMEM((4, m, LANES), F32)] * 3,
        compiler_params=pltpu.CompilerParams(
            dimension_semantics=("arbitrary",), vmem_limit_bytes=VMEM_LIMIT),
        name="sgu_sample",
    )(z, z, coef, bias, lng, lnb)


def _swiglu(x, wg_ref, wu_ref, wd_ref):
    hg = _dot(x, wg_ref[...])
    a = (hg * _sigmoid(hg) * _dot(x, wu_ref[...])).astype(BF16)
    return _dot(a, wd_ref[...])


def _out_proj(res_refs, ya_ref, yb_ref, ga_ref, gb_ref, wout_ref, rs):
    x = res_refs[0][rs, :]
    for r in res_refs[1:]:
        x = x + r[rs, :]
    na = _rms(ya_ref[rs, :], ga_ref[...]).astype(BF16)
    nb = _rms(yb_ref[rs, :], gb_ref[...]).astype(BF16)
    return x + _dot(na, wout_ref[0:MIX_A, :]) + _dot(nb, wout_ref[MIX_A:, :])


N_DENSE_PARTS = 2


def _ffn_kernel(*refs, n_res):
    (ya_ref, yb_ref, ga_ref, gb_ref, wout_ref, gf_ref, wg_ref, wu_ref, wd_ref, o_ref) = refs[n_res:]
    x1 = _out_proj(refs[:n_res], ya_ref, yb_ref, ga_ref, gb_ref, wout_ref, slice(None))
    hn = _rms(x1, gf_ref[...]).astype(BF16)
    o_ref[...] = x1
    for p in range(N_DENSE_PARTS):
        o_ref[...] += _swiglu(hn, wg_ref.at[p], wu_ref.at[p], wd_ref.at[p])


def _ffn(xs, ya, yb, ga, gb, wout, gf, wg, wu, wd, *, l, tm):
    m = xs[0].shape[0]
    j = l // 2
    one = pl.Buffered(1)
    lay3 = lambda i: (l, 0, 0)
    wsel = lambda i: (j, 0, 0)
    return pl.pallas_call(
        functools.partial(_ffn_kernel, n_res=len(xs)),
        grid=(m // tm,),
        in_specs=[pl.BlockSpec((tm, D_MODEL), lambda i: (i, 0)) for _ in xs] + [
            pl.BlockSpec((tm, MIX_A), lambda i: (i, 0)),
            pl.BlockSpec((tm, MIX_B), lambda i: (i, 0)),
            pl.BlockSpec((None, 1, MIX_A), lay3),
            pl.BlockSpec((None, 1, MIX_B), lay3),
            pl.BlockSpec((None, D_MODEL, D_MODEL), lay3, pipeline_mode=one),
            pl.BlockSpec((None, 1, D_MODEL), lay3),
            pl.BlockSpec((N_DENSE_PARTS, D_MODEL, FF_PAD), wsel, pipeline_mode=one),
            pl.BlockSpec((N_DENSE_PARTS, D_MODEL, FF_PAD), wsel, pipeline_mode=one),
            pl.BlockSpec((N_DENSE_PARTS, FF_PAD, D_MODEL), wsel, pipeline_mode=one),
        ],
        out_specs=pl.BlockSpec((tm, D_MODEL), lambda i: (i, 0)),
        out_shape=jax.ShapeDtypeStruct((m, D_MODEL), F32),
        compiler_params=pltpu.CompilerParams(
            dimension_semantics=("parallel",), vmem_limit_bytes=VMEM_LIMIT),
        name="ffn_dense",
    )(*xs, ya, yb, ga, gb, wout, gf, wg, wu, wd)


MOE_WIN = 256
SEG = 16
SORT_ROWS = 2 * MOE_WIN + N_EXPERTS * SEG
ROW_TILE = 256
TAIL_TILES = (ROW_TILE // 2, ROW_TILE)
NOT_ROUTED = -1.0e6


def _top2(logits):
    lane = lax.broadcasted_iota(jnp.int32, logits.shape, 1).astype(F32)
    neg = jnp.float32(-jnp.inf)
    lg = jnp.where(lane < N_EXPERTS, logits, neg)
    m1 = jnp.max(lg, axis=1, keepdims=True)
    i1 = jnp.min(jnp.where(lg == m1, lane, float(LANES)), axis=1, keepdims=True)
    lg2 = jnp.where(lane == i1, neg, lg)
    m2 = jnp.max(lg2, axis=1, keepdims=True)
    i2 = jnp.min(jnp.where(lg2 == m2, lane, float(LANES)), axis=1, keepdims=True)
    ex = jnp.exp(m2 - m1)
    w1 = 1.0 / (1.0 + ex)
    w2 = ex / (1.0 + ex)
    gates = jnp.where(lane == i1, w1, 0.0) + jnp.where(lane == i2, w2, 0.0)
    mask = jnp.where((lane == i1) | (lane == i2), 1.0, 0.0)
    return gates, mask


def _pre_moe_kernel(*refs, n_res, n_win):
    (ya_ref, yb_ref, ga_ref, gb_ref, wout_ref, gf_ref, rwh_ref, rwl_ref, rb_ref,
     x1_ref, hn_ref, gates_ref, d_ref, dt_ref, cnt_ref) = refs[n_res:]
    x1 = _out_proj(refs[:n_res], ya_ref, yb_ref, ga_ref, gb_ref, wout_ref, slice(None))
    x1_ref[...] = x1
    hn = _rms(x1, gf_ref[...])
    hn_ref[...] = hn.astype(BF16)
    hi, lo = _split_bf16(hn)
    logits = _dot(hi, rwh_ref[...]) + _dot(lo, rwh_ref[...]) + _dot(hi, rwl_ref[...])
    gates, mask = _top2(logits + rb_ref[...])
    gates_ref[...] = gates
    row = lax.broadcasted_iota(jnp.int32, (MOE_WIN, MOE_WIN), 0)
    col = lax.broadcasted_iota(jnp.int32, (MOE_WIN, MOE_WIN), 1)
    earlier = jnp.where(col < row, 1.0, 0.0).astype(BF16)
    erow = lax.broadcasted_iota(jnp.int32, (LANES, LANES), 0)
    ecol = lax.broadcasted_iota(jnp.int32, (LANES, LANES), 1)
    lower_expert = jnp.where(erow < ecol, 1.0, 0.0).astype(BF16)
    lane = lax.broadcasted_iota(jnp.int32, (MOE_WIN, LANES), 1)
    for w in range(n_win):
        rs = slice(w * MOE_WIN, (w + 1) * MOE_WIN)
        mw = mask[rs, :]
        rank = _dot(earlier, mw.astype(BF16))
        cnt = jnp.broadcast_to(jnp.sum(mw, axis=0, keepdims=True), (SUBLANES, LANES))
        cnt_ref[w] = cnt
        padded = jnp.floor((cnt + (SEG - 1)) * (1.0 / SEG)) * SEG
        seg_start = _dot(padded.astype(BF16), lower_expert)[0:1, :]
        dest = rank + seg_start
        d_lo = jnp.min(jnp.where(mw > 0.0, dest, -NOT_ROUTED), axis=1, keepdims=True)
        d_hi = jnp.max(jnp.where(mw > 0.0, dest, NOT_ROUTED), axis=1, keepdims=True)
        d = jnp.where(lane == 0, d_lo, jnp.where(lane == 1, d_hi, 0.0))
        d_ref[rs, :] = d
        dt_ref[w] = d.T[:SUBLANES, :]


def _pre_moe(xs, ya, yb, ga, gb, wout, gf, rwh, rwl, rb, *, l, tm):
    m = xs[0].shape[0]
    n_win = tm // MOE_WIN
    j = l // 2
    lay3 = lambda i: (l, 0, 0)
    moe3 = lambda i: (j, 0, 0)
    return pl.pallas_call(
        functools.partial(_pre_moe_kernel, n_res=len(xs), n_win=n_win),
        grid=(m // tm,),
        in_specs=[pl.BlockSpec((tm, D_MODEL), lambda i: (i, 0)) for _ in xs] + [
            pl.BlockSpec((tm, MIX_A), lambda i: (i, 0)),
            pl.BlockSpec((tm, MIX_B), lambda i: (i, 0)),
            pl.BlockSpec((None, 1, MIX_A), lay3),
            pl.BlockSpec((None, 1, MIX_B), lay3),
            pl.BlockSpec((None, D_MODEL, D_MODEL), lay3),
            pl.BlockSpec((None, 1, D_MODEL), lay3),
            pl.BlockSpec((None, D_MODEL, LANES), moe3),
            pl.BlockSpec((None, D_MODEL, LANES), moe3),
            pl.BlockSpec((None, 1, LANES), moe3),
        ],
        out_specs=[
            pl.BlockSpec((tm, D_MODEL), lambda i: (i, 0)),
            pl.BlockSpec((tm, D_MODEL), lambda i: (i, 0)),
            pl.BlockSpec((tm, LANES), lambda i: (i, 0)),
            pl.BlockSpec((tm, LANES), lambda i: (i, 0)),
            pl.BlockSpec((n_win, SUBLANES, MOE_WIN), lambda i: (i, 0, 0)),
            pl.BlockSpec((n_win, SUBLANES, LANES), lambda i: (i, 0, 0)),
        ],
        out_shape=[
            jax.ShapeDtypeStruct((m, D_MODEL), F32),
            jax.ShapeDtypeStruct((m, D_MODEL), BF16),
            jax.ShapeDtypeStruct((m, LANES), F32),
            jax.ShapeDtypeStruct((m, LANES), F32),
            jax.ShapeDtypeStruct((m // MOE_WIN, SUBLANES, MOE_WIN), F32),
            jax.ShapeDtypeStruct((m // MOE_WIN, SUBLANES, LANES), F32),
        ],
        compiler_params=pltpu.CompilerParams(
            dimension_semantics=("parallel",), vmem_limit_bytes=VMEM_LIMIT),
        name="pre_moe",
    )(*xs, ya, yb, ga, gb, wout, gf, rwh, rwl, rb)


def _moe_kernel(cnt_sm, hn_ref, gates_ref, d_ref, dt_ref, wg_ref, wu_ref, wd_ref, x1_hbm, gfin_ref, o_ref,
                xs, gs, sb, gsb, x1buf, x1sem, pn_sm, s_sm, off_sm, est_sm, tot_sm, *, n_win, final):
    blk = pl.program_id(0)
    e = pl.program_id(1)

    def x1_copy(w, slot):
        row0 = pl.multiple_of((blk * n_win + w) * MOE_WIN, MOE_WIN)
        return pltpu.make_async_copy(x1_hbm.at[pl.ds(row0, MOE_WIN), :], x1buf.at[slot], x1sem.at[slot])

    def seg_copy(w, ee, to_sorted):
        s0 = s_sm[w * N_EXPERTS + ee]
        o0 = off_sm[w * N_EXPERTS + ee]

        def body(i, carry):
            src = pl.multiple_of(s0 + i * SEG, SEG)
            dst = pl.multiple_of(o0 + i * SEG, SEG)
            if to_sorted:
                xs[pl.ds(dst, SEG), :] = sb[pl.ds(src, SEG), :]
                gs[pl.ds(dst, SEG), :] = gsb[pl.ds(src, SEG), :]
            else:
                sb[pl.ds(src, SEG), :] = xs[pl.ds(dst, SEG), :]
            return carry

        lax.fori_loop(0, pn_sm[w * N_EXPERTS + ee] // SEG, body, 0)

    @pl.when(e == 0)
    def _dispatch():
        pn = [[None] * N_EXPERTS for _ in range(n_win)]
        for w in range(n_win):
            run = jnp.int32(0)
            for ee in range(N_EXPERTS):
                n = cnt_sm[(blk * n_win + w) * N_EXPERTS + ee]
                pn[w][ee] = jnp.bitwise_and(n + (SEG - 1), -SEG)
                pn_sm[w * N_EXPERTS + ee] = pn[w][ee]
                s_sm[w * N_EXPERTS + ee] = run
                run = run + pn[w][ee]
        run = jnp.int32(0)
        for ee in range(N_EXPERTS):
            est_sm[ee] = run
            start = run
            for w in range(n_win):
                off_sm[w * N_EXPERTS + ee] = run
                run = run + pn[w][ee]
            tot_sm[ee] = run - start
        end = pl.multiple_of(run, SEG)
        xs[pl.ds(end, ROW_TILE), :] = jnp.zeros((ROW_TILE, D_MODEL), BF16)
        gs[pl.ds(end, ROW_TILE), :] = jnp.zeros((ROW_TILE, LANES), F32)
        riota = lax.broadcasted_iota(jnp.int32, (SORT_ROWS, MOE_WIN), 0).astype(F32)
        for w in range(n_win):
            rs = slice(w * MOE_WIN, (w + 1) * MOE_WIN)
            g = jnp.where(riota == dt_ref[w, 0:1, :], 1.0, jnp.where(riota == dt_ref[w, 1:2, :], 1.0, 0.0))
            gb = g.astype(BF16)
            sb[...] = _dot(gb, hn_ref[rs, :]).astype(BF16)
            gh, gl = _split_bf16(gates_ref[rs, :])
            gsb[...] = _dot(gb, gh) + _dot(gb, gl)
            for ee in range(N_EXPERTS):
                seg_copy(w, ee, True)

    start = est_sm[e]
    tot = tot_sm[e]

    def row_tile(r0, size, valid):
        r0 = pl.multiple_of(r0, SEG)
        xt = xs[pl.ds(r0, size), :]
        y = _swiglu(xt, wg_ref, wu_ref, wd_ref)
        lane = lax.broadcasted_iota(jnp.int32, (size, LANES), 1)
        gate = jnp.sum(jnp.where(lane == e, gs[pl.ds(r0, size), :], 0.0), axis=1, keepdims=True)
        keep = lax.broadcasted_iota(jnp.int32, (size, D_MODEL), 0) < valid
        xs[pl.ds(r0, size), :] = jnp.where(keep, (y * gate).astype(BF16), xt)

    n_full = tot // ROW_TILE

    def full_body(i, carry):
        row_tile(start + i * ROW_TILE, ROW_TILE, ROW_TILE)
        return carry

    lax.fori_loop(0, n_full, full_body, 0)
    rem = tot - n_full * ROW_TILE
    tail = start + n_full * ROW_TILE

    lo = 0
    for size in TAIL_TILES:
        @pl.when((rem > lo) & (rem <= size))
        def _(size=size):
            row_tile(tail, size, rem)
        lo = size

    @pl.when(e == N_EXPERTS - 1)
    def _combine():
        liota = lax.broadcasted_iota(jnp.int32, (MOE_WIN, SORT_ROWS), 1).astype(F32)
        x1_copy(0, 0).start()
        for w in range(n_win):
            rs = slice(w * MOE_WIN, (w + 1) * MOE_WIN)
            slot = w % 2
            if w + 1 < n_win:
                x1_copy(w + 1, 1 - slot).start()
            for ee in range(N_EXPERTS):
                seg_copy(w, ee, False)
            g = jnp.where(liota == d_ref[rs, 0:1], 1.0, jnp.where(liota == d_ref[rs, 1:2], 1.0, 0.0))
            f = _dot(g.astype(BF16), sb[...])
            x1_copy(w, slot).wait()
            x2 = x1buf[slot] + f
            o_ref[rs, :] = _rms(x2, gfin_ref[...]) if final else x2


def _moe(cnt, hn, gates, d, dt, wg, wu, wd, x1, gfin, *, l, tb, final):
    m = hn.shape[0]
    n_win = tb // MOE_WIN
    j = l // 2
    xs_rows = 2 * tb + n_win * N_EXPERTS * SEG + ROW_TILE
    one = pl.Buffered(1)
    wsel = lambda i, e, c: (j * N_EXPERTS + e, 0, 0)
    grid_spec = pltpu.PrefetchScalarGridSpec(
        num_scalar_prefetch=1,
        grid=(m // tb, N_EXPERTS),
        in_specs=[
            pl.BlockSpec((tb, D_MODEL), lambda i, e, c: (i, 0), pipeline_mode=one),
            pl.BlockSpec((tb, LANES), lambda i, e, c: (i, 0), pipeline_mode=one),
            pl.BlockSpec((tb, LANES), lambda i, e, c: (i, 0), pipeline_mode=one),
            pl.BlockSpec((n_win, SUBLANES, MOE_WIN), lambda i, e, c: (i, 0, 0), pipeline_mode=one),
            pl.BlockSpec((None, D_MODEL, FF_PAD), wsel),
            pl.BlockSpec((None, D_MODEL, FF_PAD), wsel),
            pl.BlockSpec((None, FF_PAD, D_MODEL), wsel),
            pl.BlockSpec(memory_space=pl.ANY),
            pl.BlockSpec((1, D_MODEL), lambda i, e, c: (0, 0)),
        ],
        out_specs=pl.BlockSpec((tb, D_MODEL), lambda i, e, c: (i, 0), pipeline_mode=one),
        scratch_shapes=[
            pltpu.VMEM((xs_rows, D_MODEL), BF16),
            pltpu.VMEM((xs_rows, LANES), F32),
            pltpu.VMEM((SORT_ROWS, D_MODEL), BF16),
            pltpu.VMEM((SORT_ROWS, LANES), F32),
            pltpu.VMEM((2, MOE_WIN, D_MODEL), F32),
            pltpu.SemaphoreType.DMA((2,)),
            pltpu.SMEM((n_win * N_EXPERTS,), jnp.int32),
            pltpu.SMEM((n_win * N_EXPERTS,), jnp.int32),
            pltpu.SMEM((n_win * N_EXPERTS,), jnp.int32),
            pltpu.SMEM((N_EXPERTS,), jnp.int32),
            pltpu.SMEM((N_EXPERTS,), jnp.int32),
        ],
    )
    return pl.pallas_call(
        functools.partial(_moe_kernel, n_win=n_win, final=final),
        grid_spec=grid_spec,
        out_shape=jax.ShapeDtypeStruct((m, D_MODEL), F32),
        compiler_params=pltpu.CompilerParams(
            dimension_semantics=("arbitrary", "arbitrary"), vmem_limit_bytes=MOE_VMEM_LIMIT),
        name="moe",
    )(cnt, hn, gates, d, dt, wg, wu, wd, x1, gfin)


def kernel(x_prompt, x_sample, state_ssm_re, state_ssm_im, norm_mix, w_in, ssm_a_re, ssm_a_im, ssm_log_dt, ssm_b_re, ssm_b_im, ssm_c_re, ssm_c_im, ssm_d, glu_w, glu_b, sgu_w, sgu_b, sgu_ln_g, sgu_ln_b, out_norm_a, out_norm_b, w_out, norm_ffn, ffn_w_gate, ffn_w_up, ffn_w_down, router_w, router_b, moe_w_gate, moe_w_up, moe_w_down, norm_final):
    batch, seq = x_prompt.shape[0], x_prompt.shape[1]
    dec_batch, dec_seq = x_sample.shape[0], x_sample.shape[1]
    mp, ms = batch * seq, dec_batch * dec_seq
    n_moe = moe_w_gate.shape[0]

    lam, bbre, bbim = _s5_prep(ssm_a_re, ssm_a_im, ssm_log_dt, ssm_b_re, ssm_b_im)
    ctre = jnp.swapaxes(ssm_c_re, 2, 3).reshape(DEPTH * N_STATE, SSM_GROUP)
    ctim = jnp.swapaxes(ssm_c_im, 2, 3).reshape(DEPTH * N_STATE, SSM_GROUP)
    dskip = ssm_d.reshape(DEPTH, 1, MIX_A)
    glu = glu_w.reshape(DEPTH * MIX_A, 2 * SSM_GROUP)
    gbias = jnp.stack([glu_b[..., :SSM_GROUP].reshape(DEPTH, MIX_A),
                       glu_b[..., SSM_GROUP:].reshape(DEPTH, MIX_A)], axis=1)
    s5w = (lam, bbre, bbim, ctre, ctim, dskip, glu, gbias)
    w_in_b = w_in.astype(BF16)
    w_out_b = w_out.astype(BF16)
    g_mix = norm_mix.reshape(DEPTH, 1, D_MODEL)
    g_a = out_norm_a.reshape(DEPTH, 1, MIX_A)
    g_b = out_norm_b.reshape(DEPTH, 1, MIX_B)
    g_ffn = norm_ffn.reshape(DEPTH, 1, D_MODEL)
    sgu_bias = jnp.repeat(jnp.swapaxes(sgu_b, 1, 2), HD_B, axis=2)
    sgu_bias_s = sgu_bias[:, :dec_seq]
    lng = sgu_ln_g.reshape(DEPTH, 1, MIX_B)
    lnb = sgu_ln_b.reshape(DEPTH, 1, MIX_B)
    coef_s = jnp.repeat(
        jnp.transpose(sgu_w[:, :, :dec_seq, :dec_seq], (0, 2, 3, 1)).reshape(DEPTH, dec_seq * dec_seq, NH_B),
        HD_B, axis=2)
    dense_split = (FF_PAD, D_FF - FF_PAD)
    dense_g = _pack(jnp.swapaxes(ffn_w_gate, 1, 2), dense_split, transpose=True)
    dense_u = _pack(jnp.swapaxes(ffn_w_up, 1, 2), dense_split, transpose=True)
    dense_d = _pack(ffn_w_down, dense_split, transpose=False)
    moe_t = lambda w: jnp.swapaxes(w.reshape(-1, D_MODEL, D_FF_EXPERT), 1, 2)
    moe_g = _pack(moe_t(moe_w_gate), (D_FF_EXPERT,), transpose=True)
    moe_u = _pack(moe_t(moe_w_up), (D_FF_EXPERT,), transpose=True)
    moe_d = _pack(moe_w_down.reshape(-1, D_FF_EXPERT, D_MODEL), (D_FF_EXPERT,), transpose=False)
    rw = jnp.pad(router_w, ((0, 0), (0, 0), (0, LANES - N_EXPERTS)))
    rwh = rw.astype(BF16)
    rwl = (rw - rwh.astype(F32)).astype(BF16)
    rb = jnp.pad(router_b, ((0, 0), (0, LANES - N_EXPERTS))).reshape(n_moe, 1, LANES)
    gfin = norm_final.reshape(1, D_MODEL)

    h0re = state_ssm_re.reshape(DEPTH, dec_batch, N_STATE)
    h0im = state_ssm_im.reshape(DEPTH, dec_batch, N_STATE)

    assert DEPTH % 2 == 0

    def post_mixer(xres, ya, yb, l, tm, tb):
        common = (g_a, g_b, w_out_b, g_ffn)
        if l % 2 == 0:
            return (_ffn(xres, ya, yb, *common, dense_g, dense_u, dense_d, l=l, tm=tm),)
        x1, hn, gates, d, dt, cnt = _pre_moe(xres, ya, yb, *common, rwh, rwl, rb, l=l, tm=min(2 * tm, tb))
        cnt = cnt[:, 0, :N_EXPERTS].astype(jnp.int32).reshape(-1)
        return (_moe(cnt, hn, gates, d, dt, moe_g, moe_u, moe_d, x1, gfin, l=l, tb=tb, final=(l == DEPTH - 1)),)

    xp = (x_prompt.reshape(mp, D_MODEL),)
    xs = (x_sample.reshape(ms, D_MODEL),)
    re_p, im_p, re_s, im_s, v_s = [], [], [], [], []
    for l in range(DEPTH):
        ya_p, yb_p, hre_p, him_p = _mixer(xp[0].reshape(batch, seq, D_MODEL), g_mix, w_in_b, sgu_w, sgu_bias,
                                          lng, lnb, s5w, l)
        zs = _inproj(xs, g_mix, w_in_b, l, tm=ms)
        ya_s, hre_s, him_s = _s5_sample(zs, h0re, h0im, s5w, l, nb=dec_batch, t_blk=dec_seq)
        yb_s, vn_s = _sgu_sample(zs, coef_s, sgu_bias_s, lng, lnb, l, nb=dec_batch, t_blk=dec_seq)
        xp = post_mixer(xp, ya_p.reshape(mp, MIX_A), yb_p.reshape(mp, MIX_B), l, tm=512, tb=seq)
        xs = post_mixer(xs, ya_s, yb_s, l, tm=ms, tb=ms)
        re_p.append(hre_p)
        im_p.append(him_p)
        re_s.append(hre_s)
        im_s.append(him_s)
        v_s.append(vn_s)
    xp, xs = xp[0], xs[0]

    st = lambda hs, b: jnp.stack(hs).reshape(DEPTH, b, NG_A, SSM_STATE)
    return (xp.reshape(batch, seq, D_MODEL), xs.reshape(dec_batch, dec_seq, D_MODEL),
            st(re_p, batch), st(im_p, batch), st(re_s, dec_batch), st(im_s, dec_batch),
            jnp.stack(v_s).reshape(DEPTH, dec_batch, dec_seq, NH_B, HD_B))
```

```python
import functools
import math

import jax
import jax.numpy as jnp
from jax import lax
from jax.experimental import pallas as pl
from jax.experimental.pallas import tpu as pltpu

F32 = jnp.float32
BF16 = jnp.bfloat16

D_MODEL = 1024
DEPTH = 4
MIX_A = 512
SSM_GROUP = 16
NG_A = 32
SSM_STATE = 64
N_STATE = NG_A * SSM_STATE
MIX_B = 512
HD_B = 64
NH_B = 8
CHUNK = 128
IN_WIDTH = MIX_A + 2 * MIX_B
D_FF = 2752
N_EXPERTS = 8
D_FF_EXPERT = D_FF // 2
FF_PAD = 1408
EPS = 1e-6

LANES = 128
SUBLANES = 8
MXU_DIM = 256
STATE_COLS = 512
N_SCOL = N_STATE // STATE_COLS
GROUPS_PER_CHUNK = STATE_COLS // SSM_STATE
VMEM_LIMIT = 52 * 1024 * 1024
MOE_VMEM_LIMIT = 58 * 1024 * 1024


def _rms(x, g):
    return x * lax.rsqrt(jnp.mean(x * x, axis=-1, keepdims=True) + EPS) * g


def _gelu(x):
    c = math.sqrt(2.0 / math.pi)
    return 0.5 * x * (1.0 + jnp.tanh(c * (x + 0.044715 * (x * x * x))))


def _sigmoid(x):
    return 1.0 / (1.0 + jnp.exp(-x))


def _split_bf16(x):
    hi = x.astype(BF16)
    lo = (x - hi.astype(F32)).astype(BF16)
    return hi, lo


def _dot(a, b):
    return jnp.dot(a, b, preferred_element_type=F32)


def _log2(n):
    assert n & (n - 1) == 0
    return n.bit_length() - 1


def _block_diag_tile(src, n_rep, rows_per_group):
    r_n, k = src.shape
    n = k * n_rep
    t_row = lax.broadcasted_iota(jnp.int32, (k, n), 0)
    t_col = lax.broadcasted_iota(jnp.int32, (k, n), 1)
    tile = jnp.where(jnp.bitwise_and(t_col, k - 1) == t_row, 1.0, 0.0).astype(BF16)
    full = _dot(src.astype(BF16), tile)
    row = lax.broadcasted_iota(jnp.int32, (r_n, n), 0)
    col = lax.broadcasted_iota(jnp.int32, (r_n, n), 1)
    same = lax.shift_right_logical(row, _log2(rows_per_group)) == lax.shift_right_logical(col, _log2(k))
    return jnp.where(same, full, 0.0).astype(BF16)


def _pack_kernel(w_ref, o_ref, *, widths, transpose):
    for i, width in enumerate(widths):
        w = w_ref[i * FF_PAD:i * FF_PAD + width, :]
        if width < FF_PAD:
            w = jnp.concatenate([w, jnp.zeros((FF_PAD - width, w.shape[1]), F32)], axis=0)
        o_ref[i] = (w.T if transpose else w).astype(BF16)


def _pack(w, widths, transpose):
    n_e, n_parts = w.shape[0], len(widths)
    out_blk = (n_parts, D_MODEL, FF_PAD) if transpose else (n_parts, FF_PAD, D_MODEL)
    out = pl.pallas_call(
        functools.partial(_pack_kernel, widths=widths, transpose=transpose),
        grid=(n_e,),
        in_specs=[pl.BlockSpec((None,) + w.shape[1:], lambda e: (e, 0, 0))],
        out_specs=pl.BlockSpec((None,) + out_blk, lambda e: (e, 0, 0, 0)),
        out_shape=jax.ShapeDtypeStruct((n_e,) + out_blk, BF16),
        compiler_params=pltpu.CompilerParams(
            dimension_semantics=("parallel",), vmem_limit_bytes=VMEM_LIMIT),
        name="pack_t" if transpose else "pack",
    )(w)
    return out.reshape((n_e * n_parts,) + out_blk[1:])


def _s5_prep_kernel(are_ref, aim_ref, ldt_ref, bre_ref, bim_ref,
                    abre_ref, abim_ref, bbre_ref, bbim_ref):
    lam_re = are_ref[...]
    lam_im = aim_ref[...]
    dt = jnp.exp(ldt_ref[...])
    mag = jnp.exp(lam_re * dt)
    ab_re = mag * jnp.cos(lam_im * dt)
    ab_im = mag * jnp.sin(lam_im * dt)
    den = lam_re * lam_re + lam_im * lam_im
    nr = ab_re - 1.0
    q_re = (nr * lam_re + ab_im * lam_im) / den
    q_im = (ab_im * lam_re - nr * lam_im) / den
    b_re = bre_ref[...]
    b_im = bim_ref[...]
    abre_ref[...] = ab_re
    abim_ref[...] = ab_im
    bbre_ref[...] = q_re * b_re - q_im * b_im
    bbim_ref[...] = q_re * b_im + q_im * b_re


def _s5_prep(ssm_a_re, ssm_a_im, ssm_log_dt, ssm_b_re, ssm_b_im):
    shp = (DEPTH, NG_A, SSM_GROUP, SSM_STATE)
    rows = DEPTH * NG_A * SSM_GROUP
    bc = lambda a: jnp.broadcast_to(a, shp).reshape(rows, SSM_STATE)
    are = bc(ssm_a_re[:, :, None, :])
    aim = bc(ssm_a_im[:, :, None, :])
    ldt = bc(ssm_log_dt[:, :, None, None])
    bre = jnp.swapaxes(ssm_b_re, 2, 3).reshape(rows, SSM_STATE)
    bim = jnp.swapaxes(ssm_b_im, 2, 3).reshape(rows, SSM_STATE)
    sds = jax.ShapeDtypeStruct((rows, SSM_STATE), F32)
    abre, abim, bbre, bbim = pl.pallas_call(
        _s5_prep_kernel, out_shape=(sds, sds, sds, sds), name="s5_prep",
    )(are, aim, ldt, bre, bim)
    lam_re = abre.reshape(shp)[:, :, 0, :].reshape(DEPTH, 1, N_STATE)
    lam_im = abim.reshape(shp)[:, :, 0, :].reshape(DEPTH, 1, N_STATE)
    lam = jnp.concatenate([lam_re, lam_im], axis=1)
    return lam, bbre, bbim


def _inproj_kernel(x_ref, g_ref, w_ref, z_ref):
    hn = _rms(x_ref[...], g_ref[...])
    z_ref[...] = _dot(hn.astype(BF16), w_ref[...])


def _inproj(x, g, w, l, tm):
    m = x.shape[0]
    return pl.pallas_call(
        _inproj_kernel,
        grid=(m // tm,),
        in_specs=[
            pl.BlockSpec((tm, D_MODEL), lambda i: (i, 0)),
            pl.BlockSpec((None, 1, D_MODEL), lambda i: (l, 0, 0)),
            pl.BlockSpec((None, D_MODEL, IN_WIDTH), lambda i: (l, 0, 0)),
        ],
        out_specs=pl.BlockSpec((tm, IN_WIDTH), lambda i: (i, 0)),
        out_shape=jax.ShapeDtypeStruct((m, IN_WIDTH), F32),
        compiler_params=pltpu.CompilerParams(
            dimension_semantics=("parallel",), vmem_limit_bytes=VMEM_LIMIT),
        name="inproj",
    )(x, g, w)


def _s5_expand(bbre_ref, bbim_ref, ctre_ref, ctim_ref, glu_ref, bblk, cre, cimn, wa, wg):
    for c in range(4):
        rs = slice(c * LANES, (c + 1) * LANES)
        bblk[c, :, :STATE_COLS] = _block_diag_tile(bbre_ref[rs, :], GROUPS_PER_CHUNK, SSM_GROUP)
        bblk[c, :, STATE_COLS:] = _block_diag_tile(bbim_ref[rs, :], GROUPS_PER_CHUNK, SSM_GROUP)
        ss = slice(c * STATE_COLS, (c + 1) * STATE_COLS)
        cre[c] = _block_diag_tile(ctre_ref[ss, :], GROUPS_PER_CHUNK, SSM_STATE)
        cimn[c] = _block_diag_tile(-ctim_ref[ss, :], GROUPS_PER_CHUNK, SSM_STATE)
    for b in range(MIX_A // MXU_DIM):
        ms = slice(b * MXU_DIM, (b + 1) * MXU_DIM)
        wa[b] = _block_diag_tile(glu_ref[ms, :SSM_GROUP], MXU_DIM // SSM_GROUP, SSM_GROUP)
        wg[b] = _block_diag_tile(glu_ref[ms, SSM_GROUP:], MXU_DIM // SSM_GROUP, SSM_GROUP)


def _s5_time_major(utm, lam_ref, d_ref, gbias_ref, hre_ref, him_ref, bre, bim, bblk, cre, cimn, wa, wg,
                   nb, t_blk):
    for c in range(4):
        r = _dot(utm[c].astype(BF16), bblk[c])
        bre[:, c * STATE_COLS:(c + 1) * STATE_COLS] = r[:, :STATE_COLS]
        bim[:, c * STATE_COLS:(c + 1) * STATE_COLS] = r[:, STATE_COLS:]

    for c in range(N_SCOL):
        cs = slice(c * STATE_COLS, (c + 1) * STATE_COLS)
        lr = jnp.broadcast_to(lam_ref[0:1, cs], (SUBLANES, STATE_COLS))
        li = jnp.broadcast_to(lam_ref[1:2, cs], (SUBLANES, STATE_COLS))

        if nb == SUBLANES:
            hr, hi = hre_ref[:, cs], him_ref[:, cs]
            for t in range(t_blk):
                rs = slice(t * SUBLANES, (t + 1) * SUBLANES)
                hr, hi = lr * hr - li * hi + bre[rs, cs], lr * hi + li * hr + bim[rs, cs]
                bre[rs, cs] = hr
                bim[rs, cs] = hi
            hre_ref[:, cs] = hr
            him_ref[:, cs] = hi
            continue

        def group_body(bg, carry, cs=cs, lr=lr, li=li):
            r0 = pl.multiple_of(bg * SUBLANES, SUBLANES)

            def t_body(t, h):
                hr, hi = h
                row = pl.multiple_of(t * nb + r0, SUBLANES)
                nr = lr * hr - li * hi + bre[pl.ds(row, SUBLANES), cs]
                ni = lr * hi + li * hr + bim[pl.ds(row, SUBLANES), cs]
                bre[pl.ds(row, SUBLANES), cs] = nr
                bim[pl.ds(row, SUBLANES), cs] = ni
                return nr, ni

            h0 = (hre_ref[pl.ds(r0, SUBLANES), cs], him_ref[pl.ds(r0, SUBLANES), cs])
            hr, hi = lax.fori_loop(0, t_blk, t_body, h0, unroll=min(t_blk, 8))
            hre_ref[pl.ds(r0, SUBLANES), cs] = hr
            him_ref[pl.ds(r0, SUBLANES), cs] = hi
            return carry

        lax.fori_loop(0, nb // SUBLANES, group_body, 0)

    ys = []
    for c in range(4):
        cs = slice(c * STATE_COLS, (c + 1) * STATE_COLS)
        ls = slice(c * LANES, (c + 1) * LANES)
        y = _dot(bre[:, cs].astype(BF16), cre[c]) + _dot(bim[:, cs].astype(BF16), cimn[c])
        ys.append(_gelu(y + d_ref[:, ls] * utm[c]).astype(BF16))

    os_ = []
    for b in range(MIX_A // MXU_DIM):
        ms = slice(b * MXU_DIM, (b + 1) * MXU_DIM)
        yb = jnp.concatenate(ys[2 * b:2 * b + 2], axis=1)
        za = _dot(yb, wa[b]) + gbias_ref[0:1, ms]
        zg = _dot(yb, wg[b]) + gbias_ref[1:2, ms]
        os_.append(za * _sigmoid(zg))
    return jnp.concatenate(os_, axis=1)


def _s5_kernel(*refs, nb, t_blk):
    (u_ref, h0re_ref, h0im_ref, bbre_ref, bbim_ref, lam_ref, ctre_ref, ctim_ref, d_ref, glu_ref, gbias_ref,
     ya_ref, hre_ref, him_ref, utm, bre, bim, otm, bblk, cre, cimn, wa, wg, tmp) = refs
    hre_ref[...] = h0re_ref[...]
    him_ref[...] = h0im_ref[...]
    _s5_expand(bbre_ref, bbim_ref, ctre_ref, ctim_ref, glu_ref, bblk, cre, cimn, wa, wg)

    for c in range(4):
        ls = slice(c * LANES, (c + 1) * LANES)
        tmp[c] = u_ref[:, ls]
        for t in range(t_blk):
            utm[c, t * nb:(t + 1) * nb, :] = tmp[c, pl.ds(t, nb, stride=t_blk), :]

    o = _s5_time_major(utm, lam_ref, d_ref, gbias_ref, hre_ref, him_ref, bre, bim, bblk, cre, cimn, wa, wg,
                       nb, t_blk)

    for c in range(4):
        ls = slice(c * LANES, (c + 1) * LANES)
        for t in range(t_blk):
            otm[c, pl.ds(t, nb, stride=t_blk), :] = o[t * nb:(t + 1) * nb, ls]
        ya_ref[:, ls] = otm[c]


def _s5_weight_specs(l):
    return [
        pl.BlockSpec((NG_A * SSM_GROUP, SSM_STATE), lambda j: (l, 0)),
        pl.BlockSpec((NG_A * SSM_GROUP, SSM_STATE), lambda j: (l, 0)),
        pl.BlockSpec((None, 2, N_STATE), lambda j: (l, 0, 0)),
        pl.BlockSpec((N_STATE, SSM_GROUP), lambda j: (l, 0)),
        pl.BlockSpec((N_STATE, SSM_GROUP), lambda j: (l, 0)),
        pl.BlockSpec((None, 1, MIX_A), lambda j: (l, 0, 0)),
        pl.BlockSpec((MIX_A, 2 * SSM_GROUP), lambda j: (l, 0)),
        pl.BlockSpec((None, 2, MIX_A), lambda j: (l, 0, 0)),
    ]


def _s5_scratch(rows):
    return [
        pltpu.VMEM((4, rows, LANES), F32),
        pltpu.VMEM((rows, N_STATE), F32),
        pltpu.VMEM((rows, N_STATE), F32),
        pltpu.VMEM((4, rows, LANES), F32),
        pltpu.VMEM((4, LANES, 2 * STATE_COLS), BF16),
        pltpu.VMEM((4, STATE_COLS, LANES), BF16),
        pltpu.VMEM((4, STATE_COLS, LANES), BF16),
        pltpu.VMEM((MIX_A // MXU_DIM, MXU_DIM, MXU_DIM), BF16),
        pltpu.VMEM((MIX_A // MXU_DIM, MXU_DIM, MXU_DIM), BF16),
    ]


def _s5_sample(z, h0re, h0im, weights, l, nb, t_blk):
    rows = nb * t_blk
    lam, bbre, bbim, ctre, ctim, dskip, glu, gbias = weights
    blk = pl.BlockSpec((rows, MIX_A), lambda j: (0, 0))
    st_in = pl.BlockSpec((None, nb, N_STATE), lambda j: (l, 0, 0))
    st_out = pl.BlockSpec((nb, N_STATE), lambda j: (0, 0))
    return pl.pallas_call(
        functools.partial(_s5_kernel, nb=nb, t_blk=t_blk),
        grid=(1,),
        in_specs=[blk, st_in, st_in] + _s5_weight_specs(l),
        out_specs=[blk, st_out, st_out],
        out_shape=[
            jax.ShapeDtypeStruct((rows, MIX_A), F32),
            jax.ShapeDtypeStruct((nb, N_STATE), F32),
            jax.ShapeDtypeStruct((nb, N_STATE), F32),
        ],
        scratch_shapes=_s5_scratch(rows) + [pltpu.VMEM((4, rows, LANES), F32)],
        compiler_params=pltpu.CompilerParams(
            dimension_semantics=("arbitrary",), vmem_limit_bytes=VMEM_LIMIT),
        name="s5_sample",
    )(z, h0re, h0im, bbre, bbim, lam, ctre, ctim, dskip, glu, gbias)


def _group_layernorm(gv, lng_ref, lnb_ref):
    row = lax.broadcasted_iota(jnp.int32, (MXU_DIM, MXU_DIM), 0)
    col = lax.broadcasted_iota(jnp.int32, (MXU_DIM, MXU_DIM), 1)
    same = lax.shift_right_logical(row, _log2(HD_B)) == lax.shift_right_logical(col, _log2(HD_B))
    gm = jnp.where(same, 1.0 / HD_B, 0.0).astype(BF16)

    def gmean(a):
        hi, lo = _split_bf16(a)
        parts = []
        for b in range(MIX_B // MXU_DIM):
            ms = slice(b * MXU_DIM, (b + 1) * MXU_DIM)
            parts.append(_dot(hi[:, ms], gm) + _dot(lo[:, ms], gm))
        return jnp.concatenate(parts, axis=1)

    xc = gv - gmean(gv)
    var = gmean(xc * xc)
    return xc * lax.rsqrt(var + EPS) * lng_ref[...] + lnb_ref[...]


def _mixer_kernel(x_ref, g_ref, w_ref, ws_ref, bias_ref, lng_ref, lnb_ref,
                  bbre_ref, bbim_ref, lam_ref, ctre_ref, ctim_ref, d_ref, glu_ref, gbias_ref,
                  ya_ref, yb_ref, hre_ref, him_ref,
                  hnb, ua, zb, utm, bre, bim, otm, bblk, cre, cimn, wa, wg, *, nb):
    @pl.when(pl.program_id(0) == 0)
    def _():
        hre_ref[...] = jnp.zeros_like(hre_ref)
        him_ref[...] = jnp.zeros_like(him_ref)
        _s5_expand(bbre_ref, bbim_ref, ctre_ref, ctim_ref, glu_ref, bblk, cre, cimn, wa, wg)

    for b in range(nb):
        hnb[b * CHUNK:(b + 1) * CHUNK, :] = _rms(x_ref[b], g_ref[...]).astype(BF16)
    ua[...] = _dot(hnb[...], w_ref[:, :MIX_A])
    for c in range(4):
        ls = slice(c * LANES, (c + 1) * LANES)
        for b in range(nb):
            utm[c, pl.ds(b, CHUNK, stride=nb), :] = ua[b * CHUNK:(b + 1) * CHUNK, ls]
    zb[...] = _dot(hnb[...], w_ref[:, MIX_A:])

    row = lax.broadcasted_iota(jnp.int32, (CHUNK, 2 * CHUNK), 0)
    col = lax.broadcasted_iota(jnp.int32, (CHUNK, 2 * CHUNK), 1)
    causal = jnp.bitwise_and(col, CHUNK - 1) <= row
    lane = lax.broadcasted_iota(jnp.int32, (CHUNK, LANES), 1)
    first_head = lane < HD_B
    wcat = []
    for p in range(NH_B // 2):
        w = jnp.concatenate([ws_ref[2 * p], ws_ref[2 * p + 1]], axis=1)
        wcat.append(jnp.where(causal, w, 0.0).astype(BF16))
    for b in range(nb):
        rs = slice(b * CHUNK, (b + 1) * CHUNK)
        u = _gelu(zb[rs, :MIX_B])
        vn = _group_layernorm(_gelu(zb[rs, MIX_B:]), lng_ref, lnb_ref)
        for p in range(NH_B // 2):
            ls = slice(p * LANES, (p + 1) * LANES)
            vp = vn[:, ls]
            rhs = jnp.concatenate(
                [jnp.where(first_head, vp, 0.0), jnp.where(first_head, 0.0, vp)], axis=0).astype(BF16)
            s = _dot(wcat[p], rhs)
            yb_ref[b, :, ls] = u[:, ls] * (s + bias_ref[:, ls])

    o = _s5_time_major(utm, lam_ref, d_ref, gbias_ref, hre_ref, him_ref, bre, bim, bblk, cre, cimn, wa, wg,
                       nb, CHUNK)
    for c in range(4):
        ls = slice(c * LANES, (c + 1) * LANES)
        otm[c] = o[:, ls]
        for b in range(nb):
            ya_ref[b, :, ls] = otm[c, pl.ds(b, CHUNK, stride=nb), :]


def _mixer(x3, g, w, ws, bias, lng, lnb, s5w, l):
    nb, seq = x3.shape[0], x3.shape[1]
    rows = nb * CHUNK
    lam, bbre, bbim, ctre, ctim, dskip, glu, gbias = s5w
    lay3 = lambda j: (l, 0, 0)
    blk = lambda width: pl.BlockSpec((nb, CHUNK, width), lambda j: (0, j, 0))
    st = pl.BlockSpec((nb, N_STATE), lambda j: (0, 0))
    return pl.pallas_call(
        functools.partial(_mixer_kernel, nb=nb),
        grid=(seq // CHUNK,),
        in_specs=[
            blk(D_MODEL),
            pl.BlockSpec((None, 1, D_MODEL), lay3),
            pl.BlockSpec((None, D_MODEL, IN_WIDTH), lay3, pipeline_mode=pl.Buffered(1)),
            pl.BlockSpec((None, NH_B, CHUNK, CHUNK), lambda j: (l, 0, 0, 0)),
            pl.BlockSpec((None, CHUNK, MIX_B), lay3),
            pl.BlockSpec((None, 1, MIX_B), lay3),
            pl.BlockSpec((None, 1, MIX_B), lay3),
        ] + _s5_weight_specs(l),
        out_specs=[blk(MIX_A), blk(MIX_B), st, st],
        out_shape=[
            jax.ShapeDtypeStruct((nb, seq, MIX_A), F32),
            jax.ShapeDtypeStruct((nb, seq, MIX_B), F32),
            jax.ShapeDtypeStruct((nb, N_STATE), F32),
            jax.ShapeDtypeStruct((nb, N_STATE), F32),
        ],
        scratch_shapes=[
            pltpu.VMEM((rows, D_MODEL), BF16),
            pltpu.VMEM((rows, MIX_A), F32),
            pltpu.VMEM((rows, 2 * MIX_B), F32),
        ] + _s5_scratch(rows),
        compiler_params=pltpu.CompilerParams(
            dimension_semantics=("arbitrary",), vmem_limit_bytes=MOE_VMEM_LIMIT),
        name="mixer",
    )(x3, g, w, ws, bias, lng, lnb, bbre, bbim, lam, ctre, ctim, dskip, glu, gbias)


def _sgu_sample_kernel(u_ref, v_ref, coef_ref, bias_ref, lng_ref, lnb_ref, o_ref, vn_ref,
                       usc, vsc, osc, *, nb, t_blk):
    vn = _group_layernorm(_gelu(v_ref[...]), lng_ref, lnb_ref)
    vn_ref[...] = vn
    u = _gelu(u_ref[...])
    for c in range(4):
        ls = slice(c * LANES, (c + 1) * LANES)
        vsc[c] = vn[:, ls]
        usc[c] = u[:, ls]
        for q in range(t_blk):
            s = bias_ref[q:q + 1, ls]
            for k in range(q + 1):
                s = s + coef_ref[q * t_blk + k:q * t_blk + k + 1, ls] * vsc[c, pl.ds(k, nb, stride=t_blk), :]
            osc[c, pl.ds(q, nb, stride=t_blk), :] = usc[c, pl.ds(q, nb, stride=t_blk), :] * s
        o_ref[:, ls] = osc[c]


def _sgu_sample(z, coef, bias, lng, lnb, l, nb, t_blk):
    m = nb * t_blk
    fix2 = lambda i: (0, 0)
    lay3 = lambda i: (l, 0, 0)
    return pl.pallas_call(
        functools.partial(_sgu_sample_kernel, nb=nb, t_blk=t_blk),
        grid=(1,),
        in_specs=[
            pl.BlockSpec((m, MIX_B), lambda i: (0, 1)),
            pl.BlockSpec((m, MIX_B), lambda i: (0, 2)),
            pl.BlockSpec((None, t_blk * t_blk, MIX_B), lay3),
            pl.BlockSpec((None, t_blk, MIX_B), lay3),
            pl.BlockSpec((None, 1, MIX_B), lay3),
            pl.BlockSpec((None, 1, MIX_B), lay3),
        ],
        out_specs=[pl.BlockSpec((m, MIX_B), fix2), pl.BlockSpec((m, MIX_B), fix2)],
        out_shape=[jax.ShapeDtypeStruct((m, MIX_B), F32), jax.ShapeDtypeStruct((m, MIX_B), F32)],
        scratch_shapes=[pltpu.VMEM((4, m, LANES), F32)] * 3,
        compiler_params=pltpu.CompilerParams(
            dimension_semantics=("arbitrary",), vmem_limit_bytes=VMEM_LIMIT),
        name="sgu_sample",
    )(z, z, coef, bias, lng, lnb)


def _swiglu(x, wg_ref, wu_ref, wd_ref):
    hg = _dot(x, wg_ref[...])
    a = (hg * _sigmoid(hg) * _dot(x, wu_ref[...])).astype(BF16)
    return _dot(a, wd_ref[...])


def _out_proj(x_ref, ya_ref, yb_ref, ga_ref, gb_ref, wout_ref):
    na = _rms(ya_ref[...], ga_ref[...]).astype(BF16)
    nb = _rms(yb_ref[...], gb_ref[...]).astype(BF16)
    return x_ref[...] + _dot(na, wout_ref[0:MIX_A, :]) + _dot(nb, wout_ref[MIX_A:, :])


N_DENSE_PARTS = 2


def _ffn_kernel(x_ref, ya_ref, yb_ref, ga_ref, gb_ref, wout_ref, gf_ref, wg_ref, wu_ref, wd_ref, o_ref):
    x1 = _out_proj(x_ref, ya_ref, yb_ref, ga_ref, gb_ref, wout_ref)
    hn = _rms(x1, gf_ref[...]).astype(BF16)
    o_ref[...] = x1
    for p in range(N_DENSE_PARTS):
        o_ref[...] += _swiglu(hn, wg_ref.at[p], wu_ref.at[p], wd_ref.at[p])


def _ffn(x, ya, yb, ga, gb, wout, gf, wg, wu, wd, *, l, tm):
    m = x.shape[0]
    j = l // 2
    one = pl.Buffered(1)
    lay3 = lambda i: (l, 0, 0)
    wsel = lambda i: (j, 0, 0)
    return pl.pallas_call(
        _ffn_kernel,
        grid=(m // tm,),
        in_specs=[
            pl.BlockSpec((tm, D_MODEL), lambda i: (i, 0)),
            pl.BlockSpec((tm, MIX_A), lambda i: (i, 0)),
            pl.BlockSpec((tm, MIX_B), lambda i: (i, 0)),
            pl.BlockSpec((None, 1, MIX_A), lay3),
            pl.BlockSpec((None, 1, MIX_B), lay3),
            pl.BlockSpec((None, D_MODEL, D_MODEL), lay3, pipeline_mode=one),
            pl.BlockSpec((None, 1, D_MODEL), lay3),
            pl.BlockSpec((N_DENSE_PARTS, D_MODEL, FF_PAD), wsel, pipeline_mode=one),
            pl.BlockSpec((N_DENSE_PARTS, D_MODEL, FF_PAD), wsel, pipeline_mode=one),
            pl.BlockSpec((N_DENSE_PARTS, FF_PAD, D_MODEL), wsel, pipeline_mode=one),
        ],
        out_specs=pl.BlockSpec((tm, D_MODEL), lambda i: (i, 0)),
        out_shape=jax.ShapeDtypeStruct((m, D_MODEL), F32),
        compiler_params=pltpu.CompilerParams(
            dimension_semantics=("parallel",), vmem_limit_bytes=VMEM_LIMIT),
        name="ffn_dense",
    )(x, ya, yb, ga, gb, wout, gf, wg, wu, wd)


MOE_WIN = 256
SEG = 16
SORT_ROWS = 2 * MOE_WIN + N_EXPERTS * SEG
ROW_TILE = 256
TAIL_TILES = (ROW_TILE // 2, ROW_TILE)
NOT_ROUTED = -1.0e6


def _top2(logits):
    lane = lax.broadcasted_iota(jnp.int32, logits.shape, 1).astype(F32)
    neg = jnp.float32(-jnp.inf)
    lg = jnp.where(lane < N_EXPERTS, logits, neg)
    m1 = jnp.max(lg, axis=1, keepdims=True)
    i1 = jnp.min(jnp.where(lg == m1, lane, float(LANES)), axis=1, keepdims=True)
    lg2 = jnp.where(lane == i1, neg, lg)
    m2 = jnp.max(lg2, axis=1, keepdims=True)
    i2 = jnp.min(jnp.where(lg2 == m2, lane, float(LANES)), axis=1, keepdims=True)
    ex = jnp.exp(m2 - m1)
    w1 = 1.0 / (1.0 + ex)
    w2 = ex / (1.0 + ex)
    gates = jnp.where(lane == i1, w1, 0.0) + jnp.where(lane == i2, w2, 0.0)
    mask = jnp.where((lane == i1) | (lane == i2), 1.0, 0.0)
    return gates, mask


def _pre_moe_kernel(x_ref, ya_ref, yb_ref, ga_ref, gb_ref, wout_ref, gf_ref, rw_ref, rb_ref,
                    x1_ref, hn_ref, gates_ref, d_ref, dt_ref, cnt_ref, *, n_win):
    x1 = _out_proj(x_ref, ya_ref, yb_ref, ga_ref, gb_ref, wout_ref)
    x1_ref[...] = x1
    hn = _rms(x1, gf_ref[...])
    hn_ref[...] = hn.astype(BF16)
    hi, lo = _split_bf16(hn)
    hw = _dot(hi, rw_ref[...])
    logits = hw[:, :LANES] + hw[:, LANES:] + _dot(lo, rw_ref[:, :LANES])
    gates, mask = _top2(logits + rb_ref[...])
    gates_ref[...] = gates
    row = lax.broadcasted_iota(jnp.int32, (MOE_WIN, MOE_WIN), 0)
    col = lax.broadcasted_iota(jnp.int32, (MOE_WIN, MOE_WIN), 1)
    earlier = jnp.where(col < row, 1.0, 0.0).astype(BF16)
    erow = lax.broadcasted_iota(jnp.int32, (LANES, LANES), 0)
    ecol = lax.broadcasted_iota(jnp.int32, (LANES, LANES), 1)
    lower_expert = jnp.where(erow < ecol, 1.0, 0.0).astype(BF16)
    lane = lax.broadcasted_iota(jnp.int32, (MOE_WIN, LANES), 1)
    for w in range(n_win):
        rs = slice(w * MOE_WIN, (w + 1) * MOE_WIN)
        mw = mask[rs, :]
        rank = _dot(earlier, mw.astype(BF16))
        cnt = jnp.broadcast_to(jnp.sum(mw, axis=0, keepdims=True), (SUBLANES, LANES))
        cnt_ref[w] = cnt
        padded = jnp.floor((cnt + (SEG - 1)) * (1.0 / SEG)) * SEG
        seg_start = _dot(padded.astype(BF16), lower_expert)[0:1, :]
        dest = rank + seg_start
        d_lo = jnp.min(jnp.where(mw > 0.0, dest, -NOT_ROUTED), axis=1, keepdims=True)
        d_hi = jnp.max(jnp.where(mw > 0.0, dest, NOT_ROUTED), axis=1, keepdims=True)
        d = jnp.where(lane == 0, d_lo, jnp.where(lane == 1, d_hi, 0.0))
        d_ref[rs, :] = d
        dt_ref[w] = d.T[:SUBLANES, :]


def _pre_moe(x, ya, yb, ga, gb, wout, gf, rw, rb, *, l, tm):
    m = x.shape[0]
    n_win = tm // MOE_WIN
    j = l // 2
    lay3 = lambda i: (l, 0, 0)
    moe3 = lambda i: (j, 0, 0)
    return pl.pallas_call(
        functools.partial(_pre_moe_kernel, n_win=n_win),
        grid=(m // tm,),
        in_specs=[
            pl.BlockSpec((tm, D_MODEL), lambda i: (i, 0)),
            pl.BlockSpec((tm, MIX_A), lambda i: (i, 0)),
            pl.BlockSpec((tm, MIX_B), lambda i: (i, 0)),
            pl.BlockSpec((None, 1, MIX_A), lay3),
            pl.BlockSpec((None, 1, MIX_B), lay3),
            pl.BlockSpec((None, D_MODEL, D_MODEL), lay3),
            pl.BlockSpec((None, 1, D_MODEL), lay3),
            pl.BlockSpec((None, D_MODEL, 2 * LANES), moe3),
            pl.BlockSpec((None, 1, LANES), moe3),
        ],
        out_specs=[
            pl.BlockSpec((tm, D_MODEL), lambda i: (i, 0)),
            pl.BlockSpec((tm, D_MODEL), lambda i: (i, 0)),
            pl.BlockSpec((tm, LANES), lambda i: (i, 0)),
            pl.BlockSpec((tm, LANES), lambda i: (i, 0)),
            pl.BlockSpec((n_win, SUBLANES, MOE_WIN), lambda i: (i, 0, 0)),
            pl.BlockSpec((n_win, SUBLANES, LANES), lambda i: (i, 0, 0)),
        ],
        out_shape=[
            jax.ShapeDtypeStruct((m, D_MODEL), F32),
            jax.ShapeDtypeStruct((m, D_MODEL), BF16),
            jax.ShapeDtypeStruct((m, LANES), F32),
            jax.ShapeDtypeStruct((m, LANES), F32),
            jax.ShapeDtypeStruct((m // MOE_WIN, SUBLANES, MOE_WIN), F32),
            jax.ShapeDtypeStruct((m // MOE_WIN, SUBLANES, LANES), F32),
        ],
        compiler_params=pltpu.CompilerParams(
            dimension_semantics=("parallel",), vmem_limit_bytes=VMEM_LIMIT),
        name="pre_moe",
    )(x, ya, yb, ga, gb, wout, gf, rw, rb)


def _moe_kernel(cnt_sm, hn_ref, gates_ref, d_ref, dt_ref, wg_ref, wu_ref, wd_ref, x1_hbm, gfin_ref, o_ref,
                xs, gs, sb, gsb, x1buf, x1sem, pn_sm, s_sm, off_sm, est_sm, tot_sm, *, n_win, final):
    blk = pl.program_id(0)
    e = pl.program_id(1)

    def x1_copy(w, slot):
        row0 = pl.multiple_of((blk * n_win + w) * MOE_WIN, MOE_WIN)
        return pltpu.make_async_copy(x1_hbm.at[pl.ds(row0, MOE_WIN), :], x1buf.at[slot], x1sem.at[slot])

    def seg_copy(w, ee, to_sorted):
        s0 = s_sm[w * N_EXPERTS + ee]
        o0 = off_sm[w * N_EXPERTS + ee]

        def body(i, carry):
            src = pl.multiple_of(s0 + i * SEG, SEG)
            dst = pl.multiple_of(o0 + i * SEG, SEG)
            if to_sorted:
                xs[pl.ds(dst, SEG), :] = sb[pl.ds(src, SEG), :]
                gs[pl.ds(dst, SEG), :] = gsb[pl.ds(src, SEG), :]
            else:
                sb[pl.ds(src, SEG), :] = xs[pl.ds(dst, SEG), :]
            return carry

        lax.fori_loop(0, pn_sm[w * N_EXPERTS + ee] // SEG, body, 0)

    @pl.when(e == 0)
    def _dispatch():
        pn = [[None] * N_EXPERTS for _ in range(n_win)]
        for w in range(n_win):
            run = jnp.int32(0)
            for ee in range(N_EXPERTS):
                n = cnt_sm[(blk * n_win + w) * N_EXPERTS + ee]
                pn[w][ee] = jnp.bitwise_and(n + (SEG - 1), -SEG)
                pn_sm[w * N_EXPERTS + ee] = pn[w][ee]
                s_sm[w * N_EXPERTS + ee] = run
                run = run + pn[w][ee]
        run = jnp.int32(0)
        for ee in range(N_EXPERTS):
            est_sm[ee] = run
            start = run
            for w in range(n_win):
                off_sm[w * N_EXPERTS + ee] = run
                run = run + pn[w][ee]
            tot_sm[ee] = run - start
        end = pl.multiple_of(run, SEG)
        xs[pl.ds(end, ROW_TILE), :] = jnp.zeros((ROW_TILE, D_MODEL), BF16)
        gs[pl.ds(end, ROW_TILE), :] = jnp.zeros((ROW_TILE, LANES), F32)
        riota = lax.broadcasted_iota(jnp.int32, (SORT_ROWS, MOE_WIN), 0).astype(F32)
        for w in range(n_win):
            rs = slice(w * MOE_WIN, (w + 1) * MOE_WIN)
            g = jnp.where(riota == dt_ref[w, 0:1, :], 1.0, jnp.where(riota == dt_ref[w, 1:2, :], 1.0, 0.0))
            gb = g.astype(BF16)
            sb[...] = _dot(gb, hn_ref[rs, :]).astype(BF16)
            gh, gl = _split_bf16(gates_ref[rs, :])
            gsb[...] = _dot(gb, gh) + _dot(gb, gl)
            for ee in range(N_EXPERTS):
                seg_copy(w, ee, True)

    start = est_sm[e]
    tot = tot_sm[e]

    def row_tile(r0, size, valid):
        r0 = pl.multiple_of(r0, SEG)
        xt = xs[pl.ds(r0, size), :]
        y = _swiglu(xt, wg_ref, wu_ref, wd_ref)
        lane = lax.broadcasted_iota(jnp.int32, (size, LANES), 1)
        gate = jnp.sum(jnp.where(lane == e, gs[pl.ds(r0, size), :], 0.0), axis=1, keepdims=True)
        keep = lax.broadcasted_iota(jnp.int32, (size, D_MODEL), 0) < valid
        xs[pl.ds(r0, size), :] = jnp.where(keep, (y * gate).astype(BF16), xt)

    n_full = tot // ROW_TILE

    def full_body(i, carry):
        row_tile(start + i * ROW_TILE, ROW_TILE, ROW_TILE)
        return carry

    lax.fori_loop(0, n_full, full_body, 0)
    rem = tot - n_full * ROW_TILE
    tail = start + n_full * ROW_TILE

    lo = 0
    for size in TAIL_TILES:
        @pl.when((rem > lo) & (rem <= size))
        def _(size=size):
            row_tile(tail, size, rem)
        lo = size

    @pl.when(e == N_EXPERTS - 1)
    def _combine():
        liota = lax.broadcasted_iota(jnp.int32, (MOE_WIN, SORT_ROWS), 1).astype(F32)
        x1_copy(0, 0).start()
        for w in range(n_win):
            rs = slice(w * MOE_WIN, (w + 1) * MOE_WIN)
            slot = w % 2
            if w + 1 < n_win:
                x1_copy(w + 1, 1 - slot).start()
            for ee in range(N_EXPERTS):
                seg_copy(w, ee, False)
            g = jnp.where(liota == d_ref[rs, 0:1], 1.0, jnp.where(liota == d_ref[rs, 1:2], 1.0, 0.0))
            f = _dot(g.astype(BF16), sb[...])
            x1_copy(w, slot).wait()
            x2 = x1buf[slot] + f
            o_ref[rs, :] = _rms(x2, gfin_ref[...]) if final else x2


def _moe(cnt, hn, gates, d, dt, wg, wu, wd, x1, gfin, *, l, tb, final):
    m = hn.shape[0]
    n_win = tb // MOE_WIN
    j = l // 2
    xs_rows = 2 * tb + n_win * N_EXPERTS * SEG + ROW_TILE
    one = pl.Buffered(1)
    wsel = lambda i, e, c: (j * N_EXPERTS + e, 0, 0)
    grid_spec = pltpu.PrefetchScalarGridSpec(
        num_scalar_prefetch=1,
        grid=(m // tb, N_EXPERTS),
        in_specs=[
            pl.BlockSpec((tb, D_MODEL), lambda i, e, c: (i, 0)),
            pl.BlockSpec((tb, LANES), lambda i, e, c: (i, 0)),
            pl.BlockSpec((tb, LANES), lambda i, e, c: (i, 0)),
            pl.BlockSpec((n_win, SUBLANES, MOE_WIN), lambda i, e, c: (i, 0, 0)),
            pl.BlockSpec((None, D_MODEL, FF_PAD), wsel),
            pl.BlockSpec((None, D_MODEL, FF_PAD), wsel),
            pl.BlockSpec((None, FF_PAD, D_MODEL), wsel),
            pl.BlockSpec(memory_space=pl.ANY),
            pl.BlockSpec((1, D_MODEL), lambda i, e, c: (0, 0)),
        ],
        out_specs=pl.BlockSpec((tb, D_MODEL), lambda i, e, c: (i, 0), pipeline_mode=one),
        scratch_shapes=[
            pltpu.VMEM((xs_rows, D_MODEL), BF16),
            pltpu.VMEM((xs_rows, LANES), F32),
            pltpu.VMEM((SORT_ROWS, D_MODEL), BF16),
            pltpu.VMEM((SORT_ROWS, LANES), F32),
            pltpu.VMEM((2, MOE_WIN, D_MODEL), F32),
            pltpu.SemaphoreType.DMA((2,)),
            pltpu.SMEM((n_win * N_EXPERTS,), jnp.int32),
            pltpu.SMEM((n_win * N_EXPERTS,), jnp.int32),
            pltpu.SMEM((n_win * N_EXPERTS,), jnp.int32),
            pltpu.SMEM((N_EXPERTS,), jnp.int32),
            pltpu.SMEM((N_EXPERTS,), jnp.int32),
        ],
    )
    return pl.pallas_call(
        functools.partial(_moe_kernel, n_win=n_win, final=final),
        grid_spec=grid_spec,
        out_shape=jax.ShapeDtypeStruct((m, D_MODEL), F32),
        compiler_params=pltpu.CompilerParams(
            dimension_semantics=("arbitrary", "arbitrary"), vmem_limit_bytes=MOE_VMEM_LIMIT),
        name="moe",
    )(cnt, hn, gates, d, dt, wg, wu, wd, x1, gfin)


def kernel(x_prompt, x_sample, state_ssm_re, state_ssm_im, norm_mix, w_in, ssm_a_re, ssm_a_im, ssm_log_dt, ssm_b_re, ssm_b_im, ssm_c_re, ssm_c_im, ssm_d, glu_w, glu_b, sgu_w, sgu_b, sgu_ln_g, sgu_ln_b, out_norm_a, out_norm_b, w_out, norm_ffn, ffn_w_gate, ffn_w_up, ffn_w_down, router_w, router_b, moe_w_gate, moe_w_up, moe_w_down, norm_final):
    batch, seq = x_prompt.shape[0], x_prompt.shape[1]
    dec_batch, dec_seq = x_sample.shape[0], x_sample.shape[1]
    mp, ms = batch * seq, dec_batch * dec_seq
    n_moe = moe_w_gate.shape[0]

    lam, bbre, bbim = _s5_prep(ssm_a_re, ssm_a_im, ssm_log_dt, ssm_b_re, ssm_b_im)
    ctre = jnp.swapaxes(ssm_c_re, 2, 3).reshape(DEPTH * N_STATE, SSM_GROUP)
    ctim = jnp.swapaxes(ssm_c_im, 2, 3).reshape(DEPTH * N_STATE, SSM_GROUP)
    dskip = ssm_d.reshape(DEPTH, 1, MIX_A)
    glu = glu_w.reshape(DEPTH * MIX_A, 2 * SSM_GROUP)
    gbias = jnp.stack([glu_b[..., :SSM_GROUP].reshape(DEPTH, MIX_A),
                       glu_b[..., SSM_GROUP:].reshape(DEPTH, MIX_A)], axis=1)
    s5w = (lam, bbre, bbim, ctre, ctim, dskip, glu, gbias)
    w_in_b = w_in.astype(BF16)
    w_out_b = w_out.astype(BF16)
    g_mix = norm_mix.reshape(DEPTH, 1, D_MODEL)
    g_a = out_norm_a.reshape(DEPTH, 1, MIX_A)
    g_b = out_norm_b.reshape(DEPTH, 1, MIX_B)
    g_ffn = norm_ffn.reshape(DEPTH, 1, D_MODEL)
    sgu_bias = jnp.repeat(jnp.swapaxes(sgu_b, 1, 2), HD_B, axis=2)
    sgu_bias_s = sgu_bias[:, :dec_seq]
    lng = sgu_ln_g.reshape(DEPTH, 1, MIX_B)
    lnb = sgu_ln_b.reshape(DEPTH, 1, MIX_B)
    coef_s = jnp.repeat(
        jnp.transpose(sgu_w[:, :, :dec_seq, :dec_seq], (0, 2, 3, 1)).reshape(DEPTH, dec_seq * dec_seq, NH_B),
        HD_B, axis=2)
    dense_split = (FF_PAD, D_FF - FF_PAD)
    dense_g = _pack(jnp.swapaxes(ffn_w_gate, 1, 2), dense_split, transpose=True)
    dense_u = _pack(jnp.swapaxes(ffn_w_up, 1, 2), dense_split, transpose=True)
    dense_d = _pack(ffn_w_down, dense_split, transpose=False)
    moe_t = lambda w: jnp.swapaxes(w.reshape(-1, D_MODEL, D_FF_EXPERT), 1, 2)
    moe_g = _pack(moe_t(moe_w_gate), (D_FF_EXPERT,), transpose=True)
    moe_u = _pack(moe_t(moe_w_up), (D_FF_EXPERT,), transpose=True)
    moe_d = _pack(moe_w_down.reshape(-1, D_FF_EXPERT, D_MODEL), (D_FF_EXPERT,), transpose=False)
    rw = jnp.pad(router_w, ((0, 0), (0, 0), (0, LANES - N_EXPERTS)))
    rwh = rw.astype(BF16)
    rw2 = jnp.concatenate([rwh, (rw - rwh.astype(F32)).astype(BF16)], axis=-1)
    rb = jnp.pad(router_b, ((0, 0), (0, LANES - N_EXPERTS))).reshape(n_moe, 1, LANES)
    gfin = norm_final.reshape(1, D_MODEL)

    h0re = state_ssm_re.reshape(DEPTH, dec_batch, N_STATE)
    h0im = state_ssm_im.reshape(DEPTH, dec_batch, N_STATE)

    assert DEPTH % 2 == 0

    def post_mixer(x, ya, yb, l, tm, tb):
        common = (g_a, g_b, w_out_b, g_ffn)
        if l % 2 == 0:
            return _ffn(x, ya, yb, *common, dense_g, dense_u, dense_d, l=l, tm=tm)
        x1, hn, gates, d, dt, cnt = _pre_moe(x, ya, yb, *common, rw2, rb, l=l, tm=min(2 * tm, tb))
        cnt = cnt[:, 0, :N_EXPERTS].astype(jnp.int32).reshape(-1)
        return _moe(cnt, hn, gates, d, dt, moe_g, moe_u, moe_d, x1, gfin, l=l, tb=tb, final=(l == DEPTH - 1))

    xp = x_prompt.reshape(mp, D_MODEL)
    xs = x_sample.reshape(ms, D_MODEL)
    re_p, im_p, re_s, im_s, v_s = [], [], [], [], []
    for l in range(DEPTH):
        ya_p, yb_p, hre_p, him_p = _mixer(xp.reshape(batch, seq, D_MODEL), g_mix, w_in_b, sgu_w, sgu_bias,
                                          lng, lnb, s5w, l)
        zs = _inproj(xs, g_mix, w_in_b, l, tm=ms)
        ya_s, hre_s, him_s = _s5_sample(zs, h0re, h0im, s5w, l, nb=dec_batch, t_blk=dec_seq)
        yb_s, vn_s = _sgu_sample(zs, coef_s, sgu_bias_s, lng, lnb, l, nb=dec_batch, t_blk=dec_seq)
        xp = post_mixer(xp, ya_p.reshape(mp, MIX_A), yb_p.reshape(mp, MIX_B), l, tm=512, tb=seq)
        xs = post_mixer(xs, ya_s, yb_s, l, tm=ms, tb=ms)
        re_p.append(hre_p)
        im_p.append(him_p)
        re_s.append(hre_s)
        im_s.append(him_s)
        v_s.append(vn_s)

    st = lambda hs, b: jnp.stack(hs).reshape(DEPTH, b, NG_A, SSM_STATE)
    return (xp.reshape(batch, seq, D_MODEL), xs.reshape(dec_batch, dec_seq, D_MODEL),
            st(re_p, batch), st(im_p, batch), st(re_s, dec_batch), st(im_s, dec_batch),
            jnp.stack(v_s).reshape(DEPTH, dec_batch, dec_seq, NH_B, HD_B))
```

```python
import functools
import math

import jax
import jax.numpy as jnp
from jax import lax
from jax.experimental import pallas as pl
from jax.experimental.pallas import tpu as pltpu

F32 = jnp.float32
BF16 = jnp.bfloat16

D_MODEL = 1024
DEPTH = 4
MIX_A = 512
SSM_GROUP = 16
NG_A = 32
SSM_STATE = 64
N_STATE = NG_A * SSM_STATE
MIX_B = 512
HD_B = 64
NH_B = 8
CHUNK = 128
IN_WIDTH = MIX_A + 2 * MIX_B
D_FF = 2752
N_EXPERTS = 8
D_FF_EXPERT = D_FF // 2
FF_PAD = 1408
EPS = 1e-6

LANES = 128
SUBLANES = 8
MXU_DIM = 256
STATE_COLS = 512
N_SCOL = N_STATE // STATE_COLS
GROUPS_PER_CHUNK = STATE_COLS // SSM_STATE
VMEM_LIMIT = 52 * 1024 * 1024
MOE_VMEM_LIMIT = 58 * 1024 * 1024


def _rms(x, g):
    return x * lax.rsqrt(jnp.mean(x * x, axis=-1, keepdims=True) + EPS) * g


def _gelu(x):
    c = math.sqrt(2.0 / math.pi)
    return 0.5 * x * (1.0 + jnp.tanh(c * (x + 0.044715 * (x * x * x))))


def _sigmoid(x):
    return 1.0 / (1.0 + jnp.exp(-x))


def _split_bf16(x):
    hi = x.astype(BF16)
    lo = (x - hi.astype(F32)).astype(BF16)
    return hi, lo


def _dot(a, b):
    return jnp.dot(a, b, preferred_element_type=F32)


def _log2(n):
    assert n & (n - 1) == 0
    return n.bit_length() - 1


def _block_diag_tile(src, n_rep, rows_per_group):
    r_n, k = src.shape
    n = k * n_rep
    t_row = lax.broadcasted_iota(jnp.int32, (k, n), 0)
    t_col = lax.broadcasted_iota(jnp.int32, (k, n), 1)
    tile = jnp.where(jnp.bitwise_and(t_col, k - 1) == t_row, 1.0, 0.0).astype(BF16)
    full = _dot(src.astype(BF16), tile)
    row = lax.broadcasted_iota(jnp.int32, (r_n, n), 0)
    col = lax.broadcasted_iota(jnp.int32, (r_n, n), 1)
    same = lax.shift_right_logical(row, _log2(rows_per_group)) == lax.shift_right_logical(col, _log2(k))
    return jnp.where(same, full, 0.0).astype(BF16)


def _pack_kernel(w_ref, o_ref, *, widths, transpose):
    for i, width in enumerate(widths):
        w = w_ref[i * FF_PAD:i * FF_PAD + width, :]
        if width < FF_PAD:
            w = jnp.concatenate([w, jnp.zeros((FF_PAD - width, w.shape[1]), F32)], axis=0)
        o_ref[i] = (w.T if transpose else w).astype(BF16)


def _pack(w, widths, transpose):
    n_e, n_parts = w.shape[0], len(widths)
    out_blk = (n_parts, D_MODEL, FF_PAD) if transpose else (n_parts, FF_PAD, D_MODEL)
    out = pl.pallas_call(
        functools.partial(_pack_kernel, widths=widths, transpose=transpose),
        grid=(n_e,),
        in_specs=[pl.BlockSpec((None,) + w.shape[1:], lambda e: (e, 0, 0))],
        out_specs=pl.BlockSpec((None,) + out_blk, lambda e: (e, 0, 0, 0)),
        out_shape=jax.ShapeDtypeStruct((n_e,) + out_blk, BF16),
        compiler_params=pltpu.CompilerParams(
            dimension_semantics=("parallel",), vmem_limit_bytes=VMEM_LIMIT),
        name="pack_t" if transpose else "pack",
    )(w)
    return out.reshape((n_e * n_parts,) + out_blk[1:])


def _s5_prep_kernel(are_ref, aim_ref, ldt_ref, bre_ref, bim_ref,
                    abre_ref, abim_ref, bbre_ref, bbim_ref):
    lam_re = are_ref[...]
    lam_im = aim_ref[...]
    dt = jnp.exp(ldt_ref[...])
    mag = jnp.exp(lam_re * dt)
    ab_re = mag * jnp.cos(lam_im * dt)
    ab_im = mag * jnp.sin(lam_im * dt)
    den = lam_re * lam_re + lam_im * lam_im
    nr = ab_re - 1.0
    q_re = (nr * lam_re + ab_im * lam_im) / den
    q_im = (ab_im * lam_re - nr * lam_im) / den
    b_re = bre_ref[...]
    b_im = bim_ref[...]
    abre_ref[...] = ab_re
    abim_ref[...] = ab_im
    bbre_ref[...] = q_re * b_re - q_im * b_im
    bbim_ref[...] = q_re * b_im + q_im * b_re


def _s5_prep(ssm_a_re, ssm_a_im, ssm_log_dt, ssm_b_re, ssm_b_im):
    shp = (DEPTH, NG_A, SSM_GROUP, SSM_STATE)
    rows = DEPTH * NG_A * SSM_GROUP
    bc = lambda a: jnp.broadcast_to(a, shp).reshape(rows, SSM_STATE)
    are = bc(ssm_a_re[:, :, None, :])
    aim = bc(ssm_a_im[:, :, None, :])
    ldt = bc(ssm_log_dt[:, :, None, None])
    bre = jnp.swapaxes(ssm_b_re, 2, 3).reshape(rows, SSM_STATE)
    bim = jnp.swapaxes(ssm_b_im, 2, 3).reshape(rows, SSM_STATE)
    sds = jax.ShapeDtypeStruct((rows, SSM_STATE), F32)
    abre, abim, bbre, bbim = pl.pallas_call(
        _s5_prep_kernel, out_shape=(sds, sds, sds, sds), name="s5_prep",
    )(are, aim, ldt, bre, bim)
    lam_re = abre.reshape(shp)[:, :, 0, :].reshape(DEPTH, 1, N_STATE)
    lam_im = abim.reshape(shp)[:, :, 0, :].reshape(DEPTH, 1, N_STATE)
    lam = jnp.concatenate([lam_re, lam_im], axis=1)
    return lam, bbre, bbim


def _inproj_kernel(x_ref, g_ref, w_ref, z_ref):
    hn = _rms(x_ref[...], g_ref[...])
    z_ref[...] = _dot(hn.astype(BF16), w_ref[...])


def _inproj(x, g, w, l, tm):
    m = x.shape[0]
    return pl.pallas_call(
        _inproj_kernel,
        grid=(m // tm,),
        in_specs=[
            pl.BlockSpec((tm, D_MODEL), lambda i: (i, 0)),
            pl.BlockSpec((None, 1, D_MODEL), lambda i: (l, 0, 0)),
            pl.BlockSpec((None, D_MODEL, IN_WIDTH), lambda i: (l, 0, 0)),
        ],
        out_specs=pl.BlockSpec((tm, IN_WIDTH), lambda i: (i, 0)),
        out_shape=jax.ShapeDtypeStruct((m, IN_WIDTH), F32),
        compiler_params=pltpu.CompilerParams(
            dimension_semantics=("parallel",), vmem_limit_bytes=VMEM_LIMIT),
        name="inproj",
    )(x, g, w)


def _s5_expand(bbre_ref, bbim_ref, ctre_ref, ctim_ref, glu_ref, bblk, cre, cimn, wa, wg):
    for c in range(4):
        rs = slice(c * LANES, (c + 1) * LANES)
        bblk[c, :, :STATE_COLS] = _block_diag_tile(bbre_ref[rs, :], GROUPS_PER_CHUNK, SSM_GROUP)
        bblk[c, :, STATE_COLS:] = _block_diag_tile(bbim_ref[rs, :], GROUPS_PER_CHUNK, SSM_GROUP)
        ss = slice(c * STATE_COLS, (c + 1) * STATE_COLS)
        cre[c] = _block_diag_tile(ctre_ref[ss, :], GROUPS_PER_CHUNK, SSM_STATE)
        cimn[c] = _block_diag_tile(-ctim_ref[ss, :], GROUPS_PER_CHUNK, SSM_STATE)
    for b in range(MIX_A // MXU_DIM):
        ms = slice(b * MXU_DIM, (b + 1) * MXU_DIM)
        wa[b] = _block_diag_tile(glu_ref[ms, :SSM_GROUP], MXU_DIM // SSM_GROUP, SSM_GROUP)
        wg[b] = _block_diag_tile(glu_ref[ms, SSM_GROUP:], MXU_DIM // SSM_GROUP, SSM_GROUP)


def _s5_time_major(utm, lam_ref, d_ref, gbias_ref, hre_ref, him_ref, bre, bim, bblk, cre, cimn, wa, wg,
                   nb, t_blk):
    for c in range(4):
        r = _dot(utm[c].astype(BF16), bblk[c])
        bre[:, c * STATE_COLS:(c + 1) * STATE_COLS] = r[:, :STATE_COLS]
        bim[:, c * STATE_COLS:(c + 1) * STATE_COLS] = r[:, STATE_COLS:]

    for c in range(N_SCOL):
        cs = slice(c * STATE_COLS, (c + 1) * STATE_COLS)
        lr = jnp.broadcast_to(lam_ref[0:1, cs], (SUBLANES, STATE_COLS))
        li = jnp.broadcast_to(lam_ref[1:2, cs], (SUBLANES, STATE_COLS))

        if nb == SUBLANES:
            hr, hi = hre_ref[:, cs], him_ref[:, cs]
            for t in range(t_blk):
                rs = slice(t * SUBLANES, (t + 1) * SUBLANES)
                hr, hi = lr * hr - li * hi + bre[rs, cs], lr * hi + li * hr + bim[rs, cs]
                bre[rs, cs] = hr
                bim[rs, cs] = hi
            hre_ref[:, cs] = hr
            him_ref[:, cs] = hi
            continue

        def group_body(bg, carry, cs=cs, lr=lr, li=li):
            r0 = pl.multiple_of(bg * SUBLANES, SUBLANES)

            def t_body(t, h):
                hr, hi = h
                row = pl.multiple_of(t * nb + r0, SUBLANES)
                nr = lr * hr - li * hi + bre[pl.ds(row, SUBLANES), cs]
                ni = lr * hi + li * hr + bim[pl.ds(row, SUBLANES), cs]
                bre[pl.ds(row, SUBLANES), cs] = nr
                bim[pl.ds(row, SUBLANES), cs] = ni
                return nr, ni

            h0 = (hre_ref[pl.ds(r0, SUBLANES), cs], him_ref[pl.ds(r0, SUBLANES), cs])
            hr, hi = lax.fori_loop(0, t_blk, t_body, h0, unroll=min(t_blk, 8))
            hre_ref[pl.ds(r0, SUBLANES), cs] = hr
            him_ref[pl.ds(r0, SUBLANES), cs] = hi
            return carry

        lax.fori_loop(0, nb // SUBLANES, group_body, 0)

    ys = []
    for c in range(4):
        cs = slice(c * STATE_COLS, (c + 1) * STATE_COLS)
        ls = slice(c * LANES, (c + 1) * LANES)
        y = _dot(bre[:, cs].astype(BF16), cre[c]) + _dot(bim[:, cs].astype(BF16), cimn[c])
        ys.append(_gelu(y + d_ref[:, ls] * utm[c]).astype(BF16))

    os_ = []
    for b in range(MIX_A // MXU_DIM):
        ms = slice(b * MXU_DIM, (b + 1) * MXU_DIM)
        yb = jnp.concatenate(ys[2 * b:2 * b + 2], axis=1)
        za = _dot(yb, wa[b]) + gbias_ref[0:1, ms]
        zg = _dot(yb, wg[b]) + gbias_ref[1:2, ms]
        os_.append(za * _sigmoid(zg))
    return jnp.concatenate(os_, axis=1)


def _s5_kernel(*refs, nb, t_blk):
    (u_ref, h0re_ref, h0im_ref, bbre_ref, bbim_ref, lam_ref, ctre_ref, ctim_ref, d_ref, glu_ref, gbias_ref,
     ya_ref, hre_ref, him_ref, utm, bre, bim, otm, bblk, cre, cimn, wa, wg, tmp) = refs
    hre_ref[...] = h0re_ref[...]
    him_ref[...] = h0im_ref[...]
    _s5_expand(bbre_ref, bbim_ref, ctre_ref, ctim_ref, glu_ref, bblk, cre, cimn, wa, wg)

    for c in range(4):
        ls = slice(c * LANES, (c + 1) * LANES)
        tmp[c] = u_ref[:, ls]
        for t in range(t_blk):
            utm[c, t * nb:(t + 1) * nb, :] = tmp[c, pl.ds(t, nb, stride=t_blk), :]

    o = _s5_time_major(utm, lam_ref, d_ref, gbias_ref, hre_ref, him_ref, bre, bim, bblk, cre, cimn, wa, wg,
                       nb, t_blk)

    for c in range(4):
        ls = slice(c * LANES, (c + 1) * LANES)
        for t in range(t_blk):
            otm[c, pl.ds(t, nb, stride=t_blk), :] = o[t * nb:(t + 1) * nb, ls]
        ya_ref[:, ls] = otm[c]


def _s5_weight_specs(l):
    return [
        pl.BlockSpec((NG_A * SSM_GROUP, SSM_STATE), lambda j: (l, 0)),
        pl.BlockSpec((NG_A * SSM_GROUP, SSM_STATE), lambda j: (l, 0)),
        pl.BlockSpec((None, 2, N_STATE), lambda j: (l, 0, 0)),
        pl.BlockSpec((N_STATE, SSM_GROUP), lambda j: (l, 0)),
        pl.BlockSpec((N_STATE, SSM_GROUP), lambda j: (l, 0)),
        pl.BlockSpec((None, 1, MIX_A), lambda j: (l, 0, 0)),
        pl.BlockSpec((MIX_A, 2 * SSM_GROUP), lambda j: (l, 0)),
        pl.BlockSpec((None, 2, MIX_A), lambda j: (l, 0, 0)),
    ]


def _s5_scratch(rows):
    return [
        pltpu.VMEM((4, rows, LANES), F32),
        pltpu.VMEM((rows, N_STATE), F32),
        pltpu.VMEM((rows, N_STATE), F32),
        pltpu.VMEM((4, rows, LANES), F32),
        pltpu.VMEM((4, LANES, 2 * STATE_COLS), BF16),
        pltpu.VMEM((4, STATE_COLS, LANES), BF16),
        pltpu.VMEM((4, STATE_COLS, LANES), BF16),
        pltpu.VMEM((MIX_A // MXU_DIM, MXU_DIM, MXU_DIM), BF16),
        pltpu.VMEM((MIX_A // MXU_DIM, MXU_DIM, MXU_DIM), BF16),
    ]


def _s5_sample(z, h0re, h0im, weights, l, nb, t_blk):
    rows = nb * t_blk
    lam, bbre, bbim, ctre, ctim, dskip, glu, gbias = weights
    blk = pl.BlockSpec((rows, MIX_A), lambda j: (0, 0))
    st_in = pl.BlockSpec((None, nb, N_STATE), lambda j: (l, 0, 0))
    st_out = pl.BlockSpec((nb, N_STATE), lambda j: (0, 0))
    return pl.pallas_call(
        functools.partial(_s5_kernel, nb=nb, t_blk=t_blk),
        grid=(1,),
        in_specs=[blk, st_in, st_in] + _s5_weight_specs(l),
        out_specs=[blk, st_out, st_out],
        out_shape=[
            jax.ShapeDtypeStruct((rows, MIX_A), F32),
            jax.ShapeDtypeStruct((nb, N_STATE), F32),
            jax.ShapeDtypeStruct((nb, N_STATE), F32),
        ],
        scratch_shapes=_s5_scratch(rows) + [pltpu.VMEM((4, rows, LANES), F32)],
        compiler_params=pltpu.CompilerParams(
            dimension_semantics=("arbitrary",), vmem_limit_bytes=VMEM_LIMIT),
        name="s5_sample",
    )(z, h0re, h0im, bbre, bbim, lam, ctre, ctim, dskip, glu, gbias)


def _group_layernorm(gv, lng_ref, lnb_ref):
    row = lax.broadcasted_iota(jnp.int32, (MXU_DIM, MXU_DIM), 0)
    col = lax.broadcasted_iota(jnp.int32, (MXU_DIM, MXU_DIM), 1)
    same = lax.shift_right_logical(row, _log2(HD_B)) == lax.shift_right_logical(col, _log2(HD_B))
    gm = jnp.where(same, 1.0 / HD_B, 0.0).astype(BF16)

    def gmean(a):
        hi, lo = _split_bf16(a)
        parts = []
        for b in range(MIX_B // MXU_DIM):
            ms = slice(b * MXU_DIM, (b + 1) * MXU_DIM)
            parts.append(_dot(hi[:, ms], gm) + _dot(lo[:, ms], gm))
        return jnp.concatenate(parts, axis=1)

    xc = gv - gmean(gv)
    var = gmean(xc * xc)
    return xc * lax.rsqrt(var + EPS) * lng_ref[...] + lnb_ref[...]


def _mixer_kernel(x_ref, g_ref, w_ref, ws_ref, bias_ref, lng_ref, lnb_ref,
                  bbre_ref, bbim_ref, lam_ref, ctre_ref, ctim_ref, d_ref, glu_ref, gbias_ref,
                  ya_ref, yb_ref, hre_ref, him_ref,
                  hnb, ua, zb, utm, bre, bim, otm, bblk, cre, cimn, wa, wg, *, nb):
    @pl.when(pl.program_id(0) == 0)
    def _():
        hre_ref[...] = jnp.zeros_like(hre_ref)
        him_ref[...] = jnp.zeros_like(him_ref)
        _s5_expand(bbre_ref, bbim_ref, ctre_ref, ctim_ref, glu_ref, bblk, cre, cimn, wa, wg)

    for b in range(nb):
        hnb[b * CHUNK:(b + 1) * CHUNK, :] = _rms(x_ref[b], g_ref[...]).astype(BF16)
    ua[...] = _dot(hnb[...], w_ref[:, :MIX_A])
    for c in range(4):
        ls = slice(c * LANES, (c + 1) * LANES)
        for b in range(nb):
            utm[c, pl.ds(b, CHUNK, stride=nb), :] = ua[b * CHUNK:(b + 1) * CHUNK, ls]
    zb[...] = _dot(hnb[...], w_ref[:, MIX_A:])

    row = lax.broadcasted_iota(jnp.int32, (CHUNK, 2 * CHUNK), 0)
    col = lax.broadcasted_iota(jnp.int32, (CHUNK, 2 * CHUNK), 1)
    causal = jnp.bitwise_and(col, CHUNK - 1) <= row
    lane = lax.broadcasted_iota(jnp.int32, (CHUNK, LANES), 1)
    first_head = lane < HD_B
    wcat = []
    for p in range(NH_B // 2):
        w = jnp.concatenate([ws_ref[2 * p], ws_ref[2 * p + 1]], axis=1)
        wcat.append(jnp.where(causal, w, 0.0).astype(BF16))
    for b in range(nb):
        rs = slice(b * CHUNK, (b + 1) * CHUNK)
        u = _gelu(zb[rs, :MIX_B])
        vn = _group_layernorm(_gelu(zb[rs, MIX_B:]), lng_ref, lnb_ref)
        for p in range(NH_B // 2):
            ls = slice(p * LANES, (p + 1) * LANES)
            vp = vn[:, ls]
            rhs = jnp.concatenate(
                [jnp.where(first_head, vp, 0.0), jnp.where(first_head, 0.0, vp)], axis=0).astype(BF16)
            s = _dot(wcat[p], rhs)
            yb_ref[b, :, ls] = u[:, ls] * (s + bias_ref[:, ls])

    o = _s5_time_major(utm, lam_ref, d_ref, gbias_ref, hre_ref, him_ref, bre, bim, bblk, cre, cimn, wa, wg,
                       nb, CHUNK)
    for c in range(4):
        ls = slice(c * LANES, (c + 1) * LANES)
        otm[c] = o[:, ls]
        for b in range(nb):
            ya_ref[b, :, ls] = otm[c, pl.ds(b, CHUNK, stride=nb), :]


def _mixer(x3, g, w, ws, bias, lng, lnb, s5w, l):
    nb, seq = x3.shape[0], x3.shape[1]
    rows = nb * CHUNK
    lam, bbre, bbim, ctre, ctim, dskip, glu, gbias = s5w
    lay3 = lambda j: (l, 0, 0)
    blk = lambda width: pl.BlockSpec((nb, CHUNK, width), lambda j: (0, j, 0))
    st = pl.BlockSpec((nb, N_STATE), lambda j: (0, 0))
    return pl.pallas_call(
        functools.partial(_mixer_kernel, nb=nb),
        grid=(seq // CHUNK,),
        in_specs=[
            blk(D_MODEL),
            pl.BlockSpec((None, 1, D_MODEL), lay3),
            pl.BlockSpec((None, D_MODEL, IN_WIDTH), lay3, pipeline_mode=pl.Buffered(1)),
            pl.BlockSpec((None, NH_B, CHUNK, CHUNK), lambda j: (l, 0, 0, 0)),
            pl.BlockSpec((None, CHUNK, MIX_B), lay3),
            pl.BlockSpec((None, 1, MIX_B), lay3),
            pl.BlockSpec((None, 1, MIX_B), lay3),
        ] + _s5_weight_specs(l),
        out_specs=[blk(MIX_A), blk(MIX_B), st, st],
        out_shape=[
            jax.ShapeDtypeStruct((nb, seq, MIX_A), F32),
            jax.ShapeDtypeStruct((nb, seq, MIX_B), F32),
            jax.ShapeDtypeStruct((nb, N_STATE), F32),
            jax.ShapeDtypeStruct((nb, N_STATE), F32),
        ],
        scratch_shapes=[
            pltpu.VMEM((rows, D_MODEL), BF16),
            pltpu.VMEM((rows, MIX_A), F32),
            pltpu.VMEM((rows, 2 * MIX_B), F32),
        ] + _s5_scratch(rows),
        compiler_params=pltpu.CompilerParams(
            dimension_semantics=("arbitrary",), vmem_limit_bytes=MOE_VMEM_LIMIT),
        name="mixer",
    )(x3, g, w, ws, bias, lng, lnb, bbre, bbim, lam, ctre, ctim, dskip, glu, gbias)


def _sgu_sample_kernel(u_ref, v_ref, coef_ref, bias_ref, lng_ref, lnb_ref, o_ref, vn_ref,
                       usc, vsc, osc, *, nb, t_blk):
    vn = _group_layernorm(_gelu(v_ref[...]), lng_ref, lnb_ref)
    vn_ref[...] = vn
    u = _gelu(u_ref[...])
    for c in range(4):
        ls = slice(c * LANES, (c + 1) * LANES)
        vsc[c] = vn[:, ls]
        usc[c] = u[:, ls]
        for q in range(t_blk):
            s = bias_ref[q:q + 1, ls]
            for k in range(q + 1):
                s = s + coef_ref[q * t_blk + k:q * t_blk + k + 1, ls] * vsc[c, pl.ds(k, nb, stride=t_blk), :]
            osc[c, pl.ds(q, nb, stride=t_blk), :] = usc[c, pl.ds(q, nb, stride=t_blk), :] * s
        o_ref[:, ls] = osc[c]


def _sgu_sample(z, coef, bias, lng, lnb, l, nb, t_blk):
    m = nb * t_blk
    fix2 = lambda i: (0, 0)
    lay3 = lambda i: (l, 0, 0)
    return pl.pallas_call(
        functools.partial(_sgu_sample_kernel, nb=nb, t_blk=t_blk),
        grid=(1,),
        in_specs=[
            pl.BlockSpec((m, MIX_B), lambda i: (0, 1)),
            pl.BlockSpec((m, MIX_B), lambda i: (0, 2)),
            pl.BlockSpec((None, t_blk * t_blk, MIX_B), lay3),
            pl.BlockSpec((None, t_blk, MIX_B), lay3),
            pl.BlockSpec((None, 1, MIX_B), lay3),
            pl.BlockSpec((None, 1, MIX_B), lay3),
        ],
        out_specs=[pl.BlockSpec((m, MIX_B), fix2), pl.BlockSpec((m, MIX_B), fix2)],
        out_shape=[jax.ShapeDtypeStruct((m, MIX_B), F32), jax.ShapeDtypeStruct((m, MIX_B), F32)],
        scratch_shapes=[pltpu.VMEM((4, m, LANES), F32)] * 3,
        compiler_params=pltpu.CompilerParams(
            dimension_semantics=("arbitrary",), vmem_limit_bytes=VMEM_LIMIT),
        name="sgu_sample",
    )(z, z, coef, bias, lng, lnb)


def _swiglu(x, wg_ref, wu_ref, wd_ref):
    hg = _dot(x, wg_ref[...])
    a = (hg * _sigmoid(hg) * _dot(x, wu_ref[...])).astype(BF16)
    return _dot(a, wd_ref[...])


def _out_proj(x_ref, ya_ref, yb_ref, ga_ref, gb_ref, wout_ref):
    na = _rms(ya_ref[...], ga_ref[...]).astype(BF16)
    nb = _rms(yb_ref[...], gb_ref[...]).astype(BF16)
    return x_ref[...] + _dot(na, wout_ref[0:MIX_A, :]) + _dot(nb, wout_ref[MIX_A:, :])


N_DENSE_PARTS = 2


def _ffn_kernel(x_ref, ya_ref, yb_ref, ga_ref, gb_ref, wout_ref, gf_ref, wg_ref, wu_ref, wd_ref, o_ref):
    x1 = _out_proj(x_ref, ya_ref, yb_ref, ga_ref, gb_ref, wout_ref)
    hn = _rms(x1, gf_ref[...]).astype(BF16)
    o_ref[...] = x1
    for p in range(N_DENSE_PARTS):
        o_ref[...] += _swiglu(hn, wg_ref.at[p], wu_ref.at[p], wd_ref.at[p])


def _ffn(x, ya, yb, ga, gb, wout, gf, wg, wu, wd, *, l, tm):
    m = x.shape[0]
    j = l // 2
    one = pl.Buffered(1)
    lay3 = lambda i: (l, 0, 0)
    wsel = lambda i: (j, 0, 0)
    return pl.pallas_call(
        _ffn_kernel,
        grid=(m // tm,),
        in_specs=[
            pl.BlockSpec((tm, D_MODEL), lambda i: (i, 0)),
            pl.BlockSpec((tm, MIX_A), lambda i: (i, 0)),
            pl.BlockSpec((tm, MIX_B), lambda i: (i, 0)),
            pl.BlockSpec((None, 1, MIX_A), lay3),
            pl.BlockSpec((None, 1, MIX_B), lay3),
            pl.BlockSpec((None, D_MODEL, D_MODEL), lay3, pipeline_mode=one),
            pl.BlockSpec((None, 1, D_MODEL), lay3),
            pl.BlockSpec((N_DENSE_PARTS, D_MODEL, FF_PAD), wsel, pipeline_mode=one),
            pl.BlockSpec((N_DENSE_PARTS, D_MODEL, FF_PAD), wsel, pipeline_mode=one),
            pl.BlockSpec((N_DENSE_PARTS, FF_PAD, D_MODEL), wsel, pipeline_mode=one),
        ],
        out_specs=pl.BlockSpec((tm, D_MODEL), lambda i: (i, 0)),
        out_shape=jax.ShapeDtypeStruct((m, D_MODEL), F32),
        compiler_params=pltpu.CompilerParams(
            dimension_semantics=("parallel",), vmem_limit_bytes=VMEM_LIMIT),
        name="ffn_dense",
    )(x, ya, yb, ga, gb, wout, gf, wg, wu, wd)


MOE_WIN = 256
SEG = 16
SORT_ROWS = 2 * MOE_WIN + N_EXPERTS * SEG
ROW_TILE = 256
TAIL_TILES = (ROW_TILE // 2, ROW_TILE)
NOT_ROUTED = -1.0e6


def _top2(logits):
    lane = lax.broadcasted_iota(jnp.int32, logits.shape, 1).astype(F32)
    neg = jnp.float32(-jnp.inf)
    lg = jnp.where(lane < N_EXPERTS, logits, neg)
    m1 = jnp.max(lg, axis=1, keepdims=True)
    i1 = jnp.min(jnp.where(lg == m1, lane, float(LANES)), axis=1, keepdims=True)
    lg2 = jnp.where(lane == i1, neg, lg)
    m2 = jnp.max(lg2, axis=1, keepdims=True)
    i2 = jnp.min(jnp.where(lg2 == m2, lane, float(LANES)), axis=1, keepdims=True)
    ex = jnp.exp(m2 - m1)
    w1 = 1.0 / (1.0 + ex)
    w2 = ex / (1.0 + ex)
    gates = jnp.where(lane == i1, w1, 0.0) + jnp.where(lane == i2, w2, 0.0)
    mask = jnp.where((lane == i1) | (lane == i2), 1.0, 0.0)
    return gates, mask


def _pre_moe_kernel(x_ref, ya_ref, yb_ref, ga_ref, gb_ref, wout_ref, gf_ref, rw_ref, rb_ref,
                    x1_ref, hn_ref, gates_ref, d_ref, dt_ref, cnt_ref, *, n_win):
    x1 = _out_proj(x_ref, ya_ref, yb_ref, ga_ref, gb_ref, wout_ref)
    x1_ref[...] = x1
    hn = _rms(x1, gf_ref[...])
    hn_ref[...] = hn.astype(BF16)
    hi, lo = _split_bf16(hn)
    hw = _dot(hi, rw_ref[...])
    logits = hw[:, :LANES] + hw[:, LANES:] + _dot(lo, rw_ref[:, :LANES])
    gates, mask = _top2(logits + rb_ref[...])
    gates_ref[...] = gates
    row = lax.broadcasted_iota(jnp.int32, (MOE_WIN, MOE_WIN), 0)
    col = lax.broadcasted_iota(jnp.int32, (MOE_WIN, MOE_WIN), 1)
    earlier = jnp.where(col < row, 1.0, 0.0).astype(BF16)
    erow = lax.broadcasted_iota(jnp.int32, (LANES, LANES), 0)
    ecol = lax.broadcasted_iota(jnp.int32, (LANES, LANES), 1)
    lower_expert = jnp.where(erow < ecol, 1.0, 0.0).astype(BF16)
    lane = lax.broadcasted_iota(jnp.int32, (MOE_WIN, LANES), 1)
    for w in range(n_win):
        rs = slice(w * MOE_WIN, (w + 1) * MOE_WIN)
        mw = mask[rs, :]
        rank = _dot(earlier, mw.astype(BF16))
        cnt = jnp.broadcast_to(jnp.sum(mw, axis=0, keepdims=True), (SUBLANES, LANES))
        cnt_ref[w] = cnt
        padded = jnp.floor((cnt + (SEG - 1)) * (1.0 / SEG)) * SEG
        seg_start = _dot(padded.astype(BF16), lower_expert)[0:1, :]
        dest = rank + seg_start
        d_lo = jnp.min(jnp.where(mw > 0.0, dest, -NOT_ROUTED), axis=1, keepdims=True)
        d_hi = jnp.max(jnp.where(mw > 0.0, dest, NOT_ROUTED), axis=1, keepdims=True)
        d = jnp.where(lane == 0, d_lo, jnp.where(lane == 1, d_hi, 0.0))
        d_ref[rs, :] = d
        dt_ref[w] = d.T[:SUBLANES, :]


def _pre_moe(x, ya, yb, ga, gb, wout, gf, rw, rb, *, l, tm):
    m = x.shape[0]
    n_win = tm // MOE_WIN
    j = l // 2
    lay3 = lambda i: (l, 0, 0)
    moe3 = lambda i: (j, 0, 0)
    return pl.pallas_call(
        functools.partial(_pre_moe_kernel, n_win=n_win),
        grid=(m // tm,),
        in_specs=[
            pl.BlockSpec((tm, D_MODEL), lambda i: (i, 0)),
            pl.BlockSpec((tm, MIX_A), lambda i: (i, 0)),
            pl.BlockSpec((tm, MIX_B), lambda i: (i, 0)),
            pl.BlockSpec((None, 1, MIX_A), lay3),
            pl.BlockSpec((None, 1, MIX_B), lay3),
            pl.BlockSpec((None, D_MODEL, D_MODEL), lay3),
            pl.BlockSpec((None, 1, D_MODEL), lay3),
            pl.BlockSpec((None, D_MODEL, 2 * LANES), moe3),
            pl.BlockSpec((None, 1, LANES), moe3),
        ],
        out_specs=[
            pl.BlockSpec((tm, D_MODEL), lambda i: (i, 0)),
            pl.BlockSpec((tm, D_MODEL), lambda i: (i, 0)),
            pl.BlockSpec((tm, LANES), lambda i: (i, 0)),
            pl.BlockSpec((tm, LANES), lambda i: (i, 0)),
            pl.BlockSpec((n_win, SUBLANES, MOE_WIN), lambda i: (i, 0, 0)),
            pl.BlockSpec((n_win, SUBLANES, LANES), lambda i: (i, 0, 0)),
        ],
        out_shape=[
            jax.ShapeDtypeStruct((m, D_MODEL), F32),
            jax.ShapeDtypeStruct((m, D_MODEL), BF16),
            jax.ShapeDtypeStruct((m, LANES), F32),
            jax.ShapeDtypeStruct((m, LANES), F32),
            jax.ShapeDtypeStruct((m // MOE_WIN, SUBLANES, MOE_WIN), F32),
            jax.ShapeDtypeStruct((m // MOE_WIN, SUBLANES, LANES), F32),
        ],
        compiler_params=pltpu.CompilerParams(
            dimension_semantics=("parallel",), vmem_limit_bytes=VMEM_LIMIT),
        name="pre_moe",
    )(x, ya, yb, ga, gb, wout, gf, rw, rb)


def _moe_kernel(cnt_sm, hn_ref, gates_ref, d_ref, dt_ref, wg_ref, wu_ref, wd_ref, x1_hbm, gfin_ref, o_hbm,
                xs, gs, sb, gsb, x1buf, x1sem, obuf, osem, pn_sm, s_sm, off_sm, est_sm, tot_sm,
                *, n_win, final):
    blk = pl.program_id(0)
    e = pl.program_id(1)

    def window_rows(w):
        return pl.ds(pl.multiple_of((blk * n_win + w) * MOE_WIN, MOE_WIN), MOE_WIN)

    def x1_copy(w, slot):
        return pltpu.make_async_copy(x1_hbm.at[window_rows(w), :], x1buf.at[slot], x1sem.at[slot])

    def out_copy(w, slot):
        return pltpu.make_async_copy(obuf.at[slot], o_hbm.at[window_rows(w), :], osem.at[slot])

    def seg_copy(w, ee, to_sorted):
        s0 = s_sm[w * N_EXPERTS + ee]
        o0 = off_sm[w * N_EXPERTS + ee]

        def body(i, carry):
            src = pl.multiple_of(s0 + i * SEG, SEG)
            dst = pl.multiple_of(o0 + i * SEG, SEG)
            if to_sorted:
                xs[pl.ds(dst, SEG), :] = sb[pl.ds(src, SEG), :]
                gs[pl.ds(dst, SEG), :] = gsb[pl.ds(src, SEG), :]
            else:
                sb[pl.ds(src, SEG), :] = xs[pl.ds(dst, SEG), :]
            return carry

        lax.fori_loop(0, pn_sm[w * N_EXPERTS + ee] // SEG, body, 0)

    @pl.when(e == 0)
    def _dispatch():
        pn = [[None] * N_EXPERTS for _ in range(n_win)]
        for w in range(n_win):
            run = jnp.int32(0)
            for ee in range(N_EXPERTS):
                n = cnt_sm[(blk * n_win + w) * N_EXPERTS + ee]
                pn[w][ee] = jnp.bitwise_and(n + (SEG - 1), -SEG)
                pn_sm[w * N_EXPERTS + ee] = pn[w][ee]
                s_sm[w * N_EXPERTS + ee] = run
                run = run + pn[w][ee]
        run = jnp.int32(0)
        for ee in range(N_EXPERTS):
            est_sm[ee] = run
            start = run
            for w in range(n_win):
                off_sm[w * N_EXPERTS + ee] = run
                run = run + pn[w][ee]
            tot_sm[ee] = run - start
        end = pl.multiple_of(run, SEG)
        xs[pl.ds(end, ROW_TILE), :] = jnp.zeros((ROW_TILE, D_MODEL), BF16)
        gs[pl.ds(end, ROW_TILE), :] = jnp.zeros((ROW_TILE, LANES), F32)
        riota = lax.broadcasted_iota(jnp.int32, (SORT_ROWS, MOE_WIN), 0).astype(F32)
        for w in range(n_win):
            rs = slice(w * MOE_WIN, (w + 1) * MOE_WIN)
            g = jnp.where(riota == dt_ref[w, 0:1, :], 1.0, jnp.where(riota == dt_ref[w, 1:2, :], 1.0, 0.0))
            gb = g.astype(BF16)
            sb[...] = _dot(gb, hn_ref[rs, :]).astype(BF16)
            gh, gl = _split_bf16(gates_ref[rs, :])
            gsb[...] = _dot(gb, gh) + _dot(gb, gl)
            for ee in range(N_EXPERTS):
                seg_copy(w, ee, True)

    start = est_sm[e]
    tot = tot_sm[e]

    def row_tile(r0, size, valid):
        r0 = pl.multiple_of(r0, SEG)
        xt = xs[pl.ds(r0, size), :]
        y = _swiglu(xt, wg_ref, wu_ref, wd_ref)
        lane = lax.broadcasted_iota(jnp.int32, (size, LANES), 1)
        gate = jnp.sum(jnp.where(lane == e, gs[pl.ds(r0, size), :], 0.0), axis=1, keepdims=True)
        keep = lax.broadcasted_iota(jnp.int32, (size, D_MODEL), 0) < valid
        xs[pl.ds(r0, size), :] = jnp.where(keep, (y * gate).astype(BF16), xt)

    n_full = tot // ROW_TILE

    def full_body(i, carry):
        row_tile(start + i * ROW_TILE, ROW_TILE, ROW_TILE)
        return carry

    lax.fori_loop(0, n_full, full_body, 0)
    rem = tot - n_full * ROW_TILE
    tail = start + n_full * ROW_TILE

    lo = 0
    for size in TAIL_TILES:
        @pl.when((rem > lo) & (rem <= size))
        def _(size=size):
            row_tile(tail, size, rem)
        lo = size

    @pl.when(e == N_EXPERTS - 1)
    def _combine():
        liota = lax.broadcasted_iota(jnp.int32, (MOE_WIN, SORT_ROWS), 1).astype(F32)
        x1_copy(0, 0).start()
        for w in range(n_win):
            rs = slice(w * MOE_WIN, (w + 1) * MOE_WIN)
            slot = w % 2
            if w + 1 < n_win:
                x1_copy(w + 1, 1 - slot).start()
            for ee in range(N_EXPERTS):
                seg_copy(w, ee, False)
            g = jnp.where(liota == d_ref[rs, 0:1], 1.0, jnp.where(liota == d_ref[rs, 1:2], 1.0, 0.0))
            f = _dot(g.astype(BF16), sb[...])
            x1_copy(w, slot).wait()
            x2 = x1buf[slot] + f
            if w >= 2:
                out_copy(w - 2, slot).wait()
            obuf[slot] = _rms(x2, gfin_ref[...]) if final else x2
            out_copy(w, slot).start()
        for w in range(max(n_win - 2, 0), n_win):
            out_copy(w, w % 2).wait()


def _moe(cnt, hn, gates, d, dt, wg, wu, wd, x1, gfin, *, l, tb, final):
    m = hn.shape[0]
    n_win = tb // MOE_WIN
    j = l // 2
    xs_rows = 2 * tb + n_win * N_EXPERTS * SEG + ROW_TILE
    wsel = lambda i, e, c: (j * N_EXPERTS + e, 0, 0)
    grid_spec = pltpu.PrefetchScalarGridSpec(
        num_scalar_prefetch=1,
        grid=(m // tb, N_EXPERTS),
        in_specs=[
            pl.BlockSpec((tb, D_MODEL), lambda i, e, c: (i, 0)),
            pl.BlockSpec((tb, LANES), lambda i, e, c: (i, 0)),
            pl.BlockSpec((tb, LANES), lambda i, e, c: (i, 0)),
            pl.BlockSpec((n_win, SUBLANES, MOE_WIN), lambda i, e, c: (i, 0, 0)),
            pl.BlockSpec((None, D_MODEL, FF_PAD), wsel),
            pl.BlockSpec((None, D_MODEL, FF_PAD), wsel),
            pl.BlockSpec((None, FF_PAD, D_MODEL), wsel),
            pl.BlockSpec(memory_space=pl.ANY),
            pl.BlockSpec((1, D_MODEL), lambda i, e, c: (0, 0)),
        ],
        out_specs=pl.BlockSpec(memory_space=pl.ANY),
        scratch_shapes=[
            pltpu.VMEM((xs_rows, D_MODEL), BF16),
            pltpu.VMEM((xs_rows, LANES), F32),
            pltpu.VMEM((SORT_ROWS, D_MODEL), BF16),
            pltpu.VMEM((SORT_ROWS, LANES), F32),
            pltpu.VMEM((2, MOE_WIN, D_MODEL), F32),
            pltpu.SemaphoreType.DMA((2,)),
            pltpu.VMEM((2, MOE_WIN, D_MODEL), F32),
            pltpu.SemaphoreType.DMA((2,)),
            pltpu.SMEM((n_win * N_EXPERTS,), jnp.int32),
            pltpu.SMEM((n_win * N_EXPERTS,), jnp.int32),
            pltpu.SMEM((n_win * N_EXPERTS,), jnp.int32),
            pltpu.SMEM((N_EXPERTS,), jnp.int32),
            pltpu.SMEM((N_EXPERTS,), jnp.int32),
        ],
    )
    return pl.pallas_call(
        functools.partial(_moe_kernel, n_win=n_win, final=final),
        grid_spec=grid_spec,
        out_shape=jax.ShapeDtypeStruct((m, D_MODEL), F32),
        compiler_params=pltpu.CompilerParams(
            dimension_semantics=("arbitrary", "arbitrary"), vmem_limit_bytes=MOE_VMEM_LIMIT),
        name="moe",
    )(cnt, hn, gates, d, dt, wg, wu, wd, x1, gfin)


def kernel(x_prompt, x_sample, state_ssm_re, state_ssm_im, norm_mix, w_in, ssm_a_re, ssm_a_im, ssm_log_dt, ssm_b_re, ssm_b_im, ssm_c_re, ssm_c_im, ssm_d, glu_w, glu_b, sgu_w, sgu_b, sgu_ln_g, sgu_ln_b, out_norm_a, out_norm_b, w_out, norm_ffn, ffn_w_gate, ffn_w_up, ffn_w_down, router_w, router_b, moe_w_gate, moe_w_up, moe_w_down, norm_final):
    batch, seq = x_prompt.shape[0], x_prompt.shape[1]
    dec_batch, dec_seq = x_sample.shape[0], x_sample.shape[1]
    mp, ms = batch * seq, dec_batch * dec_seq
    n_moe = moe_w_gate.shape[0]

    lam, bbre, bbim = _s5_prep(ssm_a_re, ssm_a_im, ssm_log_dt, ssm_b_re, ssm_b_im)
    ctre = jnp.swapaxes(ssm_c_re, 2, 3).reshape(DEPTH * N_STATE, SSM_GROUP)
    ctim = jnp.swapaxes(ssm_c_im, 2, 3).reshape(DEPTH * N_STATE, SSM_GROUP)
    dskip = ssm_d.reshape(DEPTH, 1, MIX_A)
    glu = glu_w.reshape(DEPTH * MIX_A, 2 * SSM_GROUP)
    gbias = jnp.stack([glu_b[..., :SSM_GROUP].reshape(DEPTH, MIX_A),
                       glu_b[..., SSM_GROUP:].reshape(DEPTH, MIX_A)], axis=1)
    s5w = (lam, bbre, bbim, ctre, ctim, dskip, glu, gbias)
    w_in_b = w_in.astype(BF16)
    w_out_b = w_out.astype(BF16)
    g_mix = norm_mix.reshape(DEPTH, 1, D_MODEL)
    g_a = out_norm_a.reshape(DEPTH, 1, MIX_A)
    g_b = out_norm_b.reshape(DEPTH, 1, MIX_B)
    g_ffn = norm_ffn.reshape(DEPTH, 1, D_MODEL)
    sgu_bias = jnp.repeat(jnp.swapaxes(sgu_b, 1, 2), HD_B, axis=2)
    sgu_bias_s = sgu_bias[:, :dec_seq]
    lng = sgu_ln_g.reshape(DEPTH, 1, MIX_B)
    lnb = sgu_ln_b.reshape(DEPTH, 1, MIX_B)
    coef_s = jnp.repeat(
        jnp.transpose(sgu_w[:, :, :dec_seq, :dec_seq], (0, 2, 3, 1)).reshape(DEPTH, dec_seq * dec_seq, NH_B),
        HD_B, axis=2)
    dense_split = (FF_PAD, D_FF - FF_PAD)
    dense_g = _pack(jnp.swapaxes(ffn_w_gate, 1, 2), dense_split, transpose=True)
    dense_u = _pack(jnp.swapaxes(ffn_w_up, 1, 2), dense_split, transpose=True)
    dense_d = _pack(ffn_w_down, dense_split, transpose=False)
    moe_t = lambda w: jnp.swapaxes(w.reshape(-1, D_MODEL, D_FF_EXPERT), 1, 2)
    moe_g = _pack(moe_t(moe_w_gate), (D_FF_EXPERT,), transpose=True)
    moe_u = _pack(moe_t(moe_w_up), (D_FF_EXPERT,), transpose=True)
    moe_d = _pack(moe_w_down.reshape(-1, D_FF_EXPERT, D_MODEL), (D_FF_EXPERT,), transpose=False)
    rw = jnp.pad(router_w, ((0, 0), (0, 0), (0, LANES - N_EXPERTS)))
    rwh = rw.astype(BF16)
    rw2 = jnp.concatenate([rwh, (rw - rwh.astype(F32)).astype(BF16)], axis=-1)
    rb = jnp.pad(router_b, ((0, 0), (0, LANES - N_EXPERTS))).reshape(n_moe, 1, LANES)
    gfin = norm_final.reshape(1, D_MODEL)

    h0re = state_ssm_re.reshape(DEPTH, dec_batch, N_STATE)
    h0im = state_ssm_im.reshape(DEPTH, dec_batch, N_STATE)

    assert DEPTH % 2 == 0

    def post_mixer(x, ya, yb, l, tm, tb):
        common = (g_a, g_b, w_out_b, g_ffn)
        if l % 2 == 0:
            return _ffn(x, ya, yb, *common, dense_g, dense_u, dense_d, l=l, tm=tm)
        x1, hn, gates, d, dt, cnt = _pre_moe(x, ya, yb, *common, rw2, rb, l=l, tm=min(2 * tm, tb))
        cnt = cnt[:, 0, :N_EXPERTS].astype(jnp.int32).reshape(-1)
        return _moe(cnt, hn, gates, d, dt, moe_g, moe_u, moe_d, x1, gfin, l=l, tb=tb, final=(l == DEPTH - 1))

    xp = x_prompt.reshape(mp, D_MODEL)
    xs = x_sample.reshape(ms, D_MODEL)
    re_p, im_p, re_s, im_s, v_s = [], [], [], [], []
    for l in range(DEPTH):
        ya_p, yb_p, hre_p, him_p = _mixer(xp.reshape(batch, seq, D_MODEL), g_mix, w_in_b, sgu_w, sgu_bias,
                                          lng, lnb, s5w, l)
        zs = _inproj(xs, g_mix, w_in_b, l, tm=ms)
        ya_s, hre_s, him_s = _s5_sample(zs, h0re, h0im, s5w, l, nb=dec_batch, t_blk=dec_seq)
        yb_s, vn_s = _sgu_sample(zs, coef_s, sgu_bias_s, lng, lnb, l, nb=dec_batch, t_blk=dec_seq)
        xp = post_mixer(xp, ya_p.reshape(mp, MIX_A), yb_p.reshape(mp, MIX_B), l, tm=512, tb=seq)
        xs = post_mixer(xs, ya_s, yb_s, l, tm=ms, tb=ms)
        re_p.append(hre_p)
        im_p.append(him_p)
        re_s.append(hre_s)
        im_s.append(him_s)
        v_s.append(vn_s)

    st = lambda hs, b: jnp.stack(hs).reshape(DEPTH, b, NG_A, SSM_STATE)
    return (xp.reshape(batch, seq, D_MODEL), xs.reshape(dec_batch, dec_seq, D_MODEL),
            st(re_p, batch), st(im_p, batch), st(re_s, dec_batch), st(im_s, dec_batch),
            jnp.stack(v_s).reshape(DEPTH, dec_batch, dec_seq, NH_B, HD_B))
```

```python
import functools
import math

import jax
import jax.numpy as jnp
from jax import lax
from jax.experimental import pallas as pl
from jax.experimental.pallas import tpu as pltpu

F32 = jnp.float32
BF16 = jnp.bfloat16

D_MODEL = 1024
DEPTH = 4
MIX_A = 512
SSM_GROUP = 16
NG_A = 32
SSM_STATE = 64
N_STATE = NG_A * SSM_STATE
MIX_B = 512
HD_B = 64
NH_B = 8
CHUNK = 128
IN_WIDTH = MIX_A + 2 * MIX_B
D_FF = 2752
N_EXPERTS = 8
D_FF_EXPERT = D_FF // 2
FF_PAD = 1408
EPS = 1e-6

LANES = 128
SUBLANES = 8
MXU_DIM = 256
STATE_COLS = 512
N_SCOL = N_STATE // STATE_COLS
GROUPS_PER_CHUNK = STATE_COLS // SSM_STATE
VMEM_LIMIT = 52 * 1024 * 1024
MOE_VMEM_LIMIT = 58 * 1024 * 1024


def _rms(x, g):
    return x * lax.rsqrt(jnp.mean(x * x, axis=-1, keepdims=True) + EPS) * g


def _gelu(x):
    c = math.sqrt(2.0 / math.pi)
    return 0.5 * x * (1.0 + jnp.tanh(c * (x + 0.044715 * (x * x * x))))


def _sigmoid(x):
    return 1.0 / (1.0 + jnp.exp(-x))


def _split_bf16(x):
    hi = x.astype(BF16)
    lo = (x - hi.astype(F32)).astype(BF16)
    return hi, lo


def _dot(a, b):
    return jnp.dot(a, b, preferred_element_type=F32)


def _log2(n):
    assert n & (n - 1) == 0
    return n.bit_length() - 1


def _block_diag_tile(src, n_rep, rows_per_group):
    r_n, k = src.shape
    n = k * n_rep
    t_row = lax.broadcasted_iota(jnp.int32, (k, n), 0)
    t_col = lax.broadcasted_iota(jnp.int32, (k, n), 1)
    tile = jnp.where(jnp.bitwise_and(t_col, k - 1) == t_row, 1.0, 0.0).astype(BF16)
    full = _dot(src.astype(BF16), tile)
    row = lax.broadcasted_iota(jnp.int32, (r_n, n), 0)
    col = lax.broadcasted_iota(jnp.int32, (r_n, n), 1)
    same = lax.shift_right_logical(row, _log2(rows_per_group)) == lax.shift_right_logical(col, _log2(k))
    return jnp.where(same, full, 0.0).astype(BF16)


def _pack_kernel(w_ref, o_ref, *, widths, transpose):
    for i, width in enumerate(widths):
        w = w_ref[i * FF_PAD:i * FF_PAD + width, :]
        if width < FF_PAD:
            w = jnp.concatenate([w, jnp.zeros((FF_PAD - width, w.shape[1]), F32)], axis=0)
        o_ref[i] = (w.T if transpose else w).astype(BF16)


def _pack(w, widths, transpose):
    n_e, n_parts = w.shape[0], len(widths)
    out_blk = (n_parts, D_MODEL, FF_PAD) if transpose else (n_parts, FF_PAD, D_MODEL)
    out = pl.pallas_call(
        functools.partial(_pack_kernel, widths=widths, transpose=transpose),
        grid=(n_e,),
        in_specs=[pl.BlockSpec((None,) + w.shape[1:], lambda e: (e, 0, 0))],
        out_specs=pl.BlockSpec((None,) + out_blk, lambda e: (e, 0, 0, 0)),
        out_shape=jax.ShapeDtypeStruct((n_e,) + out_blk, BF16),
        compiler_params=pltpu.CompilerParams(
            dimension_semantics=("parallel",), vmem_limit_bytes=VMEM_LIMIT),
        name="pack_t" if transpose else "pack",
    )(w)
    return out.reshape((n_e * n_parts,) + out_blk[1:])


def _s5_prep_kernel(are_ref, aim_ref, ldt_ref, bre_ref, bim_ref,
                    abre_ref, abim_ref, bbre_ref, bbim_ref):
    lam_re = are_ref[...]
    lam_im = aim_ref[...]
    dt = jnp.exp(ldt_ref[...])
    mag = jnp.exp(lam_re * dt)
    ab_re = mag * jnp.cos(lam_im * dt)
    ab_im = mag * jnp.sin(lam_im * dt)
    den = lam_re * lam_re + lam_im * lam_im
    nr = ab_re - 1.0
    q_re = (nr * lam_re + ab_im * lam_im) / den
    q_im = (ab_im * lam_re - nr * lam_im) / den
    b_re = bre_ref[...]
    b_im = bim_ref[...]
    abre_ref[...] = ab_re
    abim_ref[...] = ab_im
    bbre_ref[...] = q_re * b_re - q_im * b_im
    bbim_ref[...] = q_re * b_im + q_im * b_re


def _s5_prep(ssm_a_re, ssm_a_im, ssm_log_dt, ssm_b_re, ssm_b_im):
    shp = (DEPTH, NG_A, SSM_GROUP, SSM_STATE)
    rows = DEPTH * NG_A * SSM_GROUP
    bc = lambda a: jnp.broadcast_to(a, shp).reshape(rows, SSM_STATE)
    are = bc(ssm_a_re[:, :, None, :])
    aim = bc(ssm_a_im[:, :, None, :])
    ldt = bc(ssm_log_dt[:, :, None, None])
    bre = jnp.swapaxes(ssm_b_re, 2, 3).reshape(rows, SSM_STATE)
    bim = jnp.swapaxes(ssm_b_im, 2, 3).reshape(rows, SSM_STATE)
    sds = jax.ShapeDtypeStruct((rows, SSM_STATE), F32)
    abre, abim, bbre, bbim = pl.pallas_call(
        _s5_prep_kernel, out_shape=(sds, sds, sds, sds), name="s5_prep",
    )(are, aim, ldt, bre, bim)
    lam_re = abre.reshape(shp)[:, :, 0, :].reshape(DEPTH, 1, N_STATE)
    lam_im = abim.reshape(shp)[:, :, 0, :].reshape(DEPTH, 1, N_STATE)
    lam = jnp.concatenate([lam_re, lam_im], axis=1)
    return lam, bbre, bbim


def _inproj_kernel(x_ref, g_ref, w_ref, z_ref):
    hn = _rms(x_ref[...], g_ref[...])
    z_ref[...] = _dot(hn.astype(BF16), w_ref[...])


def _inproj(x, g, w, l, tm):
    m = x.shape[0]
    return pl.pallas_call(
        _inproj_kernel,
        grid=(m // tm,),
        in_specs=[
            pl.BlockSpec((tm, D_MODEL), lambda i: (i, 0)),
            pl.BlockSpec((None, 1, D_MODEL), lambda i: (l, 0, 0)),
            pl.BlockSpec((None, D_MODEL, IN_WIDTH), lambda i: (l, 0, 0)),
        ],
        out_specs=pl.BlockSpec((tm, IN_WIDTH), lambda i: (i, 0)),
        out_shape=jax.ShapeDtypeStruct((m, IN_WIDTH), F32),
        compiler_params=pltpu.CompilerParams(
            dimension_semantics=("parallel",), vmem_limit_bytes=VMEM_LIMIT),
        name="inproj",
    )(x, g, w)


def _s5_expand(bbre_ref, bbim_ref, ctre_ref, ctim_ref, glu_ref, bblk, cre, cimn, wa, wg):
    for c in range(4):
        rs = slice(c * LANES, (c + 1) * LANES)
        bblk[c, :, :STATE_COLS] = _block_diag_tile(bbre_ref[rs, :], GROUPS_PER_CHUNK, SSM_GROUP)
        bblk[c, :, STATE_COLS:] = _block_diag_tile(bbim_ref[rs, :], GROUPS_PER_CHUNK, SSM_GROUP)
        ss = slice(c * STATE_COLS, (c + 1) * STATE_COLS)
        cre[c] = _block_diag_tile(ctre_ref[ss, :], GROUPS_PER_CHUNK, SSM_STATE)
        cimn[c] = _block_diag_tile(-ctim_ref[ss, :], GROUPS_PER_CHUNK, SSM_STATE)
    for b in range(MIX_A // MXU_DIM):
        ms = slice(b * MXU_DIM, (b + 1) * MXU_DIM)
        wa[b] = _block_diag_tile(glu_ref[ms, :SSM_GROUP], MXU_DIM // SSM_GROUP, SSM_GROUP)
        wg[b] = _block_diag_tile(glu_ref[ms, SSM_GROUP:], MXU_DIM // SSM_GROUP, SSM_GROUP)


def _s5_time_major(utm, lam_ref, d_ref, gbias_ref, hre_ref, him_ref, bre, bim, bblk, cre, cimn, wa, wg,
                   nb, t_blk):
    for c in range(4):
        r = _dot(utm[c].astype(BF16), bblk[c])
        bre[:, c * STATE_COLS:(c + 1) * STATE_COLS] = r[:, :STATE_COLS]
        bim[:, c * STATE_COLS:(c + 1) * STATE_COLS] = r[:, STATE_COLS:]

    for c in range(N_SCOL):
        cs = slice(c * STATE_COLS, (c + 1) * STATE_COLS)
        lr = jnp.broadcast_to(lam_ref[0:1, cs], (SUBLANES, STATE_COLS))
        li = jnp.broadcast_to(lam_ref[1:2, cs], (SUBLANES, STATE_COLS))

        if nb == SUBLANES:
            hr, hi = hre_ref[:, cs], him_ref[:, cs]
            for t in range(t_blk):
                rs = slice(t * SUBLANES, (t + 1) * SUBLANES)
                hr, hi = lr * hr - li * hi + bre[rs, cs], lr * hi + li * hr + bim[rs, cs]
                bre[rs, cs] = hr
                bim[rs, cs] = hi
            hre_ref[:, cs] = hr
            him_ref[:, cs] = hi
            continue

        def group_body(bg, carry, cs=cs, lr=lr, li=li):
            r0 = pl.multiple_of(bg * SUBLANES, SUBLANES)

            def t_body(t, h):
                hr, hi = h
                row = pl.multiple_of(t * nb + r0, SUBLANES)
                nr = lr * hr - li * hi + bre[pl.ds(row, SUBLANES), cs]
                ni = lr * hi + li * hr + bim[pl.ds(row, SUBLANES), cs]
                bre[pl.ds(row, SUBLANES), cs] = nr
                bim[pl.ds(row, SUBLANES), cs] = ni
                return nr, ni

            h0 = (hre_ref[pl.ds(r0, SUBLANES), cs], him_ref[pl.ds(r0, SUBLANES), cs])
            hr, hi = lax.fori_loop(0, t_blk, t_body, h0, unroll=min(t_blk, 8))
            hre_ref[pl.ds(r0, SUBLANES), cs] = hr
            him_ref[pl.ds(r0, SUBLANES), cs] = hi
            return carry

        lax.fori_loop(0, nb // SUBLANES, group_body, 0)

    ys = []
    for c in range(4):
        cs = slice(c * STATE_COLS, (c + 1) * STATE_COLS)
        ls = slice(c * LANES, (c + 1) * LANES)
        y = _dot(bre[:, cs].astype(BF16), cre[c]) + _dot(bim[:, cs].astype(BF16), cimn[c])
        ys.append(_gelu(y + d_ref[:, ls] * utm[c]).astype(BF16))

    os_ = []
    for b in range(MIX_A // MXU_DIM):
        ms = slice(b * MXU_DIM, (b + 1) * MXU_DIM)
        yb = jnp.concatenate(ys[2 * b:2 * b + 2], axis=1)
        za = _dot(yb, wa[b]) + gbias_ref[0:1, ms]
        zg = _dot(yb, wg[b]) + gbias_ref[1:2, ms]
        os_.append(za * _sigmoid(zg))
    return jnp.concatenate(os_, axis=1)


def _s5_kernel(*refs, nb, t_blk):
    (u_ref, h0re_ref, h0im_ref, bbre_ref, bbim_ref, lam_ref, ctre_ref, ctim_ref, d_ref, glu_ref, gbias_ref,
     ya_ref, hre_ref, him_ref, utm, bre, bim, otm, bblk, cre, cimn, wa, wg, tmp) = refs
    hre_ref[...] = h0re_ref[...]
    him_ref[...] = h0im_ref[...]
    _s5_expand(bbre_ref, bbim_ref, ctre_ref, ctim_ref, glu_ref, bblk, cre, cimn, wa, wg)

    for c in range(4):
        ls = slice(c * LANES, (c + 1) * LANES)
        tmp[c] = u_ref[:, ls]
        for t in range(t_blk):
            utm[c, t * nb:(t + 1) * nb, :] = tmp[c, pl.ds(t, nb, stride=t_blk), :]

    o = _s5_time_major(utm, lam_ref, d_ref, gbias_ref, hre_ref, him_ref, bre, bim, bblk, cre, cimn, wa, wg,
                       nb, t_blk)

    for c in range(4):
        ls = slice(c * LANES, (c + 1) * LANES)
        for t in range(t_blk):
            otm[c, pl.ds(t, nb, stride=t_blk), :] = o[t * nb:(t + 1) * nb, ls]
        ya_ref[:, ls] = otm[c]


def _s5_weight_specs(l):
    return [
        pl.BlockSpec((NG_A * SSM_GROUP, SSM_STATE), lambda j: (l, 0)),
        pl.BlockSpec((NG_A * SSM_GROUP, SSM_STATE), lambda j: (l, 0)),
        pl.BlockSpec((None, 2, N_STATE), lambda j: (l, 0, 0)),
        pl.BlockSpec((N_STATE, SSM_GROUP), lambda j: (l, 0)),
        pl.BlockSpec((N_STATE, SSM_GROUP), lambda j: (l, 0)),
        pl.BlockSpec((None, 1, MIX_A), lambda j: (l, 0, 0)),
        pl.BlockSpec((MIX_A, 2 * SSM_GROUP), lambda j: (l, 0)),
        pl.BlockSpec((None, 2, MIX_A), lambda j: (l, 0, 0)),
    ]


def _s5_scratch(rows):
    return [
        pltpu.VMEM((4, rows, LANES), F32),
        pltpu.VMEM((rows, N_STATE), F32),
        pltpu.VMEM((rows, N_STATE), F32),
        pltpu.VMEM((4, rows, LANES), F32),
        pltpu.VMEM((4, LANES, 2 * STATE_COLS), BF16),
        pltpu.VMEM((4, STATE_COLS, LANES), BF16),
        pltpu.VMEM((4, STATE_COLS, LANES), BF16),
        pltpu.VMEM((MIX_A // MXU_DIM, MXU_DIM, MXU_DIM), BF16),
        pltpu.VMEM((MIX_A // MXU_DIM, MXU_DIM, MXU_DIM), BF16),
    ]


def _s5_sample(z, h0re, h0im, weights, l, nb, t_blk):
    rows = nb * t_blk
    lam, bbre, bbim, ctre, ctim, dskip, glu, gbias = weights
    blk = pl.BlockSpec((rows, MIX_A), lambda j: (0, 0))
    st_in = pl.BlockSpec((None, nb, N_STATE), lambda j: (l, 0, 0))
    st_out = pl.BlockSpec((nb, N_STATE), lambda j: (0, 0))
    return pl.pallas_call(
        functools.partial(_s5_kernel, nb=nb, t_blk=t_blk),
        grid=(1,),
        in_specs=[blk, st_in, st_in] + _s5_weight_specs(l),
        out_specs=[blk, st_out, st_out],
        out_shape=[
            jax.ShapeDtypeStruct((rows, MIX_A), F32),
            jax.ShapeDtypeStruct((nb, N_STATE), F32),
            jax.ShapeDtypeStruct((nb, N_STATE), F32),
        ],
        scratch_shapes=_s5_scratch(rows) + [pltpu.VMEM((4, rows, LANES), F32)],
        compiler_params=pltpu.CompilerParams(
            dimension_semantics=("arbitrary",), vmem_limit_bytes=VMEM_LIMIT),
        name="s5_sample",
    )(z, h0re, h0im, bbre, bbim, lam, ctre, ctim, dskip, glu, gbias)


def _group_layernorm(gv, lng_ref, lnb_ref):
    row = lax.broadcasted_iota(jnp.int32, (MXU_DIM, MXU_DIM), 0)
    col = lax.broadcasted_iota(jnp.int32, (MXU_DIM, MXU_DIM), 1)
    same = lax.shift_right_logical(row, _log2(HD_B)) == lax.shift_right_logical(col, _log2(HD_B))
    gm = jnp.where(same, 1.0 / HD_B, 0.0).astype(BF16)

    def gmean(a):
        hi, lo = _split_bf16(a)
        parts = []
        for b in range(MIX_B // MXU_DIM):
            ms = slice(b * MXU_DIM, (b + 1) * MXU_DIM)
            parts.append(_dot(hi[:, ms], gm) + _dot(lo[:, ms], gm))
        return jnp.concatenate(parts, axis=1)

    xc = gv - gmean(gv)
    var = gmean(xc * xc)
    return xc * lax.rsqrt(var + EPS) * lng_ref[...] + lnb_ref[...]


def _mixer_kernel(x_ref, g_ref, w_ref, ws_ref, bias_ref, lng_ref, lnb_ref,
                  bbre_ref, bbim_ref, lam_ref, ctre_ref, ctim_ref, d_ref, glu_ref, gbias_ref,
                  ya_ref, yb_ref, hre_ref, him_ref,
                  hnb, ua, zb, utm, bre, bim, otm, bblk, cre, cimn, wa, wg, *, nb):
    @pl.when(pl.program_id(0) == 0)
    def _():
        hre_ref[...] = jnp.zeros_like(hre_ref)
        him_ref[...] = jnp.zeros_like(him_ref)
        _s5_expand(bbre_ref, bbim_ref, ctre_ref, ctim_ref, glu_ref, bblk, cre, cimn, wa, wg)

    for b in range(nb):
        hnb[b * CHUNK:(b + 1) * CHUNK, :] = _rms(x_ref[b], g_ref[...]).astype(BF16)
    ua[...] = _dot(hnb[...], w_ref[:, :MIX_A])
    for c in range(4):
        ls = slice(c * LANES, (c + 1) * LANES)
        for b in range(nb):
            utm[c, pl.ds(b, CHUNK, stride=nb), :] = ua[b * CHUNK:(b + 1) * CHUNK, ls]
    zb[...] = _dot(hnb[...], w_ref[:, MIX_A:])

    row = lax.broadcasted_iota(jnp.int32, (CHUNK, 2 * CHUNK), 0)
    col = lax.broadcasted_iota(jnp.int32, (CHUNK, 2 * CHUNK), 1)
    causal = jnp.bitwise_and(col, CHUNK - 1) <= row
    lane = lax.broadcasted_iota(jnp.int32, (CHUNK, LANES), 1)
    first_head = lane < HD_B
    wcat = []
    for p in range(NH_B // 2):
        w = jnp.concatenate([ws_ref[2 * p], ws_ref[2 * p + 1]], axis=1)
        wcat.append(jnp.where(causal, w, 0.0).astype(BF16))
    for b in range(nb):
        rs = slice(b * CHUNK, (b + 1) * CHUNK)
        u = _gelu(zb[rs, :MIX_B])
        vn = _group_layernorm(_gelu(zb[rs, MIX_B:]), lng_ref, lnb_ref)
        for p in range(NH_B // 2):
            ls = slice(p * LANES, (p + 1) * LANES)
            vp = vn[:, ls]
            rhs = jnp.concatenate(
                [jnp.where(first_head, vp, 0.0), jnp.where(first_head, 0.0, vp)], axis=0).astype(BF16)
            s = _dot(wcat[p], rhs)
            yb_ref[b, :, ls] = u[:, ls] * (s + bias_ref[:, ls])

    o = _s5_time_major(utm, lam_ref, d_ref, gbias_ref, hre_ref, him_ref, bre, bim, bblk, cre, cimn, wa, wg,
                       nb, CHUNK)
    for c in range(4):
        ls = slice(c * LANES, (c + 1) * LANES)
        otm[c] = o[:, ls]
        for b in range(nb):
            ya_ref[b, :, ls] = otm[c, pl.ds(b, CHUNK, stride=nb), :]


def _mixer(x3, g, w, ws, bias, lng, lnb, s5w, l):
    nb, seq = x3.shape[0], x3.shape[1]
    rows = nb * CHUNK
    lam, bbre, bbim, ctre, ctim, dskip, glu, gbias = s5w
    lay3 = lambda j: (l, 0, 0)
    blk = lambda width: pl.BlockSpec((nb, CHUNK, width), lambda j: (0, j, 0))
    st = pl.BlockSpec((nb, N_STATE), lambda j: (0, 0))
    return pl.pallas_call(
        functools.partial(_mixer_kernel, nb=nb),
        grid=(seq // CHUNK,),
        in_specs=[
            blk(D_MODEL),
            pl.BlockSpec((None, 1, D_MODEL), lay3),
            pl.BlockSpec((None, D_MODEL, IN_WIDTH), lay3, pipeline_mode=pl.Buffered(1)),
            pl.BlockSpec((None, NH_B, CHUNK, CHUNK), lambda j: (l, 0, 0, 0)),
            pl.BlockSpec((None, CHUNK, MIX_B), lay3),
            pl.BlockSpec((None, 1, MIX_B), lay3),
            pl.BlockSpec((None, 1, MIX_B), lay3),
        ] + _s5_weight_specs(l),
        out_specs=[blk(MIX_A), blk(MIX_B), st, st],
        out_shape=[
            jax.ShapeDtypeStruct((nb, seq, MIX_A), F32),
            jax.ShapeDtypeStruct((nb, seq, MIX_B), F32),
            jax.ShapeDtypeStruct((nb, N_STATE), F32),
            jax.ShapeDtypeStruct((nb, N_STATE), F32),
        ],
        scratch_shapes=[
            pltpu.VMEM((rows, D_MODEL), BF16),
            pltpu.VMEM((rows, MIX_A), F32),
            pltpu.VMEM((rows, 2 * MIX_B), F32),
        ] + _s5_scratch(rows),
        compiler_params=pltpu.CompilerParams(
            dimension_semantics=("arbitrary",), vmem_limit_bytes=MOE_VMEM_LIMIT),
        name="mixer",
    )(x3, g, w, ws, bias, lng, lnb, bbre, bbim, lam, ctre, ctim, dskip, glu, gbias)


def _sgu_sample_kernel(u_ref, v_ref, coef_ref, bias_ref, lng_ref, lnb_ref, o_ref, vn_ref,
                       usc, vsc, osc, *, nb, t_blk):
    vn = _group_layernorm(_gelu(v_ref[...]), lng_ref, lnb_ref)
    vn_ref[...] = vn
    u = _gelu(u_ref[...])
    for c in range(4):
        ls = slice(c * LANES, (c + 1) * LANES)
        vsc[c] = vn[:, ls]
        usc[c] = u[:, ls]
        for q in range(t_blk):
            s = bias_ref[q:q + 1, ls]
            for k in range(q + 1):
                s = s + coef_ref[q * t_blk + k:q * t_blk + k + 1, ls] * vsc[c, pl.ds(k, nb, stride=t_blk), :]
            osc[c, pl.ds(q, nb, stride=t_blk), :] = usc[c, pl.ds(q, nb, stride=t_blk), :] * s
        o_ref[:, ls] = osc[c]


def _sgu_sample(z, coef, bias, lng, lnb, l, nb, t_blk):
    m = nb * t_blk
    fix2 = lambda i: (0, 0)
    lay3 = lambda i: (l, 0, 0)
    return pl.pallas_call(
        functools.partial(_sgu_sample_kernel, nb=nb, t_blk=t_blk),
        grid=(1,),
        in_specs=[
            pl.BlockSpec((m, MIX_B), lambda i: (0, 1)),
            pl.BlockSpec((m, MIX_B), lambda i: (0, 2)),
            pl.BlockSpec((None, t_blk * t_blk, MIX_B), lay3),
            pl.BlockSpec((None, t_blk, MIX_B), lay3),
            pl.BlockSpec((None, 1, MIX_B), lay3),
            pl.BlockSpec((None, 1, MIX_B), lay3),
        ],
        out_specs=[pl.BlockSpec((m, MIX_B), fix2), pl.BlockSpec((m, MIX_B), fix2)],
        out_shape=[jax.ShapeDtypeStruct((m, MIX_B), F32), jax.ShapeDtypeStruct((m, MIX_B), F32)],
        scratch_shapes=[pltpu.VMEM((4, m, LANES), F32)] * 3,
        compiler_params=pltpu.CompilerParams(
            dimension_semantics=("arbitrary",), vmem_limit_bytes=VMEM_LIMIT),
        name="sgu_sample",
    )(z, z, coef, bias, lng, lnb)


def _swiglu(x, wg_ref, wu_ref, wd_ref):
    hg = _dot(x, wg_ref[...])
    a = (hg * _sigmoid(hg) * _dot(x, wu_ref[...])).astype(BF16)
    return _dot(a, wd_ref[...])


def _out_proj(x_ref, ya_ref, yb_ref, ga_ref, gb_ref, wout_ref):
    na = _rms(ya_ref[...], ga_ref[...]).astype(BF16)
    nb = _rms(yb_ref[...], gb_ref[...]).astype(BF16)
    return x_ref[...] + _dot(na, wout_ref[0:MIX_A, :]) + _dot(nb, wout_ref[MIX_A:, :])


N_DENSE_PARTS = 2


def _ffn_kernel(x_ref, ya_ref, yb_ref, ga_ref, gb_ref, wout_ref, gf_ref, wg_ref, wu_ref, wd_ref, o_ref):
    x1 = _out_proj(x_ref, ya_ref, yb_ref, ga_ref, gb_ref, wout_ref)
    hn = _rms(x1, gf_ref[...]).astype(BF16)
    o_ref[...] = x1
    for p in range(N_DENSE_PARTS):
        o_ref[...] += _swiglu(hn, wg_ref.at[p], wu_ref.at[p], wd_ref.at[p])


def _ffn(x, ya, yb, ga, gb, wout, gf, wg, wu, wd, *, l, tm):
    m = x.shape[0]
    j = l // 2
    one = pl.Buffered(1)
    lay3 = lambda i: (l, 0, 0)
    wsel = lambda i: (j, 0, 0)
    return pl.pallas_call(
        _ffn_kernel,
        grid=(m // tm,),
        in_specs=[
            pl.BlockSpec((tm, D_MODEL), lambda i: (i, 0)),
            pl.BlockSpec((tm, MIX_A), lambda i: (i, 0)),
            pl.BlockSpec((tm, MIX_B), lambda i: (i, 0)),
            pl.BlockSpec((None, 1, MIX_A), lay3),
            pl.BlockSpec((None, 1, MIX_B), lay3),
            pl.BlockSpec((None, D_MODEL, D_MODEL), lay3, pipeline_mode=one),
            pl.BlockSpec((None, 1, D_MODEL), lay3),
            pl.BlockSpec((N_DENSE_PARTS, D_MODEL, FF_PAD), wsel, pipeline_mode=one),
            pl.BlockSpec((N_DENSE_PARTS, D_MODEL, FF_PAD), wsel, pipeline_mode=one),
            pl.BlockSpec((N_DENSE_PARTS, FF_PAD, D_MODEL), wsel, pipeline_mode=one),
        ],
        out_specs=pl.BlockSpec((tm, D_MODEL), lambda i: (i, 0)),
        out_shape=jax.ShapeDtypeStruct((m, D_MODEL), F32),
        compiler_params=pltpu.CompilerParams(
            dimension_semantics=("parallel",), vmem_limit_bytes=VMEM_LIMIT),
        name="ffn_dense",
    )(x, ya, yb, ga, gb, wout, gf, wg, wu, wd)


MOE_WIN = 256
SEG = 16
SORT_ROWS = 2 * MOE_WIN + N_EXPERTS * SEG
ROW_TILE = 256
TAIL_TILES = (ROW_TILE // 2, ROW_TILE)
NOT_ROUTED = -1.0e6


def _top2(logits):
    lane = lax.broadcasted_iota(jnp.int32, logits.shape, 1).astype(F32)
    neg = jnp.float32(-jnp.inf)
    lg = jnp.where(lane < N_EXPERTS, logits, neg)
    m1 = jnp.max(lg, axis=1, keepdims=True)
    i1 = jnp.min(jnp.where(lg == m1, lane, float(LANES)), axis=1, keepdims=True)
    lg2 = jnp.where(lane == i1, neg, lg)
    m2 = jnp.max(lg2, axis=1, keepdims=True)
    i2 = jnp.min(jnp.where(lg2 == m2, lane, float(LANES)), axis=1, keepdims=True)
    ex = jnp.exp(m2 - m1)
    w1 = 1.0 / (1.0 + ex)
    w2 = ex / (1.0 + ex)
    gates = jnp.where(lane == i1, w1, 0.0) + jnp.where(lane == i2, w2, 0.0)
    mask = jnp.where((lane == i1) | (lane == i2), 1.0, 0.0)
    return gates, mask


def _pre_moe_kernel(x_ref, ya_ref, yb_ref, ga_ref, gb_ref, wout_ref, gf_ref, rw_ref, rb_ref,
                    x1_ref, hn_ref, gates_ref, d_ref, dt_ref, cnt_ref, *, n_win):
    x1 = _out_proj(x_ref, ya_ref, yb_ref, ga_ref, gb_ref, wout_ref)
    x1_ref[...] = x1
    hn = _rms(x1, gf_ref[...])
    hn_ref[...] = hn.astype(BF16)
    hi, lo = _split_bf16(hn)
    hw = _dot(hi, rw_ref[...])
    logits = hw[:, :LANES] + hw[:, LANES:] + _dot(lo, rw_ref[:, :LANES])
    gates, mask = _top2(logits + rb_ref[...])
    gates_ref[...] = gates
    row = lax.broadcasted_iota(jnp.int32, (MOE_WIN, MOE_WIN), 0)
    col = lax.broadcasted_iota(jnp.int32, (MOE_WIN, MOE_WIN), 1)
    earlier = jnp.where(col < row, 1.0, 0.0).astype(BF16)
    erow = lax.broadcasted_iota(jnp.int32, (LANES, LANES), 0)
    ecol = lax.broadcasted_iota(jnp.int32, (LANES, LANES), 1)
    lower_expert = jnp.where(erow < ecol, 1.0, 0.0).astype(BF16)
    lane = lax.broadcasted_iota(jnp.int32, (MOE_WIN, LANES), 1)
    for w in range(n_win):
        rs = slice(w * MOE_WIN, (w + 1) * MOE_WIN)
        mw = mask[rs, :]
        rank = _dot(earlier, mw.astype(BF16))
        cnt = jnp.broadcast_to(jnp.sum(mw, axis=0, keepdims=True), (SUBLANES, LANES))
        cnt_ref[w] = cnt
        padded = jnp.floor((cnt + (SEG - 1)) * (1.0 / SEG)) * SEG
        seg_start = _dot(padded.astype(BF16), lower_expert)[0:1, :]
        dest = rank + seg_start
        d_lo = jnp.min(jnp.where(mw > 0.0, dest, -NOT_ROUTED), axis=1, keepdims=True)
        d_hi = jnp.max(jnp.where(mw > 0.0, dest, NOT_ROUTED), axis=1, keepdims=True)
        d = jnp.where(lane == 0, d_lo, jnp.where(lane == 1, d_hi, 0.0))
        d_ref[rs, :] = d
        dt_ref[w] = d.T[:SUBLANES, :]


def _pre_moe(x, ya, yb, ga, gb, wout, gf, rw, rb, *, l, tm):
    m = x.shape[0]
    n_win = tm // MOE_WIN
    j = l // 2
    lay3 = lambda i: (l, 0, 0)
    moe3 = lambda i: (j, 0, 0)
    return pl.pallas_call(
        functools.partial(_pre_moe_kernel, n_win=n_win),
        grid=(m // tm,),
        in_specs=[
            pl.BlockSpec((tm, D_MODEL), lambda i: (i, 0)),
            pl.BlockSpec((tm, MIX_A), lambda i: (i, 0)),
            pl.BlockSpec((tm, MIX_B), lambda i: (i, 0)),
            pl.BlockSpec((None, 1, MIX_A), lay3),
            pl.BlockSpec((None, 1, MIX_B), lay3),
            pl.BlockSpec((None, D_MODEL, D_MODEL), lay3),
            pl.BlockSpec((None, 1, D_MODEL), lay3),
            pl.BlockSpec((None, D_MODEL, 2 * LANES), moe3),
            pl.BlockSpec((None, 1, LANES), moe3),
        ],
        out_specs=[
            pl.BlockSpec((tm, D_MODEL), lambda i: (i, 0)),
            pl.BlockSpec((tm, D_MODEL), lambda i: (i, 0)),
            pl.BlockSpec((tm, LANES), lambda i: (i, 0)),
            pl.BlockSpec((tm, LANES), lambda i: (i, 0)),
            pl.BlockSpec((n_win, SUBLANES, MOE_WIN), lambda i: (i, 0, 0)),
            pl.BlockSpec((n_win, SUBLANES, LANES), lambda i: (i, 0, 0)),
        ],
        out_shape=[
            jax.ShapeDtypeStruct((m, D_MODEL), F32),
            jax.ShapeDtypeStruct((m, D_MODEL), BF16),
            jax.ShapeDtypeStruct((m, LANES), F32),
            jax.ShapeDtypeStruct((m, LANES), F32),
            jax.ShapeDtypeStruct((m // MOE_WIN, SUBLANES, MOE_WIN), F32),
            jax.ShapeDtypeStruct((m // MOE_WIN, SUBLANES, LANES), F32),
        ],
        compiler_params=pltpu.CompilerParams(
            dimension_semantics=("parallel",), vmem_limit_bytes=VMEM_LIMIT),
        name="pre_moe",
    )(x, ya, yb, ga, gb, wout, gf, rw, rb)


def _moe_kernel(cnt_sm, hn_ref, gates_ref, d_ref, dt_ref, wg_ref, wu_ref, wd_ref, x1_hbm, gfin_ref, o_hbm,
                xs, gs, sb, gsb, x1buf, x1sem, obuf, osem, pn_sm, s_sm, off_sm, est_sm, tot_sm,
                *, n_win, final):
    blk = pl.program_id(0)
    e = pl.program_id(1)

    def window_rows(w):
        return pl.ds(pl.multiple_of((blk * n_win + w) * MOE_WIN, MOE_WIN), MOE_WIN)

    def x1_copy(w, slot):
        return pltpu.make_async_copy(x1_hbm.at[window_rows(w), :], x1buf.at[slot], x1sem.at[slot])

    def out_copy(w, slot):
        return pltpu.make_async_copy(obuf.at[slot], o_hbm.at[window_rows(w), :], osem.at[slot])

    def seg_copy(w, ee, to_sorted):
        s0 = s_sm[w * N_EXPERTS + ee]
        o0 = off_sm[w * N_EXPERTS + ee]

        def move(i, rows):
            src = pl.multiple_of(s0 + i * SEG, SEG)
            dst = pl.multiple_of(o0 + i * SEG, SEG)
            if to_sorted:
                xs[pl.ds(dst, rows), :] = sb[pl.ds(src, rows), :]
                gs[pl.ds(dst, rows), :] = gsb[pl.ds(src, rows), :]
            else:
                sb[pl.ds(src, rows), :] = xs[pl.ds(dst, rows), :]

        def body(i, carry):
            move(2 * i, 2 * SEG)
            return carry

        n_chunks = pn_sm[w * N_EXPERTS + ee] // SEG
        lax.fori_loop(0, n_chunks // 2, body, 0)

        @pl.when(n_chunks % 2 == 1)
        def _():
            move(n_chunks - 1, SEG)

    @pl.when(e == 0)
    def _dispatch():
        pn = [[None] * N_EXPERTS for _ in range(n_win)]
        for w in range(n_win):
            run = jnp.int32(0)
            for ee in range(N_EXPERTS):
                n = cnt_sm[(blk * n_win + w) * N_EXPERTS + ee]
                pn[w][ee] = jnp.bitwise_and(n + (SEG - 1), -SEG)
                pn_sm[w * N_EXPERTS + ee] = pn[w][ee]
                s_sm[w * N_EXPERTS + ee] = run
                run = run + pn[w][ee]
        run = jnp.int32(0)
        for ee in range(N_EXPERTS):
            est_sm[ee] = run
            start = run
            for w in range(n_win):
                off_sm[w * N_EXPERTS + ee] = run
                run = run + pn[w][ee]
            tot_sm[ee] = run - start
        end = pl.multiple_of(run, SEG)
        xs[pl.ds(end, ROW_TILE), :] = jnp.zeros((ROW_TILE, D_MODEL), BF16)
        gs[pl.ds(end, ROW_TILE), :] = jnp.zeros((ROW_TILE, LANES), F32)
        riota = lax.broadcasted_iota(jnp.int32, (SORT_ROWS, MOE_WIN), 0).astype(F32)
        for w in range(n_win):
            rs = slice(w * MOE_WIN, (w + 1) * MOE_WIN)
            g = jnp.where(riota == dt_ref[w, 0:1, :], 1.0, jnp.where(riota == dt_ref[w, 1:2, :], 1.0, 0.0))
            gb = g.astype(BF16)
            sb[...] = _dot(gb, hn_ref[rs, :]).astype(BF16)
            gw = gates_ref[rs, :]
            g_hi = gw.astype(BF16).astype(F32)
            gsb[...] = _dot(gb, (g_hi + pltpu.roll(gw - g_hi, N_EXPERTS, axis=1)).astype(BF16))
            for ee in range(N_EXPERTS):
                seg_copy(w, ee, True)

    start = est_sm[e]
    tot = tot_sm[e]

    def row_tile(r0, size, valid):
        r0 = pl.multiple_of(r0, SEG)
        xt = xs[pl.ds(r0, size), :]
        y = _swiglu(xt, wg_ref, wu_ref, wd_ref)
        lane = lax.broadcasted_iota(jnp.int32, (size, LANES), 1)
        mine = (lane == e) | (lane == e + N_EXPERTS)
        gate = jnp.sum(jnp.where(mine, gs[pl.ds(r0, size), :], 0.0), axis=1, keepdims=True)
        keep = lax.broadcasted_iota(jnp.int32, (size, D_MODEL), 0) < valid
        xs[pl.ds(r0, size), :] = jnp.where(keep, (y * gate).astype(BF16), xt)

    n_full = tot // ROW_TILE

    def full_body(i, carry):
        row_tile(start + i * ROW_TILE, ROW_TILE, ROW_TILE)
        return carry

    lax.fori_loop(0, n_full, full_body, 0)
    rem = tot - n_full * ROW_TILE
    tail = start + n_full * ROW_TILE

    lo = 0
    for size in TAIL_TILES:
        @pl.when((rem > lo) & (rem <= size))
        def _(size=size):
            row_tile(tail, size, rem)
        lo = size

    @pl.when(e == N_EXPERTS - 1)
    def _combine():
        liota = lax.broadcasted_iota(jnp.int32, (MOE_WIN, SORT_ROWS), 1).astype(F32)
        x1_copy(0, 0).start()
        for w in range(n_win):
            rs = slice(w * MOE_WIN, (w + 1) * MOE_WIN)
            slot = w % 2
            if w + 1 < n_win:
                x1_copy(w + 1, 1 - slot).start()
            for ee in range(N_EXPERTS):
                seg_copy(w, ee, False)
            g = jnp.where(liota == d_ref[rs, 0:1], 1.0, jnp.where(liota == d_ref[rs, 1:2], 1.0, 0.0))
            f = _dot(g.astype(BF16), sb[...])
            x1_copy(w, slot).wait()
            x2 = x1buf[slot] + f
            if w >= 2:
                out_copy(w - 2, slot).wait()
            obuf[slot] = _rms(x2, gfin_ref[...]) if final else x2
            out_copy(w, slot).start()
        for w in range(max(n_win - 2, 0), n_win):
            out_copy(w, w % 2).wait()


def _moe(cnt, hn, gates, d, dt, wg, wu, wd, x1, gfin, *, l, tb, final):
    m = hn.shape[0]
    n_win = tb // MOE_WIN
    j = l // 2
    xs_rows = 2 * tb + n_win * N_EXPERTS * SEG + ROW_TILE
    wsel = lambda i, e, c: (j * N_EXPERTS + e, 0, 0)
    grid_spec = pltpu.PrefetchScalarGridSpec(
        num_scalar_prefetch=1,
        grid=(m // tb, N_EXPERTS),
        in_specs=[
            pl.BlockSpec((tb, D_MODEL), lambda i, e, c: (i, 0)),
            pl.BlockSpec((tb, LANES), lambda i, e, c: (i, 0)),
            pl.BlockSpec((tb, LANES), lambda i, e, c: (i, 0)),
            pl.BlockSpec((n_win, SUBLANES, MOE_WIN), lambda i, e, c: (i, 0, 0)),
            pl.BlockSpec((None, D_MODEL, FF_PAD), wsel),
            pl.BlockSpec((None, D_MODEL, FF_PAD), wsel),
            pl.BlockSpec((None, FF_PAD, D_MODEL), wsel),
            pl.BlockSpec(memory_space=pl.ANY),
            pl.BlockSpec((1, D_MODEL), lambda i, e, c: (0, 0)),
        ],
        out_specs=pl.BlockSpec(memory_space=pl.ANY),
        scratch_shapes=[
            pltpu.VMEM((xs_rows, D_MODEL), BF16),
            pltpu.VMEM((xs_rows, LANES), F32),
            pltpu.VMEM((SORT_ROWS, D_MODEL), BF16),
            pltpu.VMEM((SORT_ROWS, LANES), F32),
            pltpu.VMEM((2, MOE_WIN, D_MODEL), F32),
            pltpu.SemaphoreType.DMA((2,)),
            pltpu.VMEM((2, MOE_WIN, D_MODEL), F32),
            pltpu.SemaphoreType.DMA((2,)),
            pltpu.SMEM((n_win * N_EXPERTS,), jnp.int32),
            pltpu.SMEM((n_win * N_EXPERTS,), jnp.int32),
            pltpu.SMEM((n_win * N_EXPERTS,), jnp.int32),
            pltpu.SMEM((N_EXPERTS,), jnp.int32),
            pltpu.SMEM((N_EXPERTS,), jnp.int32),
        ],
    )
    return pl.pallas_call(
        functools.partial(_moe_kernel, n_win=n_win, final=final),
        grid_spec=grid_spec,
        out_shape=jax.ShapeDtypeStruct((m, D_MODEL), F32),
        compiler_params=pltpu.CompilerParams(
            dimension_semantics=("arbitrary", "arbitrary"), vmem_limit_bytes=MOE_VMEM_LIMIT),
        name="moe",
    )(cnt, hn, gates, d, dt, wg, wu, wd, x1, gfin)


def kernel(x_prompt, x_sample, state_ssm_re, state_ssm_im, norm_mix, w_in, ssm_a_re, ssm_a_im, ssm_log_dt, ssm_b_re, ssm_b_im, ssm_c_re, ssm_c_im, ssm_d, glu_w, glu_b, sgu_w, sgu_b, sgu_ln_g, sgu_ln_b, out_norm_a, out_norm_b, w_out, norm_ffn, ffn_w_gate, ffn_w_up, ffn_w_down, router_w, router_b, moe_w_gate, moe_w_up, moe_w_down, norm_final):
    batch, seq = x_prompt.shape[0], x_prompt.shape[1]
    dec_batch, dec_seq = x_sample.shape[0], x_sample.shape[1]
    mp, ms = batch * seq, dec_batch * dec_seq
    n_moe = moe_w_gate.shape[0]

    lam, bbre, bbim = _s5_prep(ssm_a_re, ssm_a_im, ssm_log_dt, ssm_b_re, ssm_b_im)
    ctre = jnp.swapaxes(ssm_c_re, 2, 3).reshape(DEPTH * N_STATE, SSM_GROUP)
    ctim = jnp.swapaxes(ssm_c_im, 2, 3).reshape(DEPTH * N_STATE, SSM_GROUP)
    dskip = ssm_d.reshape(DEPTH, 1, MIX_A)
    glu = glu_w.reshape(DEPTH * MIX_A, 2 * SSM_GROUP)
    gbias = jnp.stack([glu_b[..., :SSM_GROUP].reshape(DEPTH, MIX_A),
                       glu_b[..., SSM_GROUP:].reshape(DEPTH, MIX_A)], axis=1)
    s5w = (lam, bbre, bbim, ctre, ctim, dskip, glu, gbias)
    w_in_b = w_in.astype(BF16)
    w_out_b = w_out.astype(BF16)
    g_mix = norm_mix.reshape(DEPTH, 1, D_MODEL)
    g_a = out_norm_a.reshape(DEPTH, 1, MIX_A)
    g_b = out_norm_b.reshape(DEPTH, 1, MIX_B)
    g_ffn = norm_ffn.reshape(DEPTH, 1, D_MODEL)
    sgu_bias = jnp.repeat(jnp.swapaxes(sgu_b, 1, 2), HD_B, axis=2)
    sgu_bias_s = sgu_bias[:, :dec_seq]
    lng = sgu_ln_g.reshape(DEPTH, 1, MIX_B)
    lnb = sgu_ln_b.reshape(DEPTH, 1, MIX_B)
    coef_s = jnp.repeat(
        jnp.transpose(sgu_w[:, :, :dec_seq, :dec_seq], (0, 2, 3, 1)).reshape(DEPTH, dec_seq * dec_seq, NH_B),
        HD_B, axis=2)
    dense_split = (FF_PAD, D_FF - FF_PAD)
    dense_g = _pack(jnp.swapaxes(ffn_w_gate, 1, 2), dense_split, transpose=True)
    dense_u = _pack(jnp.swapaxes(ffn_w_up, 1, 2), dense_split, transpose=True)
    dense_d = _pack(ffn_w_down, dense_split, transpose=False)
    moe_t = lambda w: jnp.swapaxes(w.reshape(-1, D_MODEL, D_FF_EXPERT), 1, 2)
    moe_g = _pack(moe_t(moe_w_gate), (D_FF_EXPERT,), transpose=True)
    moe_u = _pack(moe_t(moe_w_up), (D_FF_EXPERT,), transpose=True)
    moe_d = _pack(moe_w_down.reshape(-1, D_FF_EXPERT, D_MODEL), (D_FF_EXPERT,), transpose=False)
    rw = jnp.pad(router_w, ((0, 0), (0, 0), (0, LANES - N_EXPERTS)))
    rwh = rw.astype(BF16)
    rw2 = jnp.concatenate([rwh, (rw - rwh.astype(F32)).astype(BF16)], axis=-1)
    rb = jnp.pad(router_b, ((0, 0), (0, LANES - N_EXPERTS))).reshape(n_moe, 1, LANES)
    gfin = norm_final.reshape(1, D_MODEL)

    h0re = state_ssm_re.reshape(DEPTH, dec_batch, N_STATE)
    h0im = state_ssm_im.reshape(DEPTH, dec_batch, N_STATE)

    assert DEPTH % 2 == 0

    def post_mixer(x, ya, yb, l, tm, tb):
        common = (g_a, g_b, w_out_b, g_ffn)
        if l % 2 == 0:
            return _ffn(x, ya, yb, *common, dense_g, dense_u, dense_d, l=l, tm=tm)
        x1, hn, gates, d, dt, cnt = _pre_moe(x, ya, yb, *common, rw2, rb, l=l, tm=min(2 * tm, tb))
        cnt = cnt[:, 0, :N_EXPERTS].astype(jnp.int32).reshape(-1)
        return _moe(cnt, hn, gates, d, dt, moe_g, moe_u, moe_d, x1, gfin, l=l, tb=tb, final=(l == DEPTH - 1))

    xp = x_prompt.reshape(mp, D_MODEL)
    xs = x_sample.reshape(ms, D_MODEL)
    re_p, im_p, re_s, im_s, v_s = [], [], [], [], []
    for l in range(DEPTH):
        ya_p, yb_p, hre_p, him_p = _mixer(xp.reshape(batch, seq, D_MODEL), g_mix, w_in_b, sgu_w, sgu_bias,
                                          lng, lnb, s5w, l)
        zs = _inproj(xs, g_mix, w_in_b, l, tm=ms)
        ya_s, hre_s, him_s = _s5_sample(zs, h0re, h0im, s5w, l, nb=dec_batch, t_blk=dec_seq)
        yb_s, vn_s = _sgu_sample(zs, coef_s, sgu_bias_s, lng, lnb, l, nb=dec_batch, t_blk=dec_seq)
        xp = post_mixer(xp, ya_p.reshape(mp, MIX_A), yb_p.reshape(mp, MIX_B), l, tm=512, tb=seq)
        xs = post_mixer(xs, ya_s, yb_s, l, tm=ms, tb=ms)
        re_p.append(hre_p)
        im_p.append(him_p)
        re_s.append(hre_s)
        im_s.append(him_s)
        v_s.append(vn_s)

    st = lambda hs, b: jnp.stack(hs).reshape(DEPTH, b, NG_A, SSM_STATE)
    return (xp.reshape(batch, seq, D_MODEL), xs.reshape(dec_batch, dec_seq, D_MODEL),
            st(re_p, batch), st(im_p, batch), st(re_s, dec_batch), st(im_s, dec_batch),
            jnp.stack(v_s).reshape(DEPTH, dec_batch, dec_seq, NH_B, HD_B))
```

```python
import functools
import math

import jax
import jax.numpy as jnp
from jax import lax
from jax.experimental import pallas as pl
from jax.experimental.pallas import tpu as pltpu

F32 = jnp.float32
BF16 = jnp.bfloat16

D_MODEL = 1024
DEPTH = 4
MIX_A = 512
SSM_GROUP = 16
NG_A = 32
SSM_STATE = 64
N_STATE = NG_A * SSM_STATE
MIX_B = 512
HD_B = 64
NH_B = 8
CHUNK = 128
IN_WIDTH = MIX_A + 2 * MIX_B
D_FF = 2752
N_EXPERTS = 8
D_FF_EXPERT = D_FF // 2
FF_PAD = 1408
EPS = 1e-6

LANES = 128
SUBLANES = 8
MXU_DIM = 256
STATE_COLS = 512
N_SCOL = N_STATE // STATE_COLS
GROUPS_PER_CHUNK = STATE_COLS // SSM_STATE
VMEM_LIMIT = 52 * 1024 * 1024
MOE_VMEM_LIMIT = 58 * 1024 * 1024


def _rms(x, g):
    return x * lax.rsqrt(jnp.mean(x * x, axis=-1, keepdims=True) + EPS) * g


def _gelu(x):
    c = math.sqrt(2.0 / math.pi)
    return 0.5 * x * (1.0 + jnp.tanh(c * (x + 0.044715 * (x * x * x))))


def _sigmoid(x):
    return 1.0 / (1.0 + jnp.exp(-x))


def _split_bf16(x):
    hi = x.astype(BF16)
    lo = (x - hi.astype(F32)).astype(BF16)
    return hi, lo


def _dot(a, b):
    return jnp.dot(a, b, preferred_element_type=F32)


def _log2(n):
    assert n & (n - 1) == 0
    return n.bit_length() - 1


def _block_diag_tile(src, n_rep, rows_per_group):
    r_n, k = src.shape
    n = k * n_rep
    t_row = lax.broadcasted_iota(jnp.int32, (k, n), 0)
    t_col = lax.broadcasted_iota(jnp.int32, (k, n), 1)
    tile = jnp.where(jnp.bitwise_and(t_col, k - 1) == t_row, 1.0, 0.0).astype(BF16)
    full = _dot(src.astype(BF16), tile)
    row = lax.broadcasted_iota(jnp.int32, (r_n, n), 0)
    col = lax.broadcasted_iota(jnp.int32, (r_n, n), 1)
    same = lax.shift_right_logical(row, _log2(rows_per_group)) == lax.shift_right_logical(col, _log2(k))
    return jnp.where(same, full, 0.0).astype(BF16)


def _pack_kernel(w_ref, o_ref, *, widths, transpose):
    for i, width in enumerate(widths):
        w = w_ref[i * FF_PAD:i * FF_PAD + width, :]
        if width < FF_PAD:
            w = jnp.concatenate([w, jnp.zeros((FF_PAD - width, w.shape[1]), F32)], axis=0)
        o_ref[i] = (w.T if transpose else w).astype(BF16)


def _pack(w, widths, transpose):
    n_e, n_parts = w.shape[0], len(widths)
    out_blk = (n_parts, D_MODEL, FF_PAD) if transpose else (n_parts, FF_PAD, D_MODEL)
    out = pl.pallas_call(
        functools.partial(_pack_kernel, widths=widths, transpose=transpose),
        grid=(n_e,),
        in_specs=[pl.BlockSpec((None,) + w.shape[1:], lambda e: (e, 0, 0))],
        out_specs=pl.BlockSpec((None,) + out_blk, lambda e: (e, 0, 0, 0)),
        out_shape=jax.ShapeDtypeStruct((n_e,) + out_blk, BF16),
        compiler_params=pltpu.CompilerParams(
            dimension_semantics=("parallel",), vmem_limit_bytes=VMEM_LIMIT),
        name="pack_t" if transpose else "pack",
    )(w)
    return out.reshape((n_e * n_parts,) + out_blk[1:])


def _s5_prep_kernel(are_ref, aim_ref, ldt_ref, bre_ref, bim_ref,
                    abre_ref, abim_ref, bbre_ref, bbim_ref):
    lam_re = are_ref[...]
    lam_im = aim_ref[...]
    dt = jnp.exp(ldt_ref[...])
    mag = jnp.exp(lam_re * dt)
    ab_re = mag * jnp.cos(lam_im * dt)
    ab_im = mag * jnp.sin(lam_im * dt)
    den = lam_re * lam_re + lam_im * lam_im
    nr = ab_re - 1.0
    q_re = (nr * lam_re + ab_im * lam_im) / den
    q_im = (ab_im * lam_re - nr * lam_im) / den
    b_re = bre_ref[...]
    b_im = bim_ref[...]
    abre_ref[...] = ab_re
    abim_ref[...] = ab_im
    bbre_ref[...] = q_re * b_re - q_im * b_im
    bbim_ref[...] = q_re * b_im + q_im * b_re


def _s5_prep(ssm_a_re, ssm_a_im, ssm_log_dt, ssm_b_re, ssm_b_im):
    shp = (DEPTH, NG_A, SSM_GROUP, SSM_STATE)
    rows = DEPTH * NG_A * SSM_GROUP
    bc = lambda a: jnp.broadcast_to(a, shp).reshape(rows, SSM_STATE)
    are = bc(ssm_a_re[:, :, None, :])
    aim = bc(ssm_a_im[:, :, None, :])
    ldt = bc(ssm_log_dt[:, :, None, None])
    bre = jnp.swapaxes(ssm_b_re, 2, 3).reshape(rows, SSM_STATE)
    bim = jnp.swapaxes(ssm_b_im, 2, 3).reshape(rows, SSM_STATE)
    sds = jax.ShapeDtypeStruct((rows, SSM_STATE), F32)
    abre, abim, bbre, bbim = pl.pallas_call(
        _s5_prep_kernel, out_shape=(sds, sds, sds, sds), name="s5_prep",
    )(are, aim, ldt, bre, bim)
    lam_re = abre.reshape(shp)[:, :, 0, :].reshape(DEPTH, 1, N_STATE)
    lam_im = abim.reshape(shp)[:, :, 0, :].reshape(DEPTH, 1, N_STATE)
    lam = jnp.concatenate([lam_re, lam_im], axis=1)
    return lam, bbre, bbim


def _inproj_kernel(x_ref, g_ref, w_ref, z_ref):
    hn = _rms(x_ref[...], g_ref[...])
    z_ref[...] = _dot(hn.astype(BF16), w_ref[...])


def _inproj(x, g, w, l, tm):
    m = x.shape[0]
    return pl.pallas_call(
        _inproj_kernel,
        grid=(m // tm,),
        in_specs=[
            pl.BlockSpec((tm, D_MODEL), lambda i: (i, 0)),
            pl.BlockSpec((None, 1, D_MODEL), lambda i: (l, 0, 0)),
            pl.BlockSpec((None, D_MODEL, IN_WIDTH), lambda i: (l, 0, 0)),
        ],
        out_specs=pl.BlockSpec((tm, IN_WIDTH), lambda i: (i, 0)),
        out_shape=jax.ShapeDtypeStruct((m, IN_WIDTH), F32),
        compiler_params=pltpu.CompilerParams(
            dimension_semantics=("parallel",), vmem_limit_bytes=VMEM_LIMIT),
        name="inproj",
    )(x, g, w)


def _s5_expand(bbre_ref, bbim_ref, ctre_ref, ctim_ref, glu_ref, bblk, cre, cimn, wa, wg):
    for c in range(4):
        rs = slice(c * LANES, (c + 1) * LANES)
        bblk[c, :, :STATE_COLS] = _block_diag_tile(bbre_ref[rs, :], GROUPS_PER_CHUNK, SSM_GROUP)
        bblk[c, :, STATE_COLS:] = _block_diag_tile(bbim_ref[rs, :], GROUPS_PER_CHUNK, SSM_GROUP)
        ss = slice(c * STATE_COLS, (c + 1) * STATE_COLS)
        cre[c] = _block_diag_tile(ctre_ref[ss, :], GROUPS_PER_CHUNK, SSM_STATE)
        cimn[c] = _block_diag_tile(-ctim_ref[ss, :], GROUPS_PER_CHUNK, SSM_STATE)
    for b in range(MIX_A // MXU_DIM):
        ms = slice(b * MXU_DIM, (b + 1) * MXU_DIM)
        wa[b] = _block_diag_tile(glu_ref[ms, :SSM_GROUP], MXU_DIM // SSM_GROUP, SSM_GROUP)
        wg[b] = _block_diag_tile(glu_ref[ms, SSM_GROUP:], MXU_DIM // SSM_GROUP, SSM_GROUP)


def _s5_time_major(utm, lam_ref, d_ref, gbias_ref, hre_ref, him_ref, bre, bim, bblk, cre, cimn, wa, wg,
                   nb, t_blk):
    for c in range(4):
        r = _dot(utm[c].astype(BF16), bblk[c])
        bre[:, c * STATE_COLS:(c + 1) * STATE_COLS] = r[:, :STATE_COLS]
        bim[:, c * STATE_COLS:(c + 1) * STATE_COLS] = r[:, STATE_COLS:]

    for c in range(N_SCOL):
        cs = slice(c * STATE_COLS, (c + 1) * STATE_COLS)
        lr = jnp.broadcast_to(lam_ref[0:1, cs], (SUBLANES, STATE_COLS))
        li = jnp.broadcast_to(lam_ref[1:2, cs], (SUBLANES, STATE_COLS))

        if nb == SUBLANES:
            hr, hi = hre_ref[:, cs], him_ref[:, cs]
            for t in range(t_blk):
                rs = slice(t * SUBLANES, (t + 1) * SUBLANES)
                hr, hi = lr * hr - li * hi + bre[rs, cs], lr * hi + li * hr + bim[rs, cs]
                bre[rs, cs] = hr
                bim[rs, cs] = hi
            hre_ref[:, cs] = hr
            him_ref[:, cs] = hi
            continue

        def group_body(bg, carry, cs=cs, lr=lr, li=li):
            r0 = pl.multiple_of(bg * SUBLANES, SUBLANES)

            def t_body(t, h):
                hr, hi = h
                row = pl.multiple_of(t * nb + r0, SUBLANES)
                nr = lr * hr - li * hi + bre[pl.ds(row, SUBLANES), cs]
                ni = lr * hi + li * hr + bim[pl.ds(row, SUBLANES), cs]
                bre[pl.ds(row, SUBLANES), cs] = nr
                bim[pl.ds(row, SUBLANES), cs] = ni
                return nr, ni

            h0 = (hre_ref[pl.ds(r0, SUBLANES), cs], him_ref[pl.ds(r0, SUBLANES), cs])
            hr, hi = lax.fori_loop(0, t_blk, t_body, h0, unroll=min(t_blk, 8))
            hre_ref[pl.ds(r0, SUBLANES), cs] = hr
            him_ref[pl.ds(r0, SUBLANES), cs] = hi
            return carry

        lax.fori_loop(0, nb // SUBLANES, group_body, 0)

    ys = []
    for c in range(4):
        cs = slice(c * STATE_COLS, (c + 1) * STATE_COLS)
        ls = slice(c * LANES, (c + 1) * LANES)
        y = _dot(bre[:, cs].astype(BF16), cre[c]) + _dot(bim[:, cs].astype(BF16), cimn[c])
        ys.append(_gelu(y + d_ref[:, ls] * utm[c]).astype(BF16))

    os_ = []
    for b in range(MIX_A // MXU_DIM):
        ms = slice(b * MXU_DIM, (b + 1) * MXU_DIM)
        yb = jnp.concatenate(ys[2 * b:2 * b + 2], axis=1)
        za = _dot(yb, wa[b]) + gbias_ref[0:1, ms]
        zg = _dot(yb, wg[b]) + gbias_ref[1:2, ms]
        os_.append(za * _sigmoid(zg))
    return jnp.concatenate(os_, axis=1)


def _s5_kernel(*refs, nb, t_blk):
    (u_ref, h0re_ref, h0im_ref, bbre_ref, bbim_ref, lam_ref, ctre_ref, ctim_ref, d_ref, glu_ref, gbias_ref,
     ya_ref, hre_ref, him_ref, utm, bre, bim, otm, bblk, cre, cimn, wa, wg, tmp) = refs
    hre_ref[...] = h0re_ref[...]
    him_ref[...] = h0im_ref[...]
    _s5_expand(bbre_ref, bbim_ref, ctre_ref, ctim_ref, glu_ref, bblk, cre, cimn, wa, wg)

    for c in range(4):
        ls = slice(c * LANES, (c + 1) * LANES)
        tmp[c] = u_ref[:, ls]
        for t in range(t_blk):
            utm[c, t * nb:(t + 1) * nb, :] = tmp[c, pl.ds(t, nb, stride=t_blk), :]

    o = _s5_time_major(utm, lam_ref, d_ref, gbias_ref, hre_ref, him_ref, bre, bim, bblk, cre, cimn, wa, wg,
                       nb, t_blk)

    for c in range(4):
        ls = slice(c * LANES, (c + 1) * LANES)
        for t in range(t_blk):
            otm[c, pl.ds(t, nb, stride=t_blk), :] = o[t * nb:(t + 1) * nb, ls]
        ya_ref[:, ls] = otm[c]


def _s5_weight_specs(l):
    return [
        pl.BlockSpec((NG_A * SSM_GROUP, SSM_STATE), lambda j: (l, 0)),
        pl.BlockSpec((NG_A * SSM_GROUP, SSM_STATE), lambda j: (l, 0)),
        pl.BlockSpec((None, 2, N_STATE), lambda j: (l, 0, 0)),
        pl.BlockSpec((N_STATE, SSM_GROUP), lambda j: (l, 0)),
        pl.BlockSpec((N_STATE, SSM_GROUP), lambda j: (l, 0)),
        pl.BlockSpec((None, 1, MIX_A), lambda j: (l, 0, 0)),
        pl.BlockSpec((MIX_A, 2 * SSM_GROUP), lambda j: (l, 0)),
        pl.BlockSpec((None, 2, MIX_A), lambda j: (l, 0, 0)),
    ]


def _s5_scratch(rows):
    return [
        pltpu.VMEM((4, rows, LANES), F32),
        pltpu.VMEM((rows, N_STATE), F32),
        pltpu.VMEM((rows, N_STATE), F32),
        pltpu.VMEM((4, rows, LANES), F32),
        pltpu.VMEM((4, LANES, 2 * STATE_COLS), BF16),
        pltpu.VMEM((4, STATE_COLS, LANES), BF16),
        pltpu.VMEM((4, STATE_COLS, LANES), BF16),
        pltpu.VMEM((MIX_A // MXU_DIM, MXU_DIM, MXU_DIM), BF16),
        pltpu.VMEM((MIX_A // MXU_DIM, MXU_DIM, MXU_DIM), BF16),
    ]


def _s5_sample(z, h0re, h0im, weights, l, nb, t_blk):
    rows = nb * t_blk
    lam, bbre, bbim, ctre, ctim, dskip, glu, gbias = weights
    blk = pl.BlockSpec((rows, MIX_A), lambda j: (0, 0))
    st_in = pl.BlockSpec((None, nb, N_STATE), lambda j: (l, 0, 0))
    st_out = pl.BlockSpec((nb, N_STATE), lambda j: (0, 0))
    return pl.pallas_call(
        functools.partial(_s5_kernel, nb=nb, t_blk=t_blk),
        grid=(1,),
        in_specs=[blk, st_in, st_in] + _s5_weight_specs(l),
        out_specs=[blk, st_out, st_out],
        out_shape=[
            jax.ShapeDtypeStruct((rows, MIX_A), F32),
            jax.ShapeDtypeStruct((nb, N_STATE), F32),
            jax.ShapeDtypeStruct((nb, N_STATE), F32),
        ],
        scratch_shapes=_s5_scratch(rows) + [pltpu.VMEM((4, rows, LANES), F32)],
        compiler_params=pltpu.CompilerParams(
            dimension_semantics=("arbitrary",), vmem_limit_bytes=VMEM_LIMIT),
        name="s5_sample",
    )(z, h0re, h0im, bbre, bbim, lam, ctre, ctim, dskip, glu, gbias)


def _group_layernorm(gv, lng_ref, lnb_ref):
    row = lax.broadcasted_iota(jnp.int32, (MXU_DIM, MXU_DIM), 0)
    col = lax.broadcasted_iota(jnp.int32, (MXU_DIM, MXU_DIM), 1)
    same = lax.shift_right_logical(row, _log2(HD_B)) == lax.shift_right_logical(col, _log2(HD_B))
    gm = jnp.where(same, 1.0 / HD_B, 0.0).astype(BF16)

    def gmean(a):
        hi, lo = _split_bf16(a)
        parts = []
        for b in range(MIX_B // MXU_DIM):
            ms = slice(b * MXU_DIM, (b + 1) * MXU_DIM)
            parts.append(_dot(hi[:, ms], gm) + _dot(lo[:, ms], gm))
        return jnp.concatenate(parts, axis=1)

    xc = gv - gmean(gv)
    var = gmean(xc * xc)
    return xc * lax.rsqrt(var + EPS) * lng_ref[...] + lnb_ref[...]


INPROJ_ROWS = 256


def _mixer_kernel(x_ref, g_ref, w_ref, ws_ref, bias_ref, lng_ref, lnb_ref,
                  bbre_ref, bbim_ref, lam_ref, ctre_ref, ctim_ref, d_ref, glu_ref, gbias_ref,
                  ya_ref, yb_ref, hre_ref, him_ref,
                  hnb, ua, zb, utm, bre, bim, otm, bblk, cre, cimn, wa, wg, *, nb):
    @pl.when(pl.program_id(0) == 0)
    def _():
        hre_ref[...] = jnp.zeros_like(hre_ref)
        him_ref[...] = jnp.zeros_like(him_ref)
        _s5_expand(bbre_ref, bbim_ref, ctre_ref, ctim_ref, glu_ref, bblk, cre, cimn, wa, wg)

    for b in range(nb):
        hnb[b * CHUNK:(b + 1) * CHUNK, :] = _rms(x_ref[b], g_ref[...]).astype(BF16)
    for m0 in range(0, nb * CHUNK, INPROJ_ROWS):
        ua[m0:m0 + INPROJ_ROWS, :] = _dot(hnb[m0:m0 + INPROJ_ROWS, :], w_ref[:, :MIX_A])
    for c in range(4):
        ls = slice(c * LANES, (c + 1) * LANES)
        for b in range(nb):
            utm[c, pl.ds(b, CHUNK, stride=nb), :] = ua[b * CHUNK:(b + 1) * CHUNK, ls]
    for m0 in range(0, nb * CHUNK, INPROJ_ROWS):
        zb[m0:m0 + INPROJ_ROWS, :] = _dot(hnb[m0:m0 + INPROJ_ROWS, :], w_ref[:, MIX_A:])

    row = lax.broadcasted_iota(jnp.int32, (CHUNK, 2 * CHUNK), 0)
    col = lax.broadcasted_iota(jnp.int32, (CHUNK, 2 * CHUNK), 1)
    causal = jnp.bitwise_and(col, CHUNK - 1) <= row
    lane = lax.broadcasted_iota(jnp.int32, (CHUNK, LANES), 1)
    first_head = lane < HD_B
    wcat = []
    for p in range(NH_B // 2):
        w = jnp.concatenate([ws_ref[2 * p], ws_ref[2 * p + 1]], axis=1)
        wcat.append(jnp.where(causal, w, 0.0).astype(BF16))
    for b in range(nb):
        rs = slice(b * CHUNK, (b + 1) * CHUNK)
        u = _gelu(zb[rs, :MIX_B])
        vn = _group_layernorm(_gelu(zb[rs, MIX_B:]), lng_ref, lnb_ref)
        for p in range(NH_B // 2):
            ls = slice(p * LANES, (p + 1) * LANES)
            vp = vn[:, ls]
            rhs = jnp.concatenate(
                [jnp.where(first_head, vp, 0.0), jnp.where(first_head, 0.0, vp)], axis=0).astype(BF16)
            s = _dot(wcat[p], rhs)
            yb_ref[b, :, ls] = u[:, ls] * (s + bias_ref[:, ls])

    o = _s5_time_major(utm, lam_ref, d_ref, gbias_ref, hre_ref, him_ref, bre, bim, bblk, cre, cimn, wa, wg,
                       nb, CHUNK)
    for c in range(4):
        ls = slice(c * LANES, (c + 1) * LANES)
        otm[c] = o[:, ls]
        for b in range(nb):
            ya_ref[b, :, ls] = otm[c, pl.ds(b, CHUNK, stride=nb), :]


def _mixer(x3, g, w, ws, bias, lng, lnb, s5w, l):
    nb, seq = x3.shape[0], x3.shape[1]
    rows = nb * CHUNK
    lam, bbre, bbim, ctre, ctim, dskip, glu, gbias = s5w
    lay3 = lambda j: (l, 0, 0)
    blk = lambda width: pl.BlockSpec((nb, CHUNK, width), lambda j: (0, j, 0))
    st = pl.BlockSpec((nb, N_STATE), lambda j: (0, 0))
    return pl.pallas_call(
        functools.partial(_mixer_kernel, nb=nb),
        grid=(seq // CHUNK,),
        in_specs=[
            blk(D_MODEL),
            pl.BlockSpec((None, 1, D_MODEL), lay3),
            pl.BlockSpec((None, D_MODEL, IN_WIDTH), lay3, pipeline_mode=pl.Buffered(1)),
            pl.BlockSpec((None, NH_B, CHUNK, CHUNK), lambda j: (l, 0, 0, 0)),
            pl.BlockSpec((None, CHUNK, MIX_B), lay3),
            pl.BlockSpec((None, 1, MIX_B), lay3),
            pl.BlockSpec((None, 1, MIX_B), lay3),
        ] + _s5_weight_specs(l),
        out_specs=[blk(MIX_A), blk(MIX_B), st, st],
        out_shape=[
            jax.ShapeDtypeStruct((nb, seq, MIX_A), F32),
            jax.ShapeDtypeStruct((nb, seq, MIX_B), F32),
            jax.ShapeDtypeStruct((nb, N_STATE), F32),
            jax.ShapeDtypeStruct((nb, N_STATE), F32),
        ],
        scratch_shapes=[
            pltpu.VMEM((rows, D_MODEL), BF16),
            pltpu.VMEM((rows, MIX_A), F32),
            pltpu.VMEM((rows, 2 * MIX_B), F32),
        ] + _s5_scratch(rows),
        compiler_params=pltpu.CompilerParams(
            dimension_semantics=("arbitrary",), vmem_limit_bytes=MOE_VMEM_LIMIT),
        name="mixer",
    )(x3, g, w, ws, bias, lng, lnb, bbre, bbim, lam, ctre, ctim, dskip, glu, gbias)


def _sgu_sample_kernel(u_ref, v_ref, coef_ref, bias_ref, lng_ref, lnb_ref, o_ref, vn_ref,
                       usc, vsc, osc, *, nb, t_blk):
    vn = _group_layernorm(_gelu(v_ref[...]), lng_ref, lnb_ref)
    vn_ref[...] = vn
    u = _gelu(u_ref[...])
    for c in range(4):
        ls = slice(c * LANES, (c + 1) * LANES)
        vsc[c] = vn[:, ls]
        usc[c] = u[:, ls]
        for q in range(t_blk):
            s = bias_ref[q:q + 1, ls]
            for k in range(q + 1):
                s = s + coef_ref[q * t_blk + k:q * t_blk + k + 1, ls] * vsc[c, pl.ds(k, nb, stride=t_blk), :]
            osc[c, pl.ds(q, nb, stride=t_blk), :] = usc[c, pl.ds(q, nb, stride=t_blk), :] * s
        o_ref[:, ls] = osc[c]


def _sgu_sample(z, coef, bias, lng, lnb, l, nb, t_blk):
    m = nb * t_blk
    fix2 = lambda i: (0, 0)
    lay3 = lambda i: (l, 0, 0)
    return pl.pallas_call(
        functools.partial(_sgu_sample_kernel, nb=nb, t_blk=t_blk),
        grid=(1,),
        in_specs=[
            pl.BlockSpec((m, MIX_B), lambda i: (0, 1)),
            pl.BlockSpec((m, MIX_B), lambda i: (0, 2)),
            pl.BlockSpec((None, t_blk * t_blk, MIX_B), lay3),
            pl.BlockSpec((None, t_blk, MIX_B), lay3),
            pl.BlockSpec((None, 1, MIX_B), lay3),
            pl.BlockSpec((None, 1, MIX_B), lay3),
        ],
        out_specs=[pl.BlockSpec((m, MIX_B), fix2), pl.BlockSpec((m, MIX_B), fix2)],
        out_shape=[jax.ShapeDtypeStruct((m, MIX_B), F32), jax.ShapeDtypeStruct((m, MIX_B), F32)],
        scratch_shapes=[pltpu.VMEM((4, m, LANES), F32)] * 3,
        compiler_params=pltpu.CompilerParams(
            dimension_semantics=("arbitrary",), vmem_limit_bytes=VMEM_LIMIT),
        name="sgu_sample",
    )(z, z, coef, bias, lng, lnb)


def _swiglu(x, wg_ref, wu_ref, wd_ref):
    hg = _dot(x, wg_ref[...])
    a = (hg * _sigmoid(hg) * _dot(x, wu_ref[...])).astype(BF16)
    return _dot(a, wd_ref[...])


def _out_proj(x_ref, ya_ref, yb_ref, ga_ref, gb_ref, wout_ref):
    na = _rms(ya_ref[...], ga_ref[...]).astype(BF16)
    nb = _rms(yb_ref[...], gb_ref[...]).astype(BF16)
    return x_ref[...] + _dot(na, wout_ref[0:MIX_A, :]) + _dot(nb, wout_ref[MIX_A:, :])


N_DENSE_PARTS = 2


def _ffn_kernel(x_ref, ya_ref, yb_ref, ga_ref, gb_ref, wout_ref, gf_ref, wg_ref, wu_ref, wd_ref, o_ref):
    x1 = _out_proj(x_ref, ya_ref, yb_ref, ga_ref, gb_ref, wout_ref)
    hn = _rms(x1, gf_ref[...]).astype(BF16)
    o_ref[...] = x1
    for p in range(N_DENSE_PARTS):
        o_ref[...] += _swiglu(hn, wg_ref.at[p], wu_ref.at[p], wd_ref.at[p])


def _ffn(x, ya, yb, ga, gb, wout, gf, wg, wu, wd, *, l, tm):
    m = x.shape[0]
    j = l // 2
    one = pl.Buffered(1)
    lay3 = lambda i: (l, 0, 0)
    wsel = lambda i: (j, 0, 0)
    return pl.pallas_call(
        _ffn_kernel,
        grid=(m // tm,),
        in_specs=[
            pl.BlockSpec((tm, D_MODEL), lambda i: (i, 0)),
            pl.BlockSpec((tm, MIX_A), lambda i: (i, 0)),
            pl.BlockSpec((tm, MIX_B), lambda i: (i, 0)),
            pl.BlockSpec((None, 1, MIX_A), lay3),
            pl.BlockSpec((None, 1, MIX_B), lay3),
            pl.BlockSpec((None, D_MODEL, D_MODEL), lay3, pipeline_mode=one),
            pl.BlockSpec((None, 1, D_MODEL), lay3),
            pl.BlockSpec((N_DENSE_PARTS, D_MODEL, FF_PAD), wsel, pipeline_mode=one),
            pl.BlockSpec((N_DENSE_PARTS, D_MODEL, FF_PAD), wsel, pipeline_mode=one),
            pl.BlockSpec((N_DENSE_PARTS, FF_PAD, D_MODEL), wsel, pipeline_mode=one),
        ],
        out_specs=pl.BlockSpec((tm, D_MODEL), lambda i: (i, 0)),
        out_shape=jax.ShapeDtypeStruct((m, D_MODEL), F32),
        compiler_params=pltpu.CompilerParams(
            dimension_semantics=("parallel",), vmem_limit_bytes=VMEM_LIMIT),
        name="ffn_dense",
    )(x, ya, yb, ga, gb, wout, gf, wg, wu, wd)


MOE_WIN = 256
SEG = 16
SORT_ROWS = 2 * MOE_WIN + N_EXPERTS * SEG
ROW_TILE = 256
TAIL_TILES = (ROW_TILE // 2, ROW_TILE)
NOT_ROUTED = -1.0e6


def _top2(logits):
    lane = lax.broadcasted_iota(jnp.int32, logits.shape, 1).astype(F32)
    neg = jnp.float32(-jnp.inf)
    lg = jnp.where(lane < N_EXPERTS, logits, neg)
    m1 = jnp.max(lg, axis=1, keepdims=True)
    i1 = jnp.min(jnp.where(lg == m1, lane, float(LANES)), axis=1, keepdims=True)
    lg2 = jnp.where(lane == i1, neg, lg)
    m2 = jnp.max(lg2, axis=1, keepdims=True)
    i2 = jnp.min(jnp.where(lg2 == m2, lane, float(LANES)), axis=1, keepdims=True)
    ex = jnp.exp(m2 - m1)
    w1 = 1.0 / (1.0 + ex)
    w2 = ex / (1.0 + ex)
    gates = jnp.where(lane == i1, w1, 0.0) + jnp.where(lane == i2, w2, 0.0)
    mask = jnp.where((lane == i1) | (lane == i2), 1.0, 0.0)
    return gates, mask


def _pre_moe_kernel(x_ref, ya_ref, yb_ref, ga_ref, gb_ref, wout_ref, gf_ref, rw_ref, rb_ref,
                    x1_ref, hn_ref, gates_ref, d_ref, dt_ref, cnt_ref, *, n_win):
    x1 = _out_proj(x_ref, ya_ref, yb_ref, ga_ref, gb_ref, wout_ref)
    x1_ref[...] = x1
    hn = _rms(x1, gf_ref[...])
    hn_ref[...] = hn.astype(BF16)
    hi, lo = _split_bf16(hn)
    hw = _dot(hi, rw_ref[...])
    logits = hw[:, :LANES] + hw[:, LANES:] + _dot(lo, rw_ref[:, :LANES])
    gates, mask = _top2(logits + rb_ref[...])
    gates_ref[...] = gates
    row = lax.broadcasted_iota(jnp.int32, (MOE_WIN, MOE_WIN), 0)
    col = lax.broadcasted_iota(jnp.int32, (MOE_WIN, MOE_WIN), 1)
    earlier = jnp.where(col < row, 1.0, 0.0).astype(BF16)
    erow = lax.broadcasted_iota(jnp.int32, (LANES, LANES), 0)
    ecol = lax.broadcasted_iota(jnp.int32, (LANES, LANES), 1)
    lower_expert = jnp.where(erow < ecol, 1.0, 0.0).astype(BF16)
    lane = lax.broadcasted_iota(jnp.int32, (MOE_WIN, LANES), 1)
    for w in range(n_win):
        rs = slice(w * MOE_WIN, (w + 1) * MOE_WIN)
        mw = mask[rs, :]
        rank = _dot(earlier, mw.astype(BF16))
        cnt = jnp.broadcast_to(jnp.sum(mw, axis=0, keepdims=True), (SUBLANES, LANES))
        cnt_ref[w] = cnt
        padded = jnp.floor((cnt + (SEG - 1)) * (1.0 / SEG)) * SEG
        seg_start = _dot(padded.astype(BF16), lower_expert)[0:1, :]
        dest = rank + seg_start
        d_lo = jnp.min(jnp.where(mw > 0.0, dest, -NOT_ROUTED), axis=1, keepdims=True)
        d_hi = jnp.max(jnp.where(mw > 0.0, dest, NOT_ROUTED), axis=1, keepdims=True)
        d = jnp.where(lane == 0, d_lo, jnp.where(lane == 1, d_hi, 0.0))
        d_ref[rs, :] = d
        dt_ref[w] = d.T[:SUBLANES, :]


def _pre_moe(x, ya, yb, ga, gb, wout, gf, rw, rb, *, l, tm):
    m = x.shape[0]
    n_win = tm // MOE_WIN
    j = l // 2
    lay3 = lambda i: (l, 0, 0)
    moe3 = lambda i: (j, 0, 0)
    return pl.pallas_call(
        functools.partial(_pre_moe_kernel, n_win=n_win),
        grid=(m // tm,),
        in_specs=[
            pl.BlockSpec((tm, D_MODEL), lambda i: (i, 0)),
            pl.BlockSpec((tm, MIX_A), lambda i: (i, 0)),
            pl.BlockSpec((tm, MIX_B), lambda i: (i, 0)),
            pl.BlockSpec((None, 1, MIX_A), lay3),
            pl.BlockSpec((None, 1, MIX_B), lay3),
            pl.BlockSpec((None, D_MODEL, D_MODEL), lay3),
            pl.BlockSpec((None, 1, D_MODEL), lay3),
            pl.BlockSpec((None, D_MODEL, 2 * LANES), moe3),
            pl.BlockSpec((None, 1, LANES), moe3),
        ],
        out_specs=[
            pl.BlockSpec((tm, D_MODEL), lambda i: (i, 0)),
            pl.BlockSpec((tm, D_MODEL), lambda i: (i, 0)),
            pl.BlockSpec((tm, LANES), lambda i: (i, 0)),
            pl.BlockSpec((tm, LANES), lambda i: (i, 0)),
            pl.BlockSpec((n_win, SUBLANES, MOE_WIN), lambda i: (i, 0, 0)),
            pl.BlockSpec((n_win, SUBLANES, LANES), lambda i: (i, 0, 0)),
        ],
        out_shape=[
            jax.ShapeDtypeStruct((m, D_MODEL), F32),
            jax.ShapeDtypeStruct((m, D_MODEL), BF16),
            jax.ShapeDtypeStruct((m, LANES), F32),
            jax.ShapeDtypeStruct((m, LANES), F32),
            jax.ShapeDtypeStruct((m // MOE_WIN, SUBLANES, MOE_WIN), F32),
            jax.ShapeDtypeStruct((m // MOE_WIN, SUBLANES, LANES), F32),
        ],
        compiler_params=pltpu.CompilerParams(
            dimension_semantics=("parallel",), vmem_limit_bytes=VMEM_LIMIT),
        name="pre_moe",
    )(x, ya, yb, ga, gb, wout, gf, rw, rb)


def _moe_kernel(cnt_sm, hn_ref, gates_ref, d_ref, dt_ref, wg_ref, wu_ref, wd_ref, x1_hbm, gfin_ref, o_hbm,
                xs, gs, sb, gsb, x1buf, x1sem, obuf, osem, pn_sm, s_sm, off_sm, est_sm, tot_sm,
                *, n_win, final):
    blk = pl.program_id(0)
    e = pl.program_id(1)

    def window_rows(w):
        return pl.ds(pl.multiple_of((blk * n_win + w) * MOE_WIN, MOE_WIN), MOE_WIN)

    def x1_copy(w, slot):
        return pltpu.make_async_copy(x1_hbm.at[window_rows(w), :], x1buf.at[slot], x1sem.at[slot])

    def out_copy(w, slot):
        return pltpu.make_async_copy(obuf.at[slot], o_hbm.at[window_rows(w), :], osem.at[slot])

    def seg_copy(w, ee, to_sorted):
        s0 = s_sm[w * N_EXPERTS + ee]
        o0 = off_sm[w * N_EXPERTS + ee]

        def body(i, carry):
            src = pl.multiple_of(s0 + i * SEG, SEG)
            dst = pl.multiple_of(o0 + i * SEG, SEG)
            if to_sorted:
                xs[pl.ds(dst, SEG), :] = sb[pl.ds(src, SEG), :]
                gs[pl.ds(dst, SEG), :] = gsb[pl.ds(src, SEG), :]
            else:
                sb[pl.ds(src, SEG), :] = xs[pl.ds(dst, SEG), :]
            return carry

        lax.fori_loop(0, pn_sm[w * N_EXPERTS + ee] // SEG, body, 0)

    @pl.when(e == 0)
    def _dispatch():
        pn = [[None] * N_EXPERTS for _ in range(n_win)]
        for w in range(n_win):
            run = jnp.int32(0)
            for ee in range(N_EXPERTS):
                n = cnt_sm[(blk * n_win + w) * N_EXPERTS + ee]
                pn[w][ee] = jnp.bitwise_and(n + (SEG - 1), -SEG)
                pn_sm[w * N_EXPERTS + ee] = pn[w][ee]
                s_sm[w * N_EXPERTS + ee] = run
                run = run + pn[w][ee]
        run = jnp.int32(0)
        for ee in range(N_EXPERTS):
            est_sm[ee] = run
            start = run
            for w in range(n_win):
                off_sm[w * N_EXPERTS + ee] = run
                run = run + pn[w][ee]
            tot_sm[ee] = run - start
        end = pl.multiple_of(run, SEG)
        xs[pl.ds(end, ROW_TILE), :] = jnp.zeros((ROW_TILE, D_MODEL), BF16)
        gs[pl.ds(end, ROW_TILE), :] = jnp.zeros((ROW_TILE, LANES), F32)
        riota = lax.broadcasted_iota(jnp.int32, (SORT_ROWS, MOE_WIN), 0).astype(F32)
        for w in range(n_win):
            rs = slice(w * MOE_WIN, (w + 1) * MOE_WIN)
            g = jnp.where(riota == dt_ref[w, 0:1, :], 1.0, jnp.where(riota == dt_ref[w, 1:2, :], 1.0, 0.0))
            gb = g.astype(BF16)
            sb[...] = _dot(gb, hn_ref[rs, :]).astype(BF16)
            gh, gl = _split_bf16(gates_ref[rs, :])
            gsb[...] = _dot(gb, gh) + _dot(gb, gl)
            for ee in range(N_EXPERTS):
                seg_copy(w, ee, True)

    start = est_sm[e]
    tot = tot_sm[e]

    def row_tile(r0, size, valid):
        r0 = pl.multiple_of(r0, SEG)
        xt = xs[pl.ds(r0, size), :]
        y = _swiglu(xt, wg_ref, wu_ref, wd_ref)
        lane = lax.broadcasted_iota(jnp.int32, (size, LANES), 1)
        gate = jnp.sum(jnp.where(lane == e, gs[pl.ds(r0, size), :], 0.0), axis=1, keepdims=True)
        keep = lax.broadcasted_iota(jnp.int32, (size, D_MODEL), 0) < valid
        xs[pl.ds(r0, size), :] = jnp.where(keep, (y * gate).astype(BF16), xt)

    n_full = tot // ROW_TILE

    def full_body(i, carry):
        row_tile(start + i * ROW_TILE, ROW_TILE, ROW_TILE)
        return carry

    lax.fori_loop(0, n_full, full_body, 0)
    rem = tot - n_full * ROW_TILE
    tail = start + n_full * ROW_TILE

    lo = 0
    for size in TAIL_TILES:
        @pl.when((rem > lo) & (rem <= size))
        def _(size=size):
            row_tile(tail, size, rem)
        lo = size

    @pl.when(e == N_EXPERTS - 1)
    def _combine():
        liota = lax.broadcasted_iota(jnp.int32, (MOE_WIN, SORT_ROWS), 1).astype(F32)
        x1_copy(0, 0).start()
        for w in range(n_win):
            rs = slice(w * MOE_WIN, (w + 1) * MOE_WIN)
            slot = w % 2
            if w + 1 < n_win:
                x1_copy(w + 1, 1 - slot).start()
            for ee in range(N_EXPERTS):
                seg_copy(w, ee, False)
            g = jnp.where(liota == d_ref[rs, 0:1], 1.0, jnp.where(liota == d_ref[rs, 1:2], 1.0, 0.0))
            f = _dot(g.astype(BF16), sb[...])
            x1_copy(w, slot).wait()
            x2 = x1buf[slot] + f
            if w >= 2:
                out_copy(w - 2, slot).wait()
            obuf[slot] = _rms(x2, gfin_ref[...]) if final else x2
            out_copy(w, slot).start()
        for w in range(max(n_win - 2, 0), n_win):
            out_copy(w, w % 2).wait()


def _moe(cnt, hn, gates, d, dt, wg, wu, wd, x1, gfin, *, l, tb, final):
    m = hn.shape[0]
    n_win = tb // MOE_WIN
    j = l // 2
    xs_rows = 2 * tb + n_win * N_EXPERTS * SEG + ROW_TILE
    wsel = lambda i, e, c: (j * N_EXPERTS + e, 0, 0)
    grid_spec = pltpu.PrefetchScalarGridSpec(
        num_scalar_prefetch=1,
        grid=(m // tb, N_EXPERTS),
        in_specs=[
            pl.BlockSpec((tb, D_MODEL), lambda i, e, c: (i, 0)),
            pl.BlockSpec((tb, LANES), lambda i, e, c: (i, 0)),
            pl.BlockSpec((tb, LANES), lambda i, e, c: (i, 0)),
            pl.BlockSpec((n_win, SUBLANES, MOE_WIN), lambda i, e, c: (i, 0, 0)),
            pl.BlockSpec((None, D_MODEL, FF_PAD), wsel),
            pl.BlockSpec((None, D_MODEL, FF_PAD), wsel),
            pl.BlockSpec((None, FF_PAD, D_MODEL), wsel),
            pl.BlockSpec(memory_space=pl.ANY),
            pl.BlockSpec((1, D_MODEL), lambda i, e, c: (0, 0)),
        ],
        out_specs=pl.BlockSpec(memory_space=pl.ANY),
        scratch_shapes=[
            pltpu.VMEM((xs_rows, D_MODEL), BF16),
            pltpu.VMEM((xs_rows, LANES), F32),
            pltpu.VMEM((SORT_ROWS, D_MODEL), BF16),
            pltpu.VMEM((SORT_ROWS, LANES), F32),
            pltpu.VMEM((2, MOE_WIN, D_MODEL), F32),
            pltpu.SemaphoreType.DMA((2,)),
            pltpu.VMEM((2, MOE_WIN, D_MODEL), F32),
            pltpu.SemaphoreType.DMA((2,)),
            pltpu.SMEM((n_win * N_EXPERTS,), jnp.int32),
            pltpu.SMEM((n_win * N_EXPERTS,), jnp.int32),
            pltpu.SMEM((n_win * N_EXPERTS,), jnp.int32),
            pltpu.SMEM((N_EXPERTS,), jnp.int32),
            pltpu.SMEM((N_EXPERTS,), jnp.int32),
        ],
    )
    return pl.pallas_call(
        functools.partial(_moe_kernel, n_win=n_win, final=final),
        grid_spec=grid_spec,
        out_shape=jax.ShapeDtypeStruct((m, D_MODEL), F32),
        compiler_params=pltpu.CompilerParams(
            dimension_semantics=("arbitrary", "arbitrary"), vmem_limit_bytes=MOE_VMEM_LIMIT),
        name="moe",
    )(cnt, hn, gates, d, dt, wg, wu, wd, x1, gfin)


def kernel(x_prompt, x_sample, state_ssm_re, state_ssm_im, norm_mix, w_in, ssm_a_re, ssm_a_im, ssm_log_dt, ssm_b_re, ssm_b_im, ssm_c_re, ssm_c_im, ssm_d, glu_w, glu_b, sgu_w, sgu_b, sgu_ln_g, sgu_ln_b, out_norm_a, out_norm_b, w_out, norm_ffn, ffn_w_gate, ffn_w_up, ffn_w_down, router_w, router_b, moe_w_gate, moe_w_up, moe_w_down, norm_final):
    batch, seq = x_prompt.shape[0], x_prompt.shape[1]
    dec_batch, dec_seq = x_sample.shape[0], x_sample.shape[1]
    mp, ms = batch * seq, dec_batch * dec_seq
    n_moe = moe_w_gate.shape[0]

    lam, bbre, bbim = _s5_prep(ssm_a_re, ssm_a_im, ssm_log_dt, ssm_b_re, ssm_b_im)
    ctre = jnp.swapaxes(ssm_c_re, 2, 3).reshape(DEPTH * N_STATE, SSM_GROUP)
    ctim = jnp.swapaxes(ssm_c_im, 2, 3).reshape(DEPTH * N_STATE, SSM_GROUP)
    dskip = ssm_d.reshape(DEPTH, 1, MIX_A)
    glu = glu_w.reshape(DEPTH * MIX_A, 2 * SSM_GROUP)
    gbias = jnp.stack([glu_b[..., :SSM_GROUP].reshape(DEPTH, MIX_A),
                       glu_b[..., SSM_GROUP:].reshape(DEPTH, MIX_A)], axis=1)
    s5w = (lam, bbre, bbim, ctre, ctim, dskip, glu, gbias)
    w_in_b = w_in.astype(BF16)
    w_out_b = w_out.astype(BF16)
    g_mix = norm_mix.reshape(DEPTH, 1, D_MODEL)
    g_a = out_norm_a.reshape(DEPTH, 1, MIX_A)
    g_b = out_norm_b.reshape(DEPTH, 1, MIX_B)
    g_ffn = norm_ffn.reshape(DEPTH, 1, D_MODEL)
    sgu_bias = jnp.repeat(jnp.swapaxes(sgu_b, 1, 2), HD_B, axis=2)
    sgu_bias_s = sgu_bias[:, :dec_seq]
    lng = sgu_ln_g.reshape(DEPTH, 1, MIX_B)
    lnb = sgu_ln_b.reshape(DEPTH, 1, MIX_B)
    coef_s = jnp.repeat(
        jnp.transpose(sgu_w[:, :, :dec_seq, :dec_seq], (0, 2, 3, 1)).reshape(DEPTH, dec_seq * dec_seq, NH_B),
        HD_B, axis=2)
    dense_split = (FF_PAD, D_FF - FF_PAD)
    dense_g = _pack(jnp.swapaxes(ffn_w_gate, 1, 2), dense_split, transpose=True)
    dense_u = _pack(jnp.swapaxes(ffn_w_up, 1, 2), dense_split, transpose=True)
    dense_d = _pack(ffn_w_down, dense_split, transpose=False)
    moe_t = lambda w: jnp.swapaxes(w.reshape(-1, D_MODEL, D_FF_EXPERT), 1, 2)
    moe_g = _pack(moe_t(moe_w_gate), (D_FF_EXPERT,), transpose=True)
    moe_u = _pack(moe_t(moe_w_up), (D_FF_EXPERT,), transpose=True)
    moe_d = _pack(moe_w_down.reshape(-1, D_FF_EXPERT, D_MODEL), (D_FF_EXPERT,), transpose=False)
    rw = jnp.pad(router_w, ((0, 0), (0, 0), (0, LANES - N_EXPERTS)))
    rwh = rw.astype(BF16)
    rw2 = jnp.concatenate([rwh, (rw - rwh.astype(F32)).astype(BF16)], axis=-1)
    rb = jnp.pad(router_b, ((0, 0), (0, LANES - N_EXPERTS))).reshape(n_moe, 1, LANES)
    gfin = norm_final.reshape(1, D_MODEL)

    h0re = state_ssm_re.reshape(DEPTH, dec_batch, N_STATE)
    h0im = state_ssm_im.reshape(DEPTH, dec_batch, N_STATE)

    assert DEPTH % 2 == 0

    def post_mixer(x, ya, yb, l, tm, tb):
        common = (g_a, g_b, w_out_b, g_ffn)
        if l % 2 == 0:
            return _ffn(x, ya, yb, *common, dense_g, dense_u, dense_d, l=l, tm=tm)
        x1, hn, gates, d, dt, cnt = _pre_moe(x, ya, yb, *common, rw2, rb, l=l, tm=min(2 * tm, tb))
        cnt = cnt[:, 0, :N_EXPERTS].astype(jnp.int32).reshape(-1)
        return _moe(cnt, hn, gates, d, dt, moe_g, moe_u, moe_d, x1, gfin, l=l, tb=tb, final=(l == DEPTH - 1))

    xp = x_prompt.reshape(mp, D_MODEL)
    xs = x_sample.reshape(ms, D_MODEL)
    re_p, im_p, re_s, im_s, v_s = [], [], [], [], []
    for l in range(DEPTH):
        ya_p, yb_p, hre_p, him_p = _mixer(xp.reshape(batch, seq, D_MODEL), g_mix, w_in_b, sgu_w, sgu_bias,
                                          lng, lnb, s5w, l)
        zs = _inproj(xs, g_mix, w_in_b, l, tm=ms)
        ya_s, hre_s, him_s = _s5_sample(zs, h0re, h0im, s5w, l, nb=dec_batch, t_blk=dec_seq)
        yb_s, vn_s = _sgu_sample(zs, coef_s, sgu_bias_s, lng, lnb, l, nb=dec_batch, t_blk=dec_seq)
        xp = post_mixer(xp, ya_p.reshape(mp, MIX_A), yb_p.reshape(mp, MIX_B), l, tm=512, tb=seq)
        xs = post_mixer(xs, ya_s, yb_s, l, tm=ms, tb=ms)
        re_p.append(hre_p)
        im_p.append(him_p)
        re_s.append(hre_s)
        im_s.append(him_s)
        v_s.append(vn_s)

    st = lambda hs, b: jnp.stack(hs).reshape(DEPTH, b, NG_A, SSM_STATE)
    return (xp.reshape(batch, seq, D_MODEL), xs.reshape(dec_batch, dec_seq, D_MODEL),
            st(re_p, batch), st(im_p, batch), st(re_s, dec_batch), st(im_s, dec_batch),
            jnp.stack(v_s).reshape(DEPTH, dec_batch, dec_seq, NH_B, HD_B))
```

```python
import functools
import math

import jax
import jax.numpy as jnp
from jax import lax
from jax.experimental import pallas as pl
from jax.experimental.pallas import tpu as pltpu

F32 = jnp.float32
BF16 = jnp.bfloat16

D_MODEL = 1024
DEPTH = 4
MIX_A = 512
SSM_GROUP = 16
NG_A = 32
SSM_STATE = 64
N_STATE = NG_A * SSM_STATE
MIX_B = 512
HD_B = 64
NH_B = 8
CHUNK = 128
IN_WIDTH = MIX_A + 2 * MIX_B
D_FF = 2752
N_EXPERTS = 8
D_FF_EXPERT = D_FF // 2
FF_PAD = 1408
EPS = 1e-6

LANES = 128
SUBLANES = 8
MXU_DIM = 256
STATE_COLS = 512
N_SCOL = N_STATE // STATE_COLS
GROUPS_PER_CHUNK = STATE_COLS // SSM_STATE
VMEM_LIMIT = 52 * 1024 * 1024
MOE_VMEM_LIMIT = 58 * 1024 * 1024


def _rms(x, g):
    return x * lax.rsqrt(jnp.mean(x * x, axis=-1, keepdims=True) + EPS) * g


def _gelu(x):
    c = math.sqrt(2.0 / math.pi)
    return 0.5 * x * (1.0 + jnp.tanh(c * (x + 0.044715 * (x * x * x))))


def _sigmoid(x):
    return 1.0 / (1.0 + jnp.exp(-x))


def _split_bf16(x):
    hi = x.astype(BF16)
    lo = (x - hi.astype(F32)).astype(BF16)
    return hi, lo


def _dot(a, b):
    return jnp.dot(a, b, preferred_element_type=F32)


def _log2(n):
    assert n & (n - 1) == 0
    return n.bit_length() - 1


def _block_diag_tile(src, n_rep, rows_per_group):
    r_n, k = src.shape
    n = k * n_rep
    t_row = lax.broadcasted_iota(jnp.int32, (k, n), 0)
    t_col = lax.broadcasted_iota(jnp.int32, (k, n), 1)
    tile = jnp.where(jnp.bitwise_and(t_col, k - 1) == t_row, 1.0, 0.0).astype(BF16)
    full = _dot(src.astype(BF16), tile)
    row = lax.broadcasted_iota(jnp.int32, (r_n, n), 0)
    col = lax.broadcasted_iota(jnp.int32, (r_n, n), 1)
    same = lax.shift_right_logical(row, _log2(rows_per_group)) == lax.shift_right_logical(col, _log2(k))
    return jnp.where(same, full, 0.0).astype(BF16)


def _pack_kernel(w_ref, o_ref, *, widths, transpose):
    for i, width in enumerate(widths):
        w = w_ref[i * FF_PAD:i * FF_PAD + width, :]
        if width < FF_PAD:
            w = jnp.concatenate([w, jnp.zeros((FF_PAD - width, w.shape[1]), F32)], axis=0)
        o_ref[i] = (w.T if transpose else w).astype(BF16)


def _pack(w, widths, transpose):
    n_e, n_parts = w.shape[0], len(widths)
    out_blk = (n_parts, D_MODEL, FF_PAD) if transpose else (n_parts, FF_PAD, D_MODEL)
    out = pl.pallas_call(
        functools.partial(_pack_kernel, widths=widths, transpose=transpose),
        grid=(n_e,),
        in_specs=[pl.BlockSpec((None,) + w.shape[1:], lambda e: (e, 0, 0))],
        out_specs=pl.BlockSpec((None,) + out_blk, lambda e: (e, 0, 0, 0)),
        out_shape=jax.ShapeDtypeStruct((n_e,) + out_blk, BF16),
        compiler_params=pltpu.CompilerParams(
            dimension_semantics=("parallel",), vmem_limit_bytes=VMEM_LIMIT),
        name="pack_t" if transpose else "pack",
    )(w)
    return out.reshape((n_e * n_parts,) + out_blk[1:])


def _s5_prep_kernel(are_ref, aim_ref, ldt_ref, bre_ref, bim_ref,
                    abre_ref, abim_ref, bbre_ref, bbim_ref):
    lam_re = are_ref[...]
    lam_im = aim_ref[...]
    dt = jnp.exp(ldt_ref[...])
    mag = jnp.exp(lam_re * dt)
    ab_re = mag * jnp.cos(lam_im * dt)
    ab_im = mag * jnp.sin(lam_im * dt)
    den = lam_re * lam_re + lam_im * lam_im
    nr = ab_re - 1.0
    q_re = (nr * lam_re + ab_im * lam_im) / den
    q_im = (ab_im * lam_re - nr * lam_im) / den
    b_re = bre_ref[...]
    b_im = bim_ref[...]
    abre_ref[...] = ab_re
    abim_ref[...] = ab_im
    bbre_ref[...] = q_re * b_re - q_im * b_im
    bbim_ref[...] = q_re * b_im + q_im * b_re


def _s5_prep(ssm_a_re, ssm_a_im, ssm_log_dt, ssm_b_re, ssm_b_im):
    shp = (DEPTH, NG_A, SSM_GROUP, SSM_STATE)
    rows = DEPTH * NG_A * SSM_GROUP
    bc = lambda a: jnp.broadcast_to(a, shp).reshape(rows, SSM_STATE)
    are = bc(ssm_a_re[:, :, None, :])
    aim = bc(ssm_a_im[:, :, None, :])
    ldt = bc(ssm_log_dt[:, :, None, None])
    bre = jnp.swapaxes(ssm_b_re, 2, 3).reshape(rows, SSM_STATE)
    bim = jnp.swapaxes(ssm_b_im, 2, 3).reshape(rows, SSM_STATE)
    sds = jax.ShapeDtypeStruct((rows, SSM_STATE), F32)
    abre, abim, bbre, bbim = pl.pallas_call(
        _s5_prep_kernel, out_shape=(sds, sds, sds, sds), name="s5_prep",
    )(are, aim, ldt, bre, bim)
    lam_re = abre.reshape(shp)[:, :, 0, :].reshape(DEPTH, 1, N_STATE)
    lam_im = abim.reshape(shp)[:, :, 0, :].reshape(DEPTH, 1, N_STATE)
    lam = jnp.concatenate([lam_re, lam_im], axis=1)
    return lam, bbre, bbim


def _inproj_kernel(x_ref, g_ref, w_ref, z_ref):
    hn = _rms(x_ref[...], g_ref[...])
    z_ref[...] = _dot(hn.astype(BF16), w_ref[...])


def _inproj(x, g, w, l, tm):
    m = x.shape[0]
    return pl.pallas_call(
        _inproj_kernel,
        grid=(m // tm,),
        in_specs=[
            pl.BlockSpec((tm, D_MODEL), lambda i: (i, 0)),
            pl.BlockSpec((None, 1, D_MODEL), lambda i: (l, 0, 0)),
            pl.BlockSpec((None, D_MODEL, IN_WIDTH), lambda i: (l, 0, 0)),
        ],
        out_specs=pl.BlockSpec((tm, IN_WIDTH), lambda i: (i, 0)),
        out_shape=jax.ShapeDtypeStruct((m, IN_WIDTH), F32),
        compiler_params=pltpu.CompilerParams(
            dimension_semantics=("parallel",), vmem_limit_bytes=VMEM_LIMIT),
        name="inproj",
    )(x, g, w)


def _s5_expand(bbre_ref, bbim_ref, ctre_ref, ctim_ref, glu_ref, bblk, cre, cimn, wa, wg):
    for c in range(4):
        rs = slice(c * LANES, (c + 1) * LANES)
        bblk[c, :, :STATE_COLS] = _block_diag_tile(bbre_ref[rs, :], GROUPS_PER_CHUNK, SSM_GROUP)
        bblk[c, :, STATE_COLS:] = _block_diag_tile(bbim_ref[rs, :], GROUPS_PER_CHUNK, SSM_GROUP)
        ss = slice(c * STATE_COLS, (c + 1) * STATE_COLS)
        cre[c] = _block_diag_tile(ctre_ref[ss, :], GROUPS_PER_CHUNK, SSM_STATE)
        cimn[c] = _block_diag_tile(-ctim_ref[ss, :], GROUPS_PER_CHUNK, SSM_STATE)
    for b in range(MIX_A // MXU_DIM):
        ms = slice(b * MXU_DIM, (b + 1) * MXU_DIM)
        wa[b] = _block_diag_tile(glu_ref[ms, :SSM_GROUP], MXU_DIM // SSM_GROUP, SSM_GROUP)
        wg[b] = _block_diag_tile(glu_ref[ms, SSM_GROUP:], MXU_DIM // SSM_GROUP, SSM_GROUP)


def _s5_time_major(utm, lam_ref, d_ref, gbias_ref, hre_ref, him_ref, bre, bim, bblk, cre, cimn, wa, wg,
                   nb, t_blk):
    for c in range(4):
        r = _dot(utm[c].astype(BF16), bblk[c])
        bre[:, c * STATE_COLS:(c + 1) * STATE_COLS] = r[:, :STATE_COLS]
        bim[:, c * STATE_COLS:(c + 1) * STATE_COLS] = r[:, STATE_COLS:]

    for c in range(N_SCOL):
        cs = slice(c * STATE_COLS, (c + 1) * STATE_COLS)
        lr = jnp.broadcast_to(lam_ref[0:1, cs], (SUBLANES, STATE_COLS))
        li = jnp.broadcast_to(lam_ref[1:2, cs], (SUBLANES, STATE_COLS))

        if nb == SUBLANES:
            hr, hi = hre_ref[:, cs], him_ref[:, cs]
            for t in range(t_blk):
                rs = slice(t * SUBLANES, (t + 1) * SUBLANES)
                hr, hi = lr * hr - li * hi + bre[rs, cs], lr * hi + li * hr + bim[rs, cs]
                bre[rs, cs] = hr
                bim[rs, cs] = hi
            hre_ref[:, cs] = hr
            him_ref[:, cs] = hi
            continue

        def group_body(bg, carry, cs=cs, lr=lr, li=li):
            r0 = pl.multiple_of(bg * SUBLANES, SUBLANES)

            def t_body(t, h):
                hr, hi = h
                row = pl.multiple_of(t * nb + r0, SUBLANES)
                nr = lr * hr - li * hi + bre[pl.ds(row, SUBLANES), cs]
                ni = lr * hi + li * hr + bim[pl.ds(row, SUBLANES), cs]
                bre[pl.ds(row, SUBLANES), cs] = nr
                bim[pl.ds(row, SUBLANES), cs] = ni
                return nr, ni

            h0 = (hre_ref[pl.ds(r0, SUBLANES), cs], him_ref[pl.ds(r0, SUBLANES), cs])
            hr, hi = lax.fori_loop(0, t_blk, t_body, h0, unroll=min(t_blk, 8))
            hre_ref[pl.ds(r0, SUBLANES), cs] = hr
            him_ref[pl.ds(r0, SUBLANES), cs] = hi
            return carry

        lax.fori_loop(0, nb // SUBLANES, group_body, 0)

    ys = []
    for c in range(4):
        cs = slice(c * STATE_COLS, (c + 1) * STATE_COLS)
        ls = slice(c * LANES, (c + 1) * LANES)
        y = _dot(bre[:, cs].astype(BF16), cre[c]) + _dot(bim[:, cs].astype(BF16), cimn[c])
        ys.append(_gelu(y + d_ref[:, ls] * utm[c]).astype(BF16))

    os_ = []
    for b in range(MIX_A // MXU_DIM):
        ms = slice(b * MXU_DIM, (b + 1) * MXU_DIM)
        yb = jnp.concatenate(ys[2 * b:2 * b + 2], axis=1)
        za = _dot(yb, wa[b]) + gbias_ref[0:1, ms]
        zg = _dot(yb, wg[b]) + gbias_ref[1:2, ms]
        os_.append(za * _sigmoid(zg))
    return jnp.concatenate(os_, axis=1)


def _s5_kernel(*refs, nb, t_blk):
    (u_ref, h0re_ref, h0im_ref, bbre_ref, bbim_ref, lam_ref, ctre_ref, ctim_ref, d_ref, glu_ref, gbias_ref,
     ya_ref, hre_ref, him_ref, utm, bre, bim, otm, bblk, cre, cimn, wa, wg, tmp) = refs
    hre_ref[...] = h0re_ref[...]
    him_ref[...] = h0im_ref[...]
    _s5_expand(bbre_ref, bbim_ref, ctre_ref, ctim_ref, glu_ref, bblk, cre, cimn, wa, wg)

    for c in range(4):
        ls = slice(c * LANES, (c + 1) * LANES)
        tmp[c] = u_ref[:, ls]
        for t in range(t_blk):
            utm[c, t * nb:(t + 1) * nb, :] = tmp[c, pl.ds(t, nb, stride=t_blk), :]

    o = _s5_time_major(utm, lam_ref, d_ref, gbias_ref, hre_ref, him_ref, bre, bim, bblk, cre, cimn, wa, wg,
                       nb, t_blk)

    for c in range(4):
        ls = slice(c * LANES, (c + 1) * LANES)
        for t in range(t_blk):
            otm[c, pl.ds(t, nb, stride=t_blk), :] = o[t * nb:(t + 1) * nb, ls]
        ya_ref[:, ls] = otm[c]


def _s5_weight_specs(l):
    return [
        pl.BlockSpec((NG_A * SSM_GROUP, SSM_STATE), lambda j: (l, 0)),
        pl.BlockSpec((NG_A * SSM_GROUP, SSM_STATE), lambda j: (l, 0)),
        pl.BlockSpec((None, 2, N_STATE), lambda j: (l, 0, 0)),
        pl.BlockSpec((N_STATE, SSM_GROUP), lambda j: (l, 0)),
        pl.BlockSpec((N_STATE, SSM_GROUP), lambda j: (l, 0)),
        pl.BlockSpec((None, 1, MIX_A), lambda j: (l, 0, 0)),
        pl.BlockSpec((MIX_A, 2 * SSM_GROUP), lambda j: (l, 0)),
        pl.BlockSpec((None, 2, MIX_A), lambda j: (l, 0, 0)),
    ]


def _s5_scratch(rows):
    return [
        pltpu.VMEM((4, rows, LANES), F32),
        pltpu.VMEM((rows, N_STATE), F32),
        pltpu.VMEM((rows, N_STATE), F32),
        pltpu.VMEM((4, rows, LANES), F32),
        pltpu.VMEM((4, LANES, 2 * STATE_COLS), BF16),
        pltpu.VMEM((4, STATE_COLS, LANES), BF16),
        pltpu.VMEM((4, STATE_COLS, LANES), BF16),
        pltpu.VMEM((MIX_A // MXU_DIM, MXU_DIM, MXU_DIM), BF16),
        pltpu.VMEM((MIX_A // MXU_DIM, MXU_DIM, MXU_DIM), BF16),
    ]


def _s5_sample(z, h0re, h0im, weights, l, nb, t_blk):
    rows = nb * t_blk
    lam, bbre, bbim, ctre, ctim, dskip, glu, gbias = weights
    blk = pl.BlockSpec((rows, MIX_A), lambda j: (0, 0))
    st_in = pl.BlockSpec((None, nb, N_STATE), lambda j: (l, 0, 0))
    st_out = pl.BlockSpec((nb, N_STATE), lambda j: (0, 0))
    return pl.pallas_call(
        functools.partial(_s5_kernel, nb=nb, t_blk=t_blk),
        grid=(1,),
        in_specs=[blk, st_in, st_in] + _s5_weight_specs(l),
        out_specs=[blk, st_out, st_out],
        out_shape=[
            jax.ShapeDtypeStruct((rows, MIX_A), F32),
            jax.ShapeDtypeStruct((nb, N_STATE), F32),
            jax.ShapeDtypeStruct((nb, N_STATE), F32),
        ],
        scratch_shapes=_s5_scratch(rows) + [pltpu.VMEM((4, rows, LANES), F32)],
        compiler_params=pltpu.CompilerParams(
            dimension_semantics=("arbitrary",), vmem_limit_bytes=VMEM_LIMIT),
        name="s5_sample",
    )(z, h0re, h0im, bbre, bbim, lam, ctre, ctim, dskip, glu, gbias)


def _group_layernorm(gv, lng_ref, lnb_ref):
    row = lax.broadcasted_iota(jnp.int32, (MXU_DIM, MXU_DIM), 0)
    col = lax.broadcasted_iota(jnp.int32, (MXU_DIM, MXU_DIM), 1)
    same = lax.shift_right_logical(row, _log2(HD_B)) == lax.shift_right_logical(col, _log2(HD_B))
    gm = jnp.where(same, 1.0 / HD_B, 0.0).astype(BF16)

    def gmean(a):
        hi, lo = _split_bf16(a)
        parts = []
        for b in range(MIX_B // MXU_DIM):
            ms = slice(b * MXU_DIM, (b + 1) * MXU_DIM)
            parts.append(_dot(hi[:, ms], gm) + _dot(lo[:, ms], gm))
        return jnp.concatenate(parts, axis=1)

    xc = gv - gmean(gv)
    var = gmean(xc * xc)
    return xc * lax.rsqrt(var + EPS) * lng_ref[...] + lnb_ref[...]


INPROJ_ROWS = 256


def _mixer_kernel(x_ref, g_ref, w_ref, ws_ref, bias_ref, lng_ref, lnb_ref,
                  bbre_ref, bbim_ref, lam_ref, ctre_ref, ctim_ref, d_ref, glu_ref, gbias_ref,
                  ya_ref, yb_ref, hre_ref, him_ref,
                  hnb, ua, zb, utm, bre, bim, otm, bblk, cre, cimn, wa, wg, *, nb):
    @pl.when(pl.program_id(0) == 0)
    def _():
        hre_ref[...] = jnp.zeros_like(hre_ref)
        him_ref[...] = jnp.zeros_like(him_ref)
        _s5_expand(bbre_ref, bbim_ref, ctre_ref, ctim_ref, glu_ref, bblk, cre, cimn, wa, wg)

    for b in range(nb):
        hnb[b * CHUNK:(b + 1) * CHUNK, :] = _rms(x_ref[b], g_ref[...]).astype(BF16)
    for m0 in range(0, nb * CHUNK, INPROJ_ROWS):
        ua[m0:m0 + INPROJ_ROWS, :] = _dot(hnb[m0:m0 + INPROJ_ROWS, :], w_ref[:, :MIX_A])
    for c in range(4):
        ls = slice(c * LANES, (c + 1) * LANES)
        for b in range(nb):
            utm[c, pl.ds(b, CHUNK, stride=nb), :] = ua[b * CHUNK:(b + 1) * CHUNK, ls]
    for m0 in range(0, nb * CHUNK, INPROJ_ROWS):
        zb[m0:m0 + INPROJ_ROWS, :] = _dot(hnb[m0:m0 + INPROJ_ROWS, :], w_ref[:, MIX_A:])

    row = lax.broadcasted_iota(jnp.int32, (CHUNK, 2 * CHUNK), 0)
    col = lax.broadcasted_iota(jnp.int32, (CHUNK, 2 * CHUNK), 1)
    causal = jnp.bitwise_and(col, CHUNK - 1) <= row
    lane = lax.broadcasted_iota(jnp.int32, (CHUNK, LANES), 1)
    first_head = lane < HD_B
    wcat = []
    for p in range(NH_B // 2):
        w = jnp.concatenate([ws_ref[2 * p], ws_ref[2 * p + 1]], axis=1)
        wcat.append(jnp.where(causal, w, 0.0).astype(BF16))
    for b in range(nb):
        rs = slice(b * CHUNK, (b + 1) * CHUNK)
        u = _gelu(zb[rs, :MIX_B])
        vn = _group_layernorm(_gelu(zb[rs, MIX_B:]), lng_ref, lnb_ref)
        for p in range(NH_B // 2):
            ls = slice(p * LANES, (p + 1) * LANES)
            vp = vn[:, ls]
            rhs = jnp.concatenate(
                [jnp.where(first_head, vp, 0.0), jnp.where(first_head, 0.0, vp)], axis=0).astype(BF16)
            s = _dot(wcat[p], rhs)
            yb_ref[b, :, ls] = u[:, ls] * (s + bias_ref[:, ls])

    o = _s5_time_major(utm, lam_ref, d_ref, gbias_ref, hre_ref, him_ref, bre, bim, bblk, cre, cimn, wa, wg,
                       nb, CHUNK)
    for c in range(4):
        ls = slice(c * LANES, (c + 1) * LANES)
        otm[c] = o[:, ls]
        for b in range(nb):
            ya_ref[b, :, ls] = otm[c, pl.ds(b, CHUNK, stride=nb), :]


def _mixer(x3, g, w, ws, bias, lng, lnb, s5w, l):
    nb, seq = x3.shape[0], x3.shape[1]
    rows = nb * CHUNK
    lam, bbre, bbim, ctre, ctim, dskip, glu, gbias = s5w
    lay3 = lambda j: (l, 0, 0)
    blk = lambda width: pl.BlockSpec((nb, CHUNK, width), lambda j: (0, j, 0))
    st = pl.BlockSpec((nb, N_STATE), lambda j: (0, 0))
    return pl.pallas_call(
        functools.partial(_mixer_kernel, nb=nb),
        grid=(seq // CHUNK,),
        in_specs=[
            blk(D_MODEL),
            pl.BlockSpec((None, 1, D_MODEL), lay3),
            pl.BlockSpec((None, D_MODEL, IN_WIDTH), lay3, pipeline_mode=pl.Buffered(1)),
            pl.BlockSpec((None, NH_B, CHUNK, CHUNK), lambda j: (l, 0, 0, 0)),
            pl.BlockSpec((None, CHUNK, MIX_B), lay3),
            pl.BlockSpec((None, 1, MIX_B), lay3),
            pl.BlockSpec((None, 1, MIX_B), lay3),
        ] + _s5_weight_specs(l),
        out_specs=[blk(MIX_A), blk(MIX_B), st, st],
        out_shape=[
            jax.ShapeDtypeStruct((nb, seq, MIX_A), F32),
            jax.ShapeDtypeStruct((nb, seq, MIX_B), F32),
            jax.ShapeDtypeStruct((nb, N_STATE), F32),
            jax.ShapeDtypeStruct((nb, N_STATE), F32),
        ],
        scratch_shapes=[
            pltpu.VMEM((rows, D_MODEL), BF16),
            pltpu.VMEM((rows, MIX_A), F32),
            pltpu.VMEM((rows, 2 * MIX_B), F32),
        ] + _s5_scratch(rows),
        compiler_params=pltpu.CompilerParams(
            dimension_semantics=("arbitrary",), vmem_limit_bytes=MOE_VMEM_LIMIT),
        name="mixer",
    )(x3, g, w, ws, bias, lng, lnb, bbre, bbim, lam, ctre, ctim, dskip, glu, gbias)


def _sgu_sample_kernel(u_ref, v_ref, coef_ref, bias_ref, lng_ref, lnb_ref, o_ref, vn_ref,
                       usc, vsc, osc, *, nb, t_blk):
    vn = _group_layernorm(_gelu(v_ref[...]), lng_ref, lnb_ref)
    vn_ref[...] = vn
    u = _gelu(u_ref[...])
    for c in range(4):
        ls = slice(c * LANES, (c + 1) * LANES)
        vsc[c] = vn[:, ls]
        usc[c] = u[:, ls]
        for q in range(t_blk):
            s = bias_ref[q:q + 1, ls]
            for k in range(q + 1):
                s = s + coef_ref[q * t_blk + k:q * t_blk + k + 1, ls] * vsc[c, pl.ds(k, nb, stride=t_blk), :]
            osc[c, pl.ds(q, nb, stride=t_blk), :] = usc[c, pl.ds(q, nb, stride=t_blk), :] * s
        o_ref[:, ls] = osc[c]


def _sgu_sample(z, coef, bias, lng, lnb, l, nb, t_blk):
    m = nb * t_blk
    fix2 = lambda i: (0, 0)
    lay3 = lambda i: (l, 0, 0)
    return pl.pallas_call(
        functools.partial(_sgu_sample_kernel, nb=nb, t_blk=t_blk),
        grid=(1,),
        in_specs=[
            pl.BlockSpec((m, MIX_B), lambda i: (0, 1)),
            pl.BlockSpec((m, MIX_B), lambda i: (0, 2)),
            pl.BlockSpec((None, t_blk * t_blk, MIX_B), lay3),
            pl.BlockSpec((None, t_blk, MIX_B), lay3),
            pl.BlockSpec((None, 1, MIX_B), lay3),
            pl.BlockSpec((None, 1, MIX_B), lay3),
        ],
        out_specs=[pl.BlockSpec((m, MIX_B), fix2), pl.BlockSpec((m, MIX_B), fix2)],
        out_shape=[jax.ShapeDtypeStruct((m, MIX_B), F32), jax.ShapeDtypeStruct((m, MIX_B), F32)],
        scratch_shapes=[pltpu.VMEM((4, m, LANES), F32)] * 3,
        compiler_params=pltpu.CompilerParams(
            dimension_semantics=("arbitrary",), vmem_limit_bytes=VMEM_LIMIT),
        name="sgu_sample",
    )(z, z, coef, bias, lng, lnb)


def _swiglu(x, wg_ref, wu_ref, wd_ref):
    hg = _dot(x, wg_ref[...])
    a = (hg * _sigmoid(hg) * _dot(x, wu_ref[...])).astype(BF16)
    return _dot(a, wd_ref[...])


def _out_proj(x_ref, ya_ref, yb_ref, ga_ref, gb_ref, wout_ref):
    na = _rms(ya_ref[...], ga_ref[...]).astype(BF16)
    nb = _rms(yb_ref[...], gb_ref[...]).astype(BF16)
    return x_ref[...] + _dot(na, wout_ref[0:MIX_A, :]) + _dot(nb, wout_ref[MIX_A:, :])


N_DENSE_PARTS = 2
FFN_SUB = 256


def _ffn_kernel(x_ref, ya_ref, yb_ref, ga_ref, gb_ref, wout_ref, gf_ref, wg_ref, wu_ref, wd_ref, o_ref):
    for s in range(x_ref.shape[0] // FFN_SUB):
        rs = pl.ds(s * FFN_SUB, FFN_SUB)
        x1 = _out_proj(x_ref.at[rs], ya_ref.at[rs], yb_ref.at[rs], ga_ref, gb_ref, wout_ref)
        hn = _rms(x1, gf_ref[...]).astype(BF16)
        o = o_ref.at[rs]
        o[...] = x1
        for p in range(N_DENSE_PARTS):
            o[...] += _swiglu(hn, wg_ref.at[p], wu_ref.at[p], wd_ref.at[p])


def _ffn(x, ya, yb, ga, gb, wout, gf, wg, wu, wd, *, l, tm):
    m = x.shape[0]
    j = l // 2
    one = pl.Buffered(1)
    lay3 = lambda i: (l, 0, 0)
    wsel = lambda i: (j, 0, 0)
    return pl.pallas_call(
        _ffn_kernel,
        grid=(m // tm,),
        in_specs=[
            pl.BlockSpec((tm, D_MODEL), lambda i: (i, 0)),
            pl.BlockSpec((tm, MIX_A), lambda i: (i, 0)),
            pl.BlockSpec((tm, MIX_B), lambda i: (i, 0)),
            pl.BlockSpec((None, 1, MIX_A), lay3),
            pl.BlockSpec((None, 1, MIX_B), lay3),
            pl.BlockSpec((None, D_MODEL, D_MODEL), lay3, pipeline_mode=one),
            pl.BlockSpec((None, 1, D_MODEL), lay3),
            pl.BlockSpec((N_DENSE_PARTS, D_MODEL, FF_PAD), wsel, pipeline_mode=one),
            pl.BlockSpec((N_DENSE_PARTS, D_MODEL, FF_PAD), wsel, pipeline_mode=one),
            pl.BlockSpec((N_DENSE_PARTS, FF_PAD, D_MODEL), wsel, pipeline_mode=one),
        ],
        out_specs=pl.BlockSpec((tm, D_MODEL), lambda i: (i, 0)),
        out_shape=jax.ShapeDtypeStruct((m, D_MODEL), F32),
        compiler_params=pltpu.CompilerParams(
            dimension_semantics=("parallel",), vmem_limit_bytes=VMEM_LIMIT),
        name="ffn_dense",
    )(x, ya, yb, ga, gb, wout, gf, wg, wu, wd)


MOE_WIN = 256
SEG = 16
SORT_ROWS = 2 * MOE_WIN + N_EXPERTS * SEG
ROW_TILE = 256
TAIL_TILES = (ROW_TILE // 2, ROW_TILE)
NOT_ROUTED = -1.0e6


def _top2(logits):
    lane = lax.broadcasted_iota(jnp.int32, logits.shape, 1).astype(F32)
    neg = jnp.float32(-jnp.inf)
    lg = jnp.where(lane < N_EXPERTS, logits, neg)
    m1 = jnp.max(lg, axis=1, keepdims=True)
    i1 = jnp.min(jnp.where(lg == m1, lane, float(LANES)), axis=1, keepdims=True)
    lg2 = jnp.where(lane == i1, neg, lg)
    m2 = jnp.max(lg2, axis=1, keepdims=True)
    i2 = jnp.min(jnp.where(lg2 == m2, lane, float(LANES)), axis=1, keepdims=True)
    ex = jnp.exp(m2 - m1)
    w1 = 1.0 / (1.0 + ex)
    w2 = ex / (1.0 + ex)
    gates = jnp.where(lane == i1, w1, 0.0) + jnp.where(lane == i2, w2, 0.0)
    mask = jnp.where((lane == i1) | (lane == i2), 1.0, 0.0)
    return gates, mask


def _pre_moe_kernel(x_ref, ya_ref, yb_ref, ga_ref, gb_ref, wout_ref, gf_ref, rw_ref, rb_ref,
                    x1_ref, hn_ref, gates_ref, d_ref, dt_ref, cnt_ref, *, n_win):
    x1 = _out_proj(x_ref, ya_ref, yb_ref, ga_ref, gb_ref, wout_ref)
    x1_ref[...] = x1
    hn = _rms(x1, gf_ref[...])
    hn_ref[...] = hn.astype(BF16)
    hi, lo = _split_bf16(hn)
    hw = _dot(hi, rw_ref[...])
    logits = hw[:, :LANES] + hw[:, LANES:] + _dot(lo, rw_ref[:, :LANES])
    gates, mask = _top2(logits + rb_ref[...])
    gates_ref[...] = gates
    row = lax.broadcasted_iota(jnp.int32, (MOE_WIN, MOE_WIN), 0)
    col = lax.broadcasted_iota(jnp.int32, (MOE_WIN, MOE_WIN), 1)
    earlier = jnp.where(col < row, 1.0, 0.0).astype(BF16)
    erow = lax.broadcasted_iota(jnp.int32, (LANES, LANES), 0)
    ecol = lax.broadcasted_iota(jnp.int32, (LANES, LANES), 1)
    lower_expert = jnp.where(erow < ecol, 1.0, 0.0).astype(BF16)
    lane = lax.broadcasted_iota(jnp.int32, (MOE_WIN, LANES), 1)
    for w in range(n_win):
        rs = slice(w * MOE_WIN, (w + 1) * MOE_WIN)
        mw = mask[rs, :]
        rank = _dot(earlier, mw.astype(BF16))
        cnt = jnp.broadcast_to(jnp.sum(mw, axis=0, keepdims=True), (SUBLANES, LANES))
        cnt_ref[w] = cnt
        padded = jnp.floor((cnt + (SEG - 1)) * (1.0 / SEG)) * SEG
        seg_start = _dot(padded.astype(BF16), lower_expert)[0:1, :]
        dest = rank + seg_start
        d_lo = jnp.min(jnp.where(mw > 0.0, dest, -NOT_ROUTED), axis=1, keepdims=True)
        d_hi = jnp.max(jnp.where(mw > 0.0, dest, NOT_ROUTED), axis=1, keepdims=True)
        d = jnp.where(lane == 0, d_lo, jnp.where(lane == 1, d_hi, 0.0))
        d_ref[rs, :] = d
        dt_ref[w] = d.T[:SUBLANES, :]


def _pre_moe(x, ya, yb, ga, gb, wout, gf, rw, rb, *, l, tm):
    m = x.shape[0]
    n_win = tm // MOE_WIN
    j = l // 2
    lay3 = lambda i: (l, 0, 0)
    moe3 = lambda i: (j, 0, 0)
    return pl.pallas_call(
        functools.partial(_pre_moe_kernel, n_win=n_win),
        grid=(m // tm,),
        in_specs=[
            pl.BlockSpec((tm, D_MODEL), lambda i: (i, 0)),
            pl.BlockSpec((tm, MIX_A), lambda i: (i, 0)),
            pl.BlockSpec((tm, MIX_B), lambda i: (i, 0)),
            pl.BlockSpec((None, 1, MIX_A), lay3),
            pl.BlockSpec((None, 1, MIX_B), lay3),
            pl.BlockSpec((None, D_MODEL, D_MODEL), lay3),
            pl.BlockSpec((None, 1, D_MODEL), lay3),
            pl.BlockSpec((None, D_MODEL, 2 * LANES), moe3),
            pl.BlockSpec((None, 1, LANES), moe3),
        ],
        out_specs=[
            pl.BlockSpec((tm, D_MODEL), lambda i: (i, 0)),
            pl.BlockSpec((tm, D_MODEL), lambda i: (i, 0)),
            pl.BlockSpec((tm, LANES), lambda i: (i, 0)),
            pl.BlockSpec((tm, LANES), lambda i: (i, 0)),
            pl.BlockSpec((n_win, SUBLANES, MOE_WIN), lambda i: (i, 0, 0)),
            pl.BlockSpec((n_win, SUBLANES, LANES), lambda i: (i, 0, 0)),
        ],
        out_shape=[
            jax.ShapeDtypeStruct((m, D_MODEL), F32),
            jax.ShapeDtypeStruct((m, D_MODEL), BF16),
            jax.ShapeDtypeStruct((m, LANES), F32),
            jax.ShapeDtypeStruct((m, LANES), F32),
            jax.ShapeDtypeStruct((m // MOE_WIN, SUBLANES, MOE_WIN), F32),
            jax.ShapeDtypeStruct((m // MOE_WIN, SUBLANES, LANES), F32),
        ],
        compiler_params=pltpu.CompilerParams(
            dimension_semantics=("parallel",), vmem_limit_bytes=VMEM_LIMIT),
        name="pre_moe",
    )(x, ya, yb, ga, gb, wout, gf, rw, rb)


def _moe_kernel(cnt_sm, hn_ref, gates_ref, d_ref, dt_ref, wg_ref, wu_ref, wd_ref, x1_hbm, gfin_ref, o_hbm,
                xs, gs, sb, gsb, x1buf, x1sem, obuf, osem, pn_sm, s_sm, off_sm, est_sm, tot_sm,
                *, n_win, final):
    blk = pl.program_id(0)
    e = pl.program_id(1)

    def window_rows(w):
        return pl.ds(pl.multiple_of((blk * n_win + w) * MOE_WIN, MOE_WIN), MOE_WIN)

    def x1_copy(w, slot):
        return pltpu.make_async_copy(x1_hbm.at[window_rows(w), :], x1buf.at[slot], x1sem.at[slot])

    def out_copy(w, slot):
        return pltpu.make_async_copy(obuf.at[slot], o_hbm.at[window_rows(w), :], osem.at[slot])

    def seg_copy(w, ee, to_sorted):
        s0 = s_sm[w * N_EXPERTS + ee]
        o0 = off_sm[w * N_EXPERTS + ee]

        def body(i, carry):
            src = pl.multiple_of(s0 + i * SEG, SEG)
            dst = pl.multiple_of(o0 + i * SEG, SEG)
            if to_sorted:
                xs[pl.ds(dst, SEG), :] = sb[pl.ds(src, SEG), :]
                gs[pl.ds(dst, SEG), :] = gsb[pl.ds(src, SEG), :]
            else:
                sb[pl.ds(src, SEG), :] = xs[pl.ds(dst, SEG), :]
            return carry

        lax.fori_loop(0, pn_sm[w * N_EXPERTS + ee] // SEG, body, 0)

    @pl.when(e == 0)
    def _dispatch():
        pn = [[None] * N_EXPERTS for _ in range(n_win)]
        for w in range(n_win):
            run = jnp.int32(0)
            for ee in range(N_EXPERTS):
                n = cnt_sm[(blk * n_win + w) * N_EXPERTS + ee]
                pn[w][ee] = jnp.bitwise_and(n + (SEG - 1), -SEG)
                pn_sm[w * N_EXPERTS + ee] = pn[w][ee]
                s_sm[w * N_EXPERTS + ee] = run
                run = run + pn[w][ee]
        run = jnp.int32(0)
        for ee in range(N_EXPERTS):
            est_sm[ee] = run
            start = run
            for w in range(n_win):
                off_sm[w * N_EXPERTS + ee] = run
                run = run + pn[w][ee]
            tot_sm[ee] = run - start
        end = pl.multiple_of(run, SEG)
        xs[pl.ds(end, ROW_TILE), :] = jnp.zeros((ROW_TILE, D_MODEL), BF16)
        gs[pl.ds(end, ROW_TILE), :] = jnp.zeros((ROW_TILE, LANES), F32)
        riota = lax.broadcasted_iota(jnp.int32, (SORT_ROWS, MOE_WIN), 0).astype(F32)
        for w in range(n_win):
            rs = slice(w * MOE_WIN, (w + 1) * MOE_WIN)
            g = jnp.where(riota == dt_ref[w, 0:1, :], 1.0, jnp.where(riota == dt_ref[w, 1:2, :], 1.0, 0.0))
            gb = g.astype(BF16)
            sb[...] = _dot(gb, hn_ref[rs, :]).astype(BF16)
            gh, gl = _split_bf16(gates_ref[rs, :])
            gsb[...] = _dot(gb, gh) + _dot(gb, gl)
            for ee in range(N_EXPERTS):
                seg_copy(w, ee, True)

    start = est_sm[e]
    tot = tot_sm[e]

    def row_tile(r0, size, valid):
        r0 = pl.multiple_of(r0, SEG)
        xt = xs[pl.ds(r0, size), :]
        y = _swiglu(xt, wg_ref, wu_ref, wd_ref)
        lane = lax.broadcasted_iota(jnp.int32, (size, LANES), 1)
        gate = jnp.sum(jnp.where(lane == e, gs[pl.ds(r0, size), :], 0.0), axis=1, keepdims=True)
        keep = lax.broadcasted_iota(jnp.int32, (size, D_MODEL), 0) < valid
        xs[pl.ds(r0, size), :] = jnp.where(keep, (y * gate).astype(BF16), xt)

    n_full = tot // ROW_TILE

    def full_body(i, carry):
        row_tile(start + i * ROW_TILE, ROW_TILE, ROW_TILE)
        return carry

    lax.fori_loop(0, n_full, full_body, 0)
    rem = tot - n_full * ROW_TILE
    tail = start + n_full * ROW_TILE

    lo = 0
    for size in TAIL_TILES:
        @pl.when((rem > lo) & (rem <= size))
        def _(size=size):
            row_tile(tail, size, rem)
        lo = size

    @pl.when(e == N_EXPERTS - 1)
    def _combine():
        liota = lax.broadcasted_iota(jnp.int32, (MOE_WIN, SORT_ROWS), 1).astype(F32)
        x1_copy(0, 0).start()
        for w in range(n_win):
            rs = slice(w * MOE_WIN, (w + 1) * MOE_WIN)
            slot = w % 2
            if w + 1 < n_win:
                x1_copy(w + 1, 1 - slot).start()
            for ee in range(N_EXPERTS):
                seg_copy(w, ee, False)
            g = jnp.where(liota == d_ref[rs, 0:1], 1.0, jnp.where(liota == d_ref[rs, 1:2], 1.0, 0.0))
            f = _dot(g.astype(BF16), sb[...])
            x1_copy(w, slot).wait()
            x2 = x1buf[slot] + f
            if w >= 2:
                out_copy(w - 2, slot).wait()
            obuf[slot] = _rms(x2, gfin_ref[...]) if final else x2
            out_copy(w, slot).start()
        for w in range(max(n_win - 2, 0), n_win):
            out_copy(w, w % 2).wait()


def _moe(cnt, hn, gates, d, dt, wg, wu, wd, x1, gfin, *, l, tb, final):
    m = hn.shape[0]
    n_win = tb // MOE_WIN
    j = l // 2
    xs_rows = 2 * tb + n_win * N_EXPERTS * SEG + ROW_TILE
    wsel = lambda i, e, c: (j * N_EXPERTS + e, 0, 0)
    grid_spec = pltpu.PrefetchScalarGridSpec(
        num_scalar_prefetch=1,
        grid=(m // tb, N_EXPERTS),
        in_specs=[
            pl.BlockSpec((tb, D_MODEL), lambda i, e, c: (i, 0)),
            pl.BlockSpec((tb, LANES), lambda i, e, c: (i, 0)),
            pl.BlockSpec((tb, LANES), lambda i, e, c: (i, 0)),
            pl.BlockSpec((n_win, SUBLANES, MOE_WIN), lambda i, e, c: (i, 0, 0)),
            pl.BlockSpec((None, D_MODEL, FF_PAD), wsel),
            pl.BlockSpec((None, D_MODEL, FF_PAD), wsel),
            pl.BlockSpec((None, FF_PAD, D_MODEL), wsel),
            pl.BlockSpec(memory_space=pl.ANY),
            pl.BlockSpec((1, D_MODEL), lambda i, e, c: (0, 0)),
        ],
        out_specs=pl.BlockSpec(memory_space=pl.ANY),
        scratch_shapes=[
            pltpu.VMEM((xs_rows, D_MODEL), BF16),
            pltpu.VMEM((xs_rows, LANES), F32),
            pltpu.VMEM((SORT_ROWS, D_MODEL), BF16),
            pltpu.VMEM((SORT_ROWS, LANES), F32),
            pltpu.VMEM((2, MOE_WIN, D_MODEL), F32),
            pltpu.SemaphoreType.DMA((2,)),
            pltpu.VMEM((2, MOE_WIN, D_MODEL), F32),
            pltpu.SemaphoreType.DMA((2,)),
            pltpu.SMEM((n_win * N_EXPERTS,), jnp.int32),
            pltpu.SMEM((n_win * N_EXPERTS,), jnp.int32),
            pltpu.SMEM((n_win * N_EXPERTS,), jnp.int32),
            pltpu.SMEM((N_EXPERTS,), jnp.int32),
            pltpu.SMEM((N_EXPERTS,), jnp.int32),
        ],
    )
    return pl.pallas_call(
        functools.partial(_moe_kernel, n_win=n_win, final=final),
        grid_spec=grid_spec,
        out_shape=jax.ShapeDtypeStruct((m, D_MODEL), F32),
        compiler_params=pltpu.CompilerParams(
            dimension_semantics=("arbitrary", "arbitrary"), vmem_limit_bytes=MOE_VMEM_LIMIT),
        name="moe",
    )(cnt, hn, gates, d, dt, wg, wu, wd, x1, gfin)


def kernel(x_prompt, x_sample, state_ssm_re, state_ssm_im, norm_mix, w_in, ssm_a_re, ssm_a_im, ssm_log_dt, ssm_b_re, ssm_b_im, ssm_c_re, ssm_c_im, ssm_d, glu_w, glu_b, sgu_w, sgu_b, sgu_ln_g, sgu_ln_b, out_norm_a, out_norm_b, w_out, norm_ffn, ffn_w_gate, ffn_w_up, ffn_w_down, router_w, router_b, moe_w_gate, moe_w_up, moe_w_down, norm_final):
    batch, seq = x_prompt.shape[0], x_prompt.shape[1]
    dec_batch, dec_seq = x_sample.shape[0], x_sample.shape[1]
    mp, ms = batch * seq, dec_batch * dec_seq
    n_moe = moe_w_gate.shape[0]

    lam, bbre, bbim = _s5_prep(ssm_a_re, ssm_a_im, ssm_log_dt, ssm_b_re, ssm_b_im)
    ctre = jnp.swapaxes(ssm_c_re, 2, 3).reshape(DEPTH * N_STATE, SSM_GROUP)
    ctim = jnp.swapaxes(ssm_c_im, 2, 3).reshape(DEPTH * N_STATE, SSM_GROUP)
    dskip = ssm_d.reshape(DEPTH, 1, MIX_A)
    glu = glu_w.reshape(DEPTH * MIX_A, 2 * SSM_GROUP)
    gbias = jnp.stack([glu_b[..., :SSM_GROUP].reshape(DEPTH, MIX_A),
                       glu_b[..., SSM_GROUP:].reshape(DEPTH, MIX_A)], axis=1)
    s5w = (lam, bbre, bbim, ctre, ctim, dskip, glu, gbias)
    w_in_b = w_in.astype(BF16)
    w_out_b = w_out.astype(BF16)
    g_mix = norm_mix.reshape(DEPTH, 1, D_MODEL)
    g_a = out_norm_a.reshape(DEPTH, 1, MIX_A)
    g_b = out_norm_b.reshape(DEPTH, 1, MIX_B)
    g_ffn = norm_ffn.reshape(DEPTH, 1, D_MODEL)
    sgu_bias = jnp.repeat(jnp.swapaxes(sgu_b, 1, 2), HD_B, axis=2)
    sgu_bias_s = sgu_bias[:, :dec_seq]
    lng = sgu_ln_g.reshape(DEPTH, 1, MIX_B)
    lnb = sgu_ln_b.reshape(DEPTH, 1, MIX_B)
    coef_s = jnp.repeat(
        jnp.transpose(sgu_w[:, :, :dec_seq, :dec_seq], (0, 2, 3, 1)).reshape(DEPTH, dec_seq * dec_seq, NH_B),
        HD_B, axis=2)
    dense_split = (FF_PAD, D_FF - FF_PAD)
    dense_g = _pack(jnp.swapaxes(ffn_w_gate, 1, 2), dense_split, transpose=True)
    dense_u = _pack(jnp.swapaxes(ffn_w_up, 1, 2), dense_split, transpose=True)
    dense_d = _pack(ffn_w_down, dense_split, transpose=False)
    moe_t = lambda w: jnp.swapaxes(w.reshape(-1, D_MODEL, D_FF_EXPERT), 1, 2)
    moe_g = _pack(moe_t(moe_w_gate), (D_FF_EXPERT,), transpose=True)
    moe_u = _pack(moe_t(moe_w_up), (D_FF_EXPERT,), transpose=True)
    moe_d = _pack(moe_w_down.reshape(-1, D_FF_EXPERT, D_MODEL), (D_FF_EXPERT,), transpose=False)
    rw = jnp.pad(router_w, ((0, 0), (0, 0), (0, LANES - N_EXPERTS)))
    rwh = rw.astype(BF16)
    rw2 = jnp.concatenate([rwh, (rw - rwh.astype(F32)).astype(BF16)], axis=-1)
    rb = jnp.pad(router_b, ((0, 0), (0, LANES - N_EXPERTS))).reshape(n_moe, 1, LANES)
    gfin = norm_final.reshape(1, D_MODEL)

    h0re = state_ssm_re.reshape(DEPTH, dec_batch, N_STATE)
    h0im = state_ssm_im.reshape(DEPTH, dec_batch, N_STATE)

    assert DEPTH % 2 == 0

    def post_mixer(x, ya, yb, l, tm, tb):
        common = (g_a, g_b, w_out_b, g_ffn)
        if l % 2 == 0:
            return _ffn(x, ya, yb, *common, dense_g, dense_u, dense_d, l=l, tm=tm)
        x1, hn, gates, d, dt, cnt = _pre_moe(x, ya, yb, *common, rw2, rb, l=l, tm=min(2 * tm, tb))
        cnt = cnt[:, 0, :N_EXPERTS].astype(jnp.int32).reshape(-1)
        return _moe(cnt, hn, gates, d, dt, moe_g, moe_u, moe_d, x1, gfin, l=l, tb=tb, final=(l == DEPTH - 1))

    xp = x_prompt.reshape(mp, D_MODEL)
    xs = x_sample.reshape(ms, D_MODEL)
    re_p, im_p, re_s, im_s, v_s = [], [], [], [], []
    for l in range(DEPTH):
        ya_p, yb_p, hre_p, him_p = _mixer(xp.reshape(batch, seq, D_MODEL), g_mix, w_in_b, sgu_w, sgu_bias,
                                          lng, lnb, s5w, l)
        zs = _inproj(xs, g_mix, w_in_b, l, tm=ms)
        ya_s, hre_s, him_s = _s5_sample(zs, h0re, h0im, s5w, l, nb=dec_batch, t_blk=dec_seq)
        yb_s, vn_s = _sgu_sample(zs, coef_s, sgu_bias_s, lng, lnb, l, nb=dec_batch, t_blk=dec_seq)
        xp = post_mixer(xp, ya_p.reshape(mp, MIX_A), yb_p.reshape(mp, MIX_B), l, tm=512, tb=seq)
        xs = post_mixer(xs, ya_s, yb_s, l, tm=ms, tb=ms)
        re_p.append(hre_p)
        im_p.append(him_p)
        re_s.append(hre_s)
        im_s.append(him_s)
        v_s.append(vn_s)

    st = lambda hs, b: jnp.stack(hs).reshape(DEPTH, b, NG_A, SSM_STATE)
    return (xp.reshape(batch, seq, D_MODEL), xs.reshape(dec_batch, dec_seq, D_MODEL),
            st(re_p, batch), st(im_p, batch), st(re_s, dec_batch), st(im_s, dec_batch),
            jnp.stack(v_s).reshape(DEPTH, dec_batch, dec_seq, NH_B, HD_B))
```
